```python
import jax
import jax.numpy as jnp
from jax import lax
import numpy as np

D_MODEL = 1024
BATCH = 4
SEQ = 4096
DEPTH = 4
DEC_BATCH = 128
DEC_SEQ = 1
PAST_LEN = 8192
PAGE_SIZE = 128

HEAD_DIM = 64
N_HEADS = D_MODEL // HEAD_DIM
Q_WIDTH = N_HEADS * HEAD_DIM
ROPE_THETA = 10000.0
N_MIXERS = 3
N_MOBA = (DEPTH + 2) // N_MIXERS
N_SWA = (DEPTH + 1) // N_MIXERS
N_NSA = DEPTH // N_MIXERS
MOBA_KV_HEADS = 4
MOBA_BLOCK = 256
MOBA_TOPK = 3
MOBA_QCHUNK = 16
MOBA_IN = Q_WIDTH + 2 * MOBA_KV_HEADS * HEAD_DIM
SWA_KV_HEADS = 2
SWA_WINDOW = 128
SWA_IN = Q_WIDTH + 2 * SWA_KV_HEADS * HEAD_DIM
NSA_KV_HEADS = 2
NSA_BLOCK = 64
NSA_TOPN = 16
NSA_WINDOW = 512
NSA_CMP_HIDDEN = 128
NSA_QCHUNK = 32
NSA_IN = Q_WIDTH + 6 * NSA_KV_HEADS * HEAD_DIM + 3 * N_HEADS
BAND_BLOCK = 128
D_FF = 2816
CONV_WIDTH = 3
DEEPNORM_ALPHA = (2 * DEPTH) ** 0.25
DEEPNORM_BETA = (8 * DEPTH) ** -0.25
LN_EPS = 1e-5
NEG_BIG = -1e30
ATTN_SCALE = HEAD_DIM ** -0.5

kernel_name = 'hybrid_moba_swa_nsa_convffn_step'


def layer_norm(x, g, b):
    xf = x.astype(jnp.float32)
    mu = jnp.mean(xf, -1, keepdims=True)
    var = jnp.mean(jnp.square(xf - mu), -1, keepdims=True)
    y = (xf - mu) * lax.rsqrt(var + LN_EPS)
    return (y * g.astype(jnp.float32) + b.astype(jnp.float32)).astype(x.dtype)


def adaln(c, w, b):
    m = (jax.nn.silu(c) @ w + b)[:, None, :]
    shift, scale, gate = jnp.split(m, 3, axis=-1)
    return shift, scale, gate


def rope(x, pos):
    half = HEAD_DIM // 2
    inv = ROPE_THETA ** (-jnp.arange(half, dtype=jnp.float32) / half)
    ang = pos.astype(jnp.float32)[:, None] * inv[None, :]
    cos = jnp.cos(ang)[:, None, :]
    sin = jnp.sin(ang)[:, None, :]
    xf = x.astype(jnp.float32)
    x1, x2 = xf[..., :half], xf[..., half:]
    return jnp.concatenate([x1 * cos - x2 * sin, x1 * sin + x2 * cos], -1).astype(x.dtype)


def masked_softmax(s, mask, sink=None):
    s = jnp.where(mask, s, NEG_BIG)
    m = jnp.max(s, -1, keepdims=True)
    if sink is not None:
        m = jnp.maximum(m, sink)
    p = jnp.exp(s - m) * mask
    den = jnp.sum(p, -1, keepdims=True)
    if sink is not None:
        den = den + jnp.exp(sink - m)
    return p / jnp.maximum(den, 1e-30)


def gqa_attend(q, k, v, mask, sink=None):
    kvh = k.shape[-2]
    g = q.shape[-2] // kvh
    qg = q.reshape(q.shape[:-2] + (kvh, g, HEAD_DIM))
    s = jnp.einsum('...qhgd,...khd->...hgqk', qg, k, preferred_element_type=jnp.float32) * ATTN_SCALE
    sk = None if sink is None else sink.astype(jnp.float32).reshape(kvh, g, 1, 1)
    p = masked_softmax(s, mask[..., None, None, :, :], sk)
    o = jnp.einsum('...hgqk,...khd->...qhgd', p.astype(v.dtype), v)
    return o.reshape(q.shape)


def map_query_chunks(fn, chunk, q_pos, *xs):
    n_q = q_pos.shape[0]
    c = chunk if n_q % chunk == 0 else n_q
    n = n_q // c
    split = lambda a: a.reshape((a.shape[0], n, c) + a.shape[2:]).swapaxes(0, 1)
    out = lax.map(lambda args: fn(*args), (q_pos.reshape(n, c),) + tuple(split(a) for a in xs))
    return out.swapaxes(0, 1).reshape((out.shape[1], n_q) + out.shape[3:])


def band_attend_prompt(q, k, v, window, sink=None):
    s_len = q.shape[1]
    pad = -(-window // BAND_BLOCK) * BAND_BLOCK
    padw = ((0, 0), (pad, 0), (0, 0), (0, 0))
    kp, vp = jnp.pad(k, padw), jnp.pad(v, padw)
    span = pad + BAND_BLOCK

    def block(pos, qi):
        start = pos[0]
        ki = lax.dynamic_slice_in_dim(kp, start, span, axis=1)
        vi = lax.dynamic_slice_in_dim(vp, start, span, axis=1)
        kpos = start - pad + jnp.arange(span)
        d = pos[:, None] - kpos[None, :]
        mask = (d >= 0) & (d <= window) & (kpos >= 0)[None, :]
        return gqa_attend(qi, ki, vi, mask, sink)

    return map_query_chunks(block, BAND_BLOCK, jnp.arange(s_len, dtype=jnp.int32), q)


def band_attend_sample(q, k, v, buf, past_len, window, sink=None):
    wb, s_len = buf.shape[1], q.shape[1]
    kk = jnp.concatenate([buf[:, :, 0], k], 1)
    vv = jnp.concatenate([buf[:, :, 1], v], 1)
    qpos = past_len + jnp.arange(s_len)
    kpos = past_len - wb + jnp.arange(wb + s_len)
    d = qpos[:, None] - kpos[None, :]
    o = gqa_attend(q, kk, vv, (d >= 0) & (d <= window), sink)
    return o, jnp.stack([kk, vv], 2)[:, -wb:]


def gather_pages(cache, layer, page_table):
    g = cache[layer, page_table]
    return g.reshape((g.shape[0], g.shape[1] * g.shape[2]) + g.shape[3:])


def split_heads(z, n):
    return z.reshape(z.shape[:2] + (n, HEAD_DIM))


def moba_attend(q, k, v, q_pos):
    b, n_keys, kvh, _ = k.shape
    g = N_HEADS // kvh
    nb = -(-n_keys // MOBA_BLOCK)
    padw = ((0, 0), (0, nb * MOBA_BLOCK - n_keys), (0, 0), (0, 0))
    kb = jnp.pad(k, padw).reshape(b, nb, MOBA_BLOCK, kvh, HEAD_DIM)
    vb = jnp.pad(v, padw).reshape(b, nb, MOBA_BLOCK, kvh, HEAD_DIM)
    k_mean = jnp.mean(kb.astype(jnp.float32), axis=2)
    kb = kb.transpose(0, 3, 1, 2, 4)
    vb = vb.transpose(0, 3, 1, 2, 4)
    n_top = min(MOBA_TOPK, nb)
    bi = jnp.arange(b)[:, None, None, None, None]
    hi = jnp.arange(kvh)[None, None, :, None, None]
    offs = jnp.arange(MOBA_BLOCK)
    blk = jnp.arange(nb)

    def chunk(pos, qc):
        c = qc.shape[1]
        qg = qc.reshape(b, c, kvh, g, HEAD_DIM)
        own = pos // MOBA_BLOCK
        gate = jnp.einsum('bqhgd,bnhd->bqhgn', qg.astype(jnp.float32), k_mean)
        gate = jnp.where((blk[None, :] < own[:, None])[None, :, None, None, :], gate, -jnp.inf)
        top_s, top_i = lax.top_k(gate, n_top)
        own_i = jnp.broadcast_to(own[None, :, None, None, None], top_i.shape[:-1] + (1,)).astype(top_i.dtype)
        sel = jnp.concatenate([top_i, own_i], -1)
        ok = jnp.concatenate([top_s > -jnp.inf, jnp.ones(own_i.shape, bool)], -1)
        n_sel = sel.shape[-1]
        ks = kb[bi, hi, sel].reshape(b, c, kvh, g, n_sel * MOBA_BLOCK, HEAD_DIM)
        vs = vb[bi, hi, sel].reshape(b, c, kvh, g, n_sel * MOBA_BLOCK, HEAD_DIM)
        kpos = sel[..., None] * MOBA_BLOCK + offs
        mask = (ok[..., None] & (kpos <= pos[None, :, None, None, None, None])).reshape(b, c, kvh, g, n_sel * MOBA_BLOCK)
        s = jnp.einsum('bqhgd,bqhgsd->bqhgs', qg, ks, preferred_element_type=jnp.float32) * ATTN_SCALE
        p = masked_softmax(s, mask)
        o = jnp.einsum('bqhgs,bqhgsd->bqhgd', p.astype(vs.dtype), vs)
        return o.reshape(b, c, N_HEADS, HEAD_DIM)

    return map_query_chunks(chunk, MOBA_QCHUNK, q_pos, q)


def moba_mixer(h, pos, w_in, w_o, past_kv=None):
    b, s_len, _ = h.shape
    q, k, v = jnp.split(h @ w_in, [Q_WIDTH, Q_WIDTH + MOBA_KV_HEADS * HEAD_DIM], -1)
    q = rope(split_heads(q, N_HEADS), pos)
    k = rope(split_heads(k, MOBA_KV_HEADS), pos)
    v = split_heads(v, MOBA_KV_HEADS)
    if past_kv is None:
        k_all, v_all = k, v
    else:
        k_all = jnp.concatenate([past_kv[:, :, 0], k], 1)
        v_all = jnp.concatenate([past_kv[:, :, 1], v], 1)
    o = moba_attend(q, k_all, v_all, pos)
    return o.reshape(b, s_len, Q_WIDTH) @ w_o, jnp.stack([k, v], 2)


def swa_mixer(h, pos, w_in, w_o, sink, buf=None, past_len=0):
    b, s_len, _ = h.shape
    q, k, v = jnp.split(h @ w_in, [Q_WIDTH, Q_WIDTH + SWA_KV_HEADS * HEAD_DIM], -1)
    q = rope(split_heads(q, N_HEADS), pos)
    k = rope(split_heads(k, SWA_KV_HEADS), pos)
    v = split_heads(v, SWA_KV_HEADS)
    if buf is None:
        o = band_attend_prompt(q, k, v, SWA_WINDOW, sink)
        new_buf = jnp.stack([k, v], 2)[:, -min(SWA_WINDOW, s_len):]
    else:
        o, new_buf = band_attend_sample(q, k, v, buf, past_len, SWA_WINDOW, sink)
    return o.reshape(b, s_len, Q_WIDTH) @ w_o, new_buf


def nsa_compress(x, pe, w1, w2):
    b, n_keys, kvh, _ = x.shape
    nb = -(-n_keys // NSA_BLOCK)
    xb = jnp.pad(x, ((0, 0), (0, nb * NSA_BLOCK - n_keys), (0, 0), (0, 0)))
    xb = xb.reshape(b, nb, NSA_BLOCK, kvh, HEAD_DIM) + pe[None, None, :, None, :]
    xb = xb.transpose(0, 1, 3, 2, 4).reshape(b, nb, kvh, NSA_BLOCK * HEAD_DIM)
    return jax.nn.gelu(xb @ w1) @ w2


def nsa_global(q, q_rot, k_cmp, v_cmp, k_slc, v_slc, q_pos, pe_k, pe_v, w1_k, w2_k, w1_v, w2_v):
    b, n_q = q.shape[:2]
    n_keys, kvh = k_slc.shape[1], k_slc.shape[2]
    g = N_HEADS // kvh
    nb = -(-n_keys // NSA_BLOCK)
    blk = jnp.arange(nb)
    kc = nsa_compress(k_cmp, pe_k, w1_k, w2_k)
    vc = nsa_compress(v_cmp, pe_v, w1_v, w2_v)
    qg = q.reshape(b, n_q, kvh, g, HEAD_DIM)
    avail = (blk[None, :] + 1) * NSA_BLOCK - 1 <= q_pos[:, None]
    s = jnp.einsum('bqhgd,bnhd->bqhgn', qg, kc, preferred_element_type=jnp.float32) * ATTN_SCALE
    p_cmp = masked_softmax(s, avail[None, :, None, None, :])
    o_cmp = jnp.einsum('bqhgn,bnhd->bqhgd', p_cmp.astype(vc.dtype), vc).reshape(q.shape)
    own = q_pos // NSA_BLOCK
    forced = (blk[None, :] == 0) | (blk[None, :] == own[:, None]) | (blk[None, :] == own[:, None] - 1)
    score = jnp.where(avail[None, :, None, :], jnp.sum(p_cmp, 3), -jnp.inf)
    score = jnp.where(forced[None, :, None, :], jnp.inf, score)
    padw = ((0, 0), (0, nb * NSA_BLOCK - n_keys), (0, 0), (0, 0))
    ks_t = jnp.pad(k_slc, padw).reshape(b, nb, NSA_BLOCK, kvh, HEAD_DIM).transpose(0, 3, 1, 2, 4)
    vs_t = jnp.pad(v_slc, padw).reshape(b, nb, NSA_BLOCK, kvh, HEAD_DIM).transpose(0, 3, 1, 2, 4)
    n_top = min(NSA_TOPN, nb)
    bi = jnp.arange(b)[:, None, None, None]
    hi = jnp.arange(kvh)[None, None, :, None]
    offs = jnp.arange(NSA_BLOCK)

    def chunk(pos, qc, sc):
        c = qc.shape[1]
        top_s, sel = lax.top_k(sc, n_top)
        ks = ks_t[bi, hi, sel].reshape(b, c, kvh, n_top * NSA_BLOCK, HEAD_DIM)
        vs = vs_t[bi, hi, sel].reshape(b, c, kvh, n_top * NSA_BLOCK, HEAD_DIM)
        kpos = sel[..., None] * NSA_BLOCK + offs
        mask = ((top_s > -jnp.inf)[..., None] & (kpos <= pos[None, :, None, None, None])).reshape(b, c, kvh, 1, n_top * NSA_BLOCK)
        qgc = qc.reshape(b, c, kvh, g, HEAD_DIM)
        s_c = jnp.einsum('bqhgd,bqhsd->bqhgs', qgc, ks, preferred_element_type=jnp.float32) * ATTN_SCALE
        p = masked_softmax(s_c, mask)
        o = jnp.einsum('bqhgs,bqhsd->bqhgd', p.astype(vs.dtype), vs)
        return o.reshape(b, c, N_HEADS, HEAD_DIM)

    o_slc = map_query_chunks(chunk, NSA_QCHUNK, q_pos, q_rot, score)
    return o_cmp, o_slc


def nsa_mixer(h, pos, w_in, w_o, pe_k, pe_v, w1_k, w2_k, w1_v, w2_v, past=None, win_buf=None, past_len=0):
    b, s_len, _ = h.shape
    kvw = NSA_KV_HEADS * HEAD_DIM
    q, kc, vc, ks, vs, kw, vw, gl = jnp.split(h @ w_in, [Q_WIDTH + i * kvw for i in range(7)], -1)
    q = split_heads(q, N_HEADS)
    q_rot = rope(q, pos)
    kc, vc, vs, vw = (split_heads(a, NSA_KV_HEADS) for a in (kc, vc, vs, vw))
    ks = rope(split_heads(ks, NSA_KV_HEADS), pos)
    kw = rope(split_heads(kw, NSA_KV_HEADS), pos)
    gates = jax.nn.sigmoid(gl.astype(jnp.float32)).astype(h.dtype).reshape(b, s_len, N_HEADS, 3)
    rows = jnp.stack([kc, vc, ks, vs], 2)
    if past is None:
        kc_all, vc_all, ks_all, vs_all = kc, vc, ks, vs
    else:
        kc_all, vc_all, ks_all, vs_all = (jnp.concatenate([past[:, :, i], a], 1) for i, a in enumerate((kc, vc, ks, vs)))
    o_cmp, o_slc = nsa_global(q, q_rot, kc_all, vc_all, ks_all, vs_all, pos, pe_k, pe_v, w1_k, w2_k, w1_v, w2_v)
    if win_buf is None:
        o_win = band_attend_prompt(q_rot, kw, vw, NSA_WINDOW)
        new_win = jnp.stack([kw, vw], 2)[:, -min(NSA_WINDOW, s_len):]
    else:
        o_win, new_win = band_attend_sample(q_rot, kw, vw, win_buf, past_len, NSA_WINDOW)
    o = gates[..., 0:1] * o_cmp + gates[..., 1:2] * o_slc + gates[..., 2:3] * o_win
    return o.reshape(b, s_len, Q_WIDTH) @ w_o, rows, new_win


def conv_ffn(h, prev, w_gate, w_up, conv_w, conv_b, w_down):
    s_len = h.shape[1]
    ext = jnp.concatenate([prev, h @ w_gate], 1)
    conv = conv_b + conv_w[0] * ext[:, 0:s_len]
    for j in range(1, CONV_WIDTH):
        conv = conv + conv_w[j] * ext[:, j:j + s_len]
    out = (jax.nn.silu(conv) * (h @ w_up)) @ w_down
    return out, ext[:, s_len:]


def setup_inputs(seed: int = 0) -> dict:
    key = jax.random.key(seed)
    keys = iter(jax.random.split(key, 48))

    def nrm(shape, scale=1.0):
        return scale * jax.random.normal(next(keys), shape, jnp.float32)

    n_pages = PAST_LEN // PAGE_SIZE
    n_phys = (DEC_BATCH * n_pages * 5) // 4
    perm = jax.random.permutation(next(keys), n_phys)
    page_table = perm[:DEC_BATCH * n_pages].reshape(DEC_BATCH, n_pages).astype(jnp.int32)
    d = D_MODEL
    return {
        'x_prompt': nrm((BATCH, SEQ, d)),
        'x_sample': nrm((DEC_BATCH, DEC_SEQ, d)),
        'cache_moba_kv': nrm((N_MOBA, n_phys, PAGE_SIZE, 2, MOBA_KV_HEADS, HEAD_DIM)),
        'state_swa_kv': nrm((N_SWA, DEC_BATCH, min(SWA_WINDOW, PAST_LEN), 2, SWA_KV_HEADS, HEAD_DIM)),
        'cache_nsa_kv': nrm((N_NSA, n_phys, PAGE_SIZE, 4, NSA_KV_HEADS, HEAD_DIM)),
        'state_nsa_win_kv': nrm((N_NSA, DEC_BATCH, min(NSA_WINDOW, PAST_LEN), 2, NSA_KV_HEADS, HEAD_DIM)),
        'state_ffn_conv': nrm((DEPTH, DEC_BATCH, CONV_WIDTH - 1, D_FF)),
        'page_table': page_table,
        'c_prompt': nrm((BATCH, d)),
        'c_sample': nrm((DEC_BATCH, d)),
        'ada_w': nrm((DEPTH, 2, d, 3 * d), d ** -0.5),
        'ada_b': nrm((DEPTH, 2, 3 * d), 0.02),
        'ln_g': 1.0 + nrm((DEPTH, 2, d), 0.1),
        'ln_b': nrm((DEPTH, 2, d), 0.02),
        'moba_w_in': nrm((N_MOBA, d, MOBA_IN), d ** -0.5),
        'moba_w_o': nrm((N_MOBA, Q_WIDTH, d), DEEPNORM_BETA * Q_WIDTH ** -0.5),
        'swa_w_in': nrm((N_SWA, d, SWA_IN), d ** -0.5),
        'swa_w_o': nrm((N_SWA, Q_WIDTH, d), DEEPNORM_BETA * Q_WIDTH ** -0.5),
        'swa_sink': nrm((N_SWA, N_HEADS), 1.0),
        'nsa_w_in': nrm((N_NSA, d, NSA_IN), d ** -0.5),
        'nsa_w_o': nrm((N_NSA, Q_WIDTH, d), DEEPNORM_BETA * Q_WIDTH ** -0.5),
        'nsa_pe_k': nrm((N_NSA, NSA_BLOCK, HEAD_DIM), 0.5),
        'nsa_pe_v': nrm((N_NSA, NSA_BLOCK, HEAD_DIM), 0.5),
        'nsa_w1_k': nrm((N_NSA, NSA_BLOCK * HEAD_DIM, NSA_CMP_HIDDEN), (NSA_BLOCK * HEAD_DIM) ** -0.5),
        'nsa_w2_k': nrm((N_NSA, NSA_CMP_HIDDEN, HEAD_DIM), 2.0 * NSA_CMP_HIDDEN ** -0.5),
        'nsa_w1_v': nrm((N_NSA, NSA_BLOCK * HEAD_DIM, NSA_CMP_HIDDEN), (NSA_BLOCK * HEAD_DIM) ** -0.5),
        'nsa_w2_v': nrm((N_NSA, NSA_CMP_HIDDEN, HEAD_DIM), 2.0 * NSA_CMP_HIDDEN ** -0.5),
        'ffn_w_gate': nrm((DEPTH, d, D_FF), d ** -0.5),
        'ffn_w_up': nrm((DEPTH, d, D_FF), d ** -0.5),
        'ffn_conv_w': nrm((DEPTH, CONV_WIDTH, D_FF), CONV_WIDTH ** -0.5),
        'ffn_conv_b': nrm((DEPTH, D_FF), 0.02),
        'ffn_w_down': nrm((DEPTH, D_FF, d), DEEPNORM_BETA * D_FF ** -0.5),
    }


def reference(x_prompt, x_sample, cache_moba_kv, state_swa_kv, cache_nsa_kv, state_nsa_win_kv, state_ffn_conv,
              page_table, c_prompt, c_sample, ada_w, ada_b, ln_g, ln_b, moba_w_in, moba_w_o, swa_w_in, swa_w_o,
              swa_sink, nsa_w_in, nsa_w_o, nsa_pe_k, nsa_pe_v, nsa_w1_k, nsa_w2_k, nsa_w1_v, nsa_w2_v,
              ffn_w_gate, ffn_w_up, ffn_conv_w, ffn_conv_b, ffn_w_down):
    past_len = page_table.shape[1] * PAGE_SIZE
    pos_p = jnp.arange(x_prompt.shape[1], dtype=jnp.int32)
    pos_s = past_len + jnp.arange(x_sample.shape[1], dtype=jnp.int32)
    xp, xs = x_prompt, x_sample
    moba_p, moba_s, swa_p, swa_s, nsa_p, nsa_s, nsaw_p, nsaw_s, conv_p, conv_s = ([] for _ in range(10))
    for i in range(DEPTH):
        kind, j = i % N_MIXERS, i // N_MIXERS
        sh_p, sc_p, gt_p = adaln(c_prompt, ada_w[i, 0], ada_b[i, 0])
        sh_s, sc_s, gt_s = adaln(c_sample, ada_w[i, 0], ada_b[i, 0])
        hp = xp * (1 + sc_p) + sh_p
        hs = xs * (1 + sc_s) + sh_s
        if kind == 0:
            yp, r_p = moba_mixer(hp, pos_p, moba_w_in[j], moba_w_o[j])
            ys, r_s = moba_mixer(hs, pos_s, moba_w_in[j], moba_w_o[j], gather_pages(cache_moba_kv, j, page_table))
            moba_p.append(r_p)
            moba_s.append(r_s)
        elif kind == 1:
            yp, r_p = swa_mixer(hp, pos_p, swa_w_in[j], swa_w_o[j], swa_sink[j])
            ys, r_s = swa_mixer(hs, pos_s, swa_w_in[j], swa_w_o[j], swa_sink[j], state_swa_kv[j], past_len)
            swa_p.append(r_p)
            swa_s.append(r_s)
        else:
            nsa_w = (nsa_w_in[j], nsa_w_o[j], nsa_pe_k[j], nsa_pe_v[j], nsa_w1_k[j], nsa_w2_k[j], nsa_w1_v[j], nsa_w2_v[j])
            yp, r_p, w_p = nsa_mixer(hp, pos_p, *nsa_w)
            ys, r_s, w_s = nsa_mixer(hs, pos_s, *nsa_w, gather_pages(cache_nsa_kv, j, page_table),
                                     state_nsa_win_kv[j], past_len)
            nsa_p.append(r_p)
            nsa_s.append(r_s)
            nsaw_p.append(w_p)
            nsaw_s.append(w_s)
        xp = layer_norm(DEEPNORM_ALPHA * xp + (1 + gt_p) * yp, ln_g[i, 0], ln_b[i, 0])
        xs = layer_norm(DEEPNORM_ALPHA * xs + (1 + gt_s) * ys, ln_g[i, 0], ln_b[i, 0])
        sh_p, sc_p, gt_p = adaln(c_prompt, ada_w[i, 1], ada_b[i, 1])
        sh_s, sc_s, gt_s = adaln(c_sample, ada_w[i, 1], ada_b[i, 1])
        hp = xp * (1 + sc_p) + sh_p
        hs = xs * (1 + sc_s) + sh_s
        ffn_w = (ffn_w_gate[i], ffn_w_up[i], ffn_conv_w[i], ffn_conv_b[i], ffn_w_down[i])
        fp, cp = conv_ffn(hp, jnp.zeros((hp.shape[0], CONV_WIDTH - 1, D_FF), hp.dtype), *ffn_w)
        fs, cs = conv_ffn(hs, state_ffn_conv[i], *ffn_w)
        conv_p.append(cp)
        conv_s.append(cs)
        xp = layer_norm(DEEPNORM_ALPHA * xp + (1 + gt_p) * fp, ln_g[i, 1], ln_b[i, 1])
        xs = layer_norm(DEEPNORM_ALPHA * xs + (1 + gt_s) * fs, ln_g[i, 1], ln_b[i, 1])
    return (xp, xs, jnp.stack(moba_p), jnp.stack(moba_s), jnp.stack(swa_p), jnp.stack(swa_s),
            jnp.stack(nsa_p), jnp.stack(nsa_s), jnp.stack(nsaw_p), jnp.stack(nsaw_s),
            jnp.stack(conv_p), jnp.stack(conv_s))
```

```python
import functools

import jax
import jax.numpy as jnp
from jax import lax
from jax.experimental import pallas as pl
from jax.experimental.pallas import tpu as pltpu

F32 = jnp.float32
BF16 = jnp.bfloat16

HEAD_DIM = 64
N_HEADS = 16
Q_WIDTH = N_HEADS * HEAD_DIM
ROPE_THETA = 10000.0
N_MIXERS = 3
PAGE_SIZE = 128
MOBA_KV_HEADS = 4
MOBA_BLOCK = 256
MOBA_TOPK = 3
SWA_KV_HEADS = 2
SWA_WINDOW = 128
NSA_KV_HEADS = 2
NSA_BLOCK = 64
NSA_TOPN = 16
NSA_WINDOW = 512
NSA_CMP_HIDDEN = 128
CONV_WIDTH = 3
LN_EPS = 1e-5
ATTN_SCALE = HEAD_DIM ** -0.5

LANES = 128
SUBLANES = 8
TQ = 256
MASKED = -1e30
GATE_FLOOR = -3e38
VMEM_LIMIT = 56 * 1024 * 1024


def _cparams(n_axes):
    return pltpu.CompilerParams(dimension_semantics=("arbitrary",) * n_axes,
                                vmem_limit_bytes=VMEM_LIMIT)


def _dot(a, b):
    return jnp.dot(a, b, preferred_element_type=F32)


def _dot_nt(a, b, precision=None):
    return lax.dot_general(a, b, (((1,), (1,)), ((), ())), precision=precision,
                           preferred_element_type=F32)


def _iota(shape, axis):
    return lax.broadcasted_iota(jnp.int32, shape, axis)


def _adaln_kernel(c_ref, w_ref, b_ref, o_ref):
    c = c_ref[...]
    a = c * (1.0 / (1.0 + jnp.exp(-c)))
    o_ref[0] = _dot(a.astype(BF16), w_ref[0].astype(BF16)) + b_ref[0]


def _adaln_all(c_all, ada_w, ada_b):
    n_sub, d, d3 = ada_w.shape
    rows = c_all.shape[0]
    tn = 1024
    return pl.pallas_call(
        _adaln_kernel,
        grid=(n_sub, d3 // tn),
        in_specs=[pl.BlockSpec((rows, d), lambda l, j: (0, 0)),
                  pl.BlockSpec((1, d, tn), lambda l, j: (l, 0, j)),
                  pl.BlockSpec((1, 1, tn), lambda l, j: (l, 0, j))],
        out_specs=pl.BlockSpec((1, rows, tn), lambda l, j: (l, 0, j)),
        out_shape=jax.ShapeDtypeStruct((n_sub, rows, d3), F32),
        compiler_params=_cparams(2),
        name="adaln",
    )(c_all, ada_w, ada_b)


def _rope_chunk(y, cos, sin_signed):
    lo = (_iota(y.shape, 1) % HEAD_DIM) < (HEAD_DIM // 2)
    swapped = jnp.where(lo, pltpu.roll(y, LANES - HEAD_DIM // 2, 1), pltpu.roll(y, HEAD_DIM // 2, 1))
    return y * cos + swapped * sin_signed


def _inproj_kernel(x_ref, sh_ref, sc_ref, w_ref, cos_ref, sin_ref, *out_refs, outs):
    h = x_ref[0] * (1.0 + sc_ref[0]) + sh_ref[0]
    y = _dot(h.astype(BF16), w_ref[...])
    cos = cos_ref[...]
    sin = sin_ref[...]
    for o_ref, segs in zip(out_refs, outs):
        dst = 0
        for (start, width, op) in segs:
            for c in range(0, width, LANES):
                blk = y[:, start + c:start + c + LANES]
                if op == "rope":
                    blk = _rope_chunk(blk, cos, sin)
                elif op == "sigmoid":
                    blk = 1.0 / (1.0 + jnp.exp(-blk))
                o_ref[0, :, dst + c:dst + c + LANES] = blk
            dst += width


def _inproj(x, shift, scale, w, cos, sin, outs, tm):
    g, r, d = x.shape
    n = w.shape[1]
    rm = shift.shape[1]
    mod_spec = (pl.BlockSpec((1, 1, d), lambda a, b: (a, 0, 0)) if rm == 1
                else pl.BlockSpec((1, tm, d), lambda a, b: (a, b, 0)))
    widths = [sum(s[1] for s in segs) for segs in outs]
    return pl.pallas_call(
        functools.partial(_inproj_kernel, outs=outs),
        grid=(g, r // tm),
        in_specs=[pl.BlockSpec((1, tm, d), lambda a, b: (a, b, 0)), mod_spec, mod_spec,
                  pl.BlockSpec((d, n), lambda a, b: (0, 0)),
                  pl.BlockSpec((tm, LANES), lambda a, b: (b, 0)),
                  pl.BlockSpec((tm, LANES), lambda a, b: (b, 0))],
        out_specs=[pl.BlockSpec((1, tm, wd), lambda a, b: (a, b, 0)) for wd in widths],
        out_shape=[jax.ShapeDtypeStruct((g, r, wd), F32) for wd in widths],
        compiler_params=_cparams(2),
        name="inproj",
    )(x, shift, scale, w, cos, sin)


def _residual_ln(x, y, gate, g, b, alpha):
    z = alpha * x + (1.0 + gate) * y
    mu = jnp.mean(z, -1, keepdims=True)
    zc = z - mu
    var = jnp.mean(zc * zc, -1, keepdims=True)
    return zc * lax.rsqrt(var + LN_EPS) * g + b


def _outproj_kernel(o_ref, w_ref, x_ref, gt_ref, g_ref, b_ref, out_ref, *, alpha):
    y = _dot(o_ref[0].astype(BF16), w_ref[...])
    out_ref[0] = _residual_ln(x_ref[0], y, gt_ref[0], g_ref[...], b_ref[...], alpha)


def _outproj_ln(o, w, x, gate, ln_g, ln_b, alpha, tm):
    g, r, d = x.shape
    k = o.shape[2]
    rm = gate.shape[1]
    mod_spec = (pl.BlockSpec((1, 1, d), lambda a, b: (a, 0, 0)) if rm == 1
                else pl.BlockSpec((1, tm, d), lambda a, b: (a, b, 0)))
    return pl.pallas_call(
        functools.partial(_outproj_kernel, alpha=alpha),
        grid=(g, r // tm),
        in_specs=[pl.BlockSpec((1, tm, k), lambda a, b: (a, b, 0)),
                  pl.BlockSpec((k, d), lambda a, b: (0, 0)),
                  pl.BlockSpec((1, tm, d), lambda a, b: (a, b, 0)), mod_spec,
                  pl.BlockSpec((1, d), lambda a, b: (0, 0)),
                  pl.BlockSpec((1, d), lambda a, b: (0, 0))],
        out_specs=pl.BlockSpec((1, tm, d), lambda a, b: (a, b, 0)),
        out_shape=jax.ShapeDtypeStruct((g, r, d), F32),
        compiler_params=_cparams(2),
        name="outproj_ln",
    )(o, w, x, gate, ln_g, ln_b)


def _silu(x):
    return x * (1.0 / (1.0 + jnp.exp(-x)))


def _ffn_seq_kernel(x_ref, xh_ref, sh_ref, sc_ref, gt_ref, prev_ref, wg_ref, wu_ref, cw_ref, cb_ref,
                    wd_ref, g_ref, b_ref, out_ref, st_ref, acc_ref, *, alpha, n_f):
    r = pl.program_id(1)
    f = pl.program_id(2)
    x = x_ref[0]
    tm = x.shape[0]
    sc = 1.0 + sc_ref[0]
    sh = sh_ref[0]
    h = (x * sc + sh).astype(BF16)
    hh = (xh_ref[0] * sc + sh).astype(BF16)
    wg = wg_ref[...]
    gcur = _dot(h, wg)
    ghalo = _dot(hh, wg)
    first = r == 0
    prev = prev_ref[0]
    pm1 = jnp.where(first, prev[1:2], ghalo[SUBLANES - 1:SUBLANES])
    pm2 = jnp.where(first, prev[0:1], ghalo[SUBLANES - 2:SUBLANES - 1])
    row = _iota(gcur.shape, 0)
    g1 = jnp.where(row == 0, pm1, pltpu.roll(gcur, 1, 0))
    g2 = jnp.where(row == 0, pm2, jnp.where(row == 1, pm1, pltpu.roll(gcur, 2, 0)))
    cw = cw_ref[...]
    conv = cb_ref[...] + cw[0:1] * g2 + cw[1:2] * g1 + cw[2:3] * gcur
    act = _silu(conv) * _dot(h, wu_ref[...])
    part = _dot(act.astype(BF16), wd_ref[...])
    st_ref[0, 0] = gcur[tm - 2:tm]

    @pl.when(f == 0)
    def _():
        acc_ref[...] = part

    @pl.when(f != 0)
    def _():
        acc_ref[...] += part

    @pl.when(f == n_f - 1)
    def _():
        out_ref[0] = _residual_ln(x, acc_ref[...], gt_ref[0], g_ref[...], b_ref[...], alpha)


def _ffn_seq(x, shift, scale, gate, prev, wg, wu, cw, cb, wd, ln_g, ln_b, alpha, tm, tf):
    g, r, d = x.shape
    ff = wg.shape[1]
    n_f = ff // tf
    n_r = r // tm
    hb = tm // SUBLANES
    mod = pl.BlockSpec((1, 1, d), lambda a, b, c: (a, 0, 0))
    out, st = pl.pallas_call(
        functools.partial(_ffn_seq_kernel, alpha=alpha, n_f=n_f),
        grid=(g, n_r, n_f),
        in_specs=[pl.BlockSpec((1, tm, d), lambda a, b, c: (a, b, 0)),
                  pl.BlockSpec((1, SUBLANES, d), lambda a, b, c: (a, jnp.maximum(b * hb - 1, 0), 0)),
                  mod, mod, mod,
                  pl.BlockSpec((1, 2, tf), lambda a, b, c: (a, 0, c)),
                  pl.BlockSpec((d, tf), lambda a, b, c: (0, c)),
                  pl.BlockSpec((d, tf), lambda a, b, c: (0, c)),
                  pl.BlockSpec((CONV_WIDTH, tf), lambda a, b, c: (0, c)),
                  pl.BlockSpec((1, tf), lambda a, b, c: (0, c)),
                  pl.BlockSpec((tf, d), lambda a, b, c: (c, 0)),
                  pl.BlockSpec((1, d), lambda a, b, c: (0, 0)),
                  pl.BlockSpec((1, d), lambda a, b, c: (0, 0))],
        out_specs=[pl.BlockSpec((1, tm, d), lambda a, b, c: (a, b, 0)),
                   pl.BlockSpec((1, 1, 2, tf), lambda a, b, c: (a, b, 0, c))],
        out_shape=[jax.ShapeDtypeStruct((g, r, d), F32),
                   jax.ShapeDtypeStruct((g, n_r, 2, ff), F32)],
        scratch_shapes=[pltpu.VMEM((tm, d), F32)],
        compiler_params=_cparams(3),
        name="ffn_seq",
    )(x, x, shift, scale, gate, prev, wg, wu, cw, cb, wd, ln_g, ln_b)
    return out, st[:, n_r - 1]


def _ffn_step_kernel(x_ref, sh_ref, sc_ref, gt_ref, p0_ref, p1_ref, wg_ref, wu_ref, cw_ref, cb_ref,
                     wd_ref, g_ref, b_ref, out_ref, st_ref, acc_ref, *, alpha, n_f):
    f = pl.program_id(0)
    x = x_ref[...]
    h = (x * (1.0 + sc_ref[...]) + sh_ref[...]).astype(BF16)
    gcur = _dot(h, wg_ref[...])
    cw = cw_ref[...]
    conv = cb_ref[...] + cw[0:1] * p0_ref[...] + cw[1:2] * p1_ref[...] + cw[2:3] * gcur
    act = _silu(conv) * _dot(h, wu_ref[...])
    part = _dot(act.astype(BF16), wd_ref[...])
    st_ref[...] = gcur

    @pl.when(f == 0)
    def _():
        acc_ref[...] = part

    @pl.when(f != 0)
    def _():
        acc_ref[...] += part

    @pl.when(f == n_f - 1)
    def _():
        out_ref[...] = _residual_ln(x, acc_ref[...], gt_ref[...], g_ref[...], b_ref[...], alpha)


def _ffn_step(x, shift, scale, gate, p0, p1, wg, wu, cw, cb, wd, ln_g, ln_b, alpha, tf):
    r, d = x.shape
    ff = wg.shape[1]
    n_f = ff // tf
    full = pl.BlockSpec((r, d), lambda c: (0, 0))
    return pl.pallas_call(
        functools.partial(_ffn_step_kernel, alpha=alpha, n_f=n_f),
        grid=(n_f,),
        in_specs=[full, full, full, full,
                  pl.BlockSpec((r, tf), lambda c: (0, c)),
                  pl.BlockSpec((r, tf), lambda c: (0, c)),
                  pl.BlockSpec((d, tf), lambda c: (0, c)),
                  pl.BlockSpec((d, tf), lambda c: (0, c)),
                  pl.BlockSpec((CONV_WIDTH, tf), lambda c: (0, c)),
                  pl.BlockSpec((1, tf), lambda c: (0, c)),
                  pl.BlockSpec((tf, d), lambda c: (c, 0)),
                  pl.BlockSpec((1, d), lambda c: (0, 0)),
                  pl.BlockSpec((1, d), lambda c: (0, 0))],
        out_specs=[full, pl.BlockSpec((r, tf), lambda c: (0, c))],
        out_shape=[jax.ShapeDtypeStruct((r, d), F32), jax.ShapeDtypeStruct((r, ff), F32)],
        scratch_shapes=[pltpu.VMEM((r, d), F32)],
        compiler_params=_cparams(1),
        name="ffn_step",
    )(x, shift, scale, gate, p0, p1, wg, wu, cw, cb, wd, ln_g, ln_b)


def _head_slot(x, h):
    pair = x[:, (h // 2) * LANES:(h // 2 + 1) * LANES]
    return pltpu.roll(pair, HEAD_DIM, 1) if h % 2 else pair


def _kv_slot(x, j, fill):
    lane = _iota((x.shape[0], LANES), 1)
    return jnp.where(lane < HEAD_DIM, _head_slot(x, j), fill)


def _flash(qa, k_ref, v_ref, lane0, i, lo, window, masked_past):
    rows = qa.shape[0]
    r_io = _iota((rows, TQ), 0) % TQ
    c_io = _iota((rows, TQ), 1)

    def scores(n, masked):
        start = pl.multiple_of(n * TQ, TQ)
        kk = k_ref[pl.ds(start, TQ), lane0:lane0 + LANES]
        vv = v_ref[pl.ds(start, TQ), lane0:lane0 + LANES]
        s = _dot_nt(qa, kk)
        if masked:
            dist = (i - n) * TQ + r_io - c_io
            ok = dist >= 0
            if window is not None:
                ok = ok & (dist <= window)
            s = jnp.where(ok, s, MASKED)
        return s, vv

    s, vv = scores(i, True)
    m = jnp.max(s, axis=1, keepdims=True)
    p = jnp.exp(s - m)
    l = jnp.sum(p, axis=1, keepdims=True)
    acc = _dot(p.astype(BF16), vv)

    def body(t, carry):
        m, l, acc = carry
        s, vv = scores(i - 1 - t, masked_past)
        m_new = jnp.maximum(m, jnp.max(s, axis=1, keepdims=True))
        a = jnp.exp(m - m_new)
        p = jnp.exp(s - m_new)
        l = a * l + jnp.sum(p, axis=1, keepdims=True)
        acc = a * acc + _dot(p.astype(BF16), vv)
        return m_new, l, acc

    m, l, acc = lax.fori_loop(0, i - lo, body, (m, l, acc))
    return acc, m, l


def _pair_lanes(a, b):
    lane = _iota(a.shape, 1)
    return jnp.where(lane < HEAD_DIM, a, pltpu.roll(b, HEAD_DIM, 1))


def _rank_desc(vals, n_cand, col0, n_io):
    rank = jnp.zeros(vals.shape, jnp.int32)
    for m_idx in range(n_cand):
        col = vals[:, col0 + m_idx:col0 + m_idx + 1]
        beats = (col > vals) | ((col == vals) & (m_idx < n_io))
        rank = rank + beats.astype(jnp.int32)
    return rank


def _moba_prompt_kernel(q_ref, kv_ref, o_ref, kaug_ref, vaug_ref, kmean_ref, *, n_blocks):
    i = pl.program_id(1)
    kvh_n = MOBA_KV_HEADS
    grp = N_HEADS // kvh_n
    kw = kvh_n * HEAD_DIM

    @pl.when(i == 0)
    def _():
        kmean_ref[...] = jnp.zeros(kmean_ref.shape, F32)

        def prep(n, carry):
            start = pl.multiple_of(n * TQ, TQ)
            kt = kv_ref[0, pl.ds(start, TQ), 0:kw]
            vt = kv_ref[0, pl.ds(start, TQ), kw:2 * kw]
            kmean_ref[pl.ds(HEAD_DIM + n, 1), :] = jnp.mean(kt, axis=0, keepdims=True)
            onehot = (_iota((TQ, LANES), 1) - HEAD_DIM == n).astype(F32)
            for j in range(kvh_n):
                kaug_ref[pl.ds(start, TQ), j * LANES:(j + 1) * LANES] = _kv_slot(kt, j, onehot).astype(BF16)
                vaug_ref[pl.ds(start, TQ), j * LANES:(j + 1) * LANES] = _kv_slot(vt, j, 0.0).astype(BF16)
            return carry

        lax.fori_loop(0, n_blocks, prep, 0)

    q = q_ref[0]
    km = kmean_ref[...]
    lane = _iota((TQ, LANES), 1)
    n_io = lane - HEAD_DIM
    eligible = (n_io >= 0) & (n_io < i)
    for j in range(kvh_n):
        kmj = km[:, j * HEAD_DIM:(j + 1) * HEAD_DIM]
        qa = []
        for gi in range(grp):
            h = j * grp + gi
            qh = _head_slot(q, h)
            gate = _dot_nt(qh[:, 0:HEAD_DIM], kmj, precision=lax.Precision.HIGHEST)
            gate = jnp.where(eligible, gate, GATE_FLOOR)
            rank = _rank_desc(gate, n_blocks, HEAD_DIM, n_io)
            sel = (eligible & (rank < MOBA_TOPK)) | (n_io == i)
            bias = jnp.where(sel, 0.0, MASKED)
            qa.append(jnp.where(lane < HEAD_DIM, qh * ATTN_SCALE, bias))
        qa = jnp.concatenate(qa, axis=0).astype(BF16)
        acc, _, l = _flash(qa, kaug_ref, vaug_ref, j * LANES, i, 0, None, False)
        out = acc / l
        for gp in range(grp // 2):
            a = out[(2 * gp) * TQ:(2 * gp + 1) * TQ]
            b = out[(2 * gp + 1) * TQ:(2 * gp + 2) * TQ]
            c0 = (j * grp // 2 + gp) * LANES
            o_ref[0, :, c0:c0 + LANES] = _pair_lanes(a, b)


def _moba_prompt(q, kv):
    b, s, _ = q.shape
    n_blocks = s // MOBA_BLOCK
    assert MOBA_BLOCK == TQ and s % TQ == 0 and n_blocks <= HEAD_DIM
    kw = 2 * MOBA_KV_HEADS * HEAD_DIM
    return pl.pallas_call(
        functools.partial(_moba_prompt_kernel, n_blocks=n_blocks),
        grid=(b, s // TQ),
        in_specs=[pl.BlockSpec((1, TQ, Q_WIDTH), lambda a, i: (a, i, 0)),
                  pl.BlockSpec((1, s, kw), lambda a, i: (a, 0, 0))],
        out_specs=pl.BlockSpec((1, TQ, Q_WIDTH), lambda a, i: (a, i, 0)),
        out_shape=jax.ShapeDtypeStruct((b, s, Q_WIDTH), F32),
        scratch_shapes=[pltpu.VMEM((s, MOBA_KV_HEADS * LANES), BF16),
                        pltpu.VMEM((s, MOBA_KV_HEADS * LANES), BF16),
                        pltpu.VMEM((LANES, MOBA_KV_HEADS * HEAD_DIM), F32)],
        compiler_params=_cparams(2),
        name="moba_prompt",
    )(q, kv)


def _window_chunks(window):
    return -(-window // TQ)


def _swa_prompt_kernel(q_ref, kv_ref, sink_ref, o_ref, kaug_ref, vaug_ref, *, n_chunks):
    i = pl.program_id(1)
    kvh_n = SWA_KV_HEADS
    grp = N_HEADS // kvh_n
    kw = kvh_n * HEAD_DIM

    @pl.when(i == 0)
    def _():
        def prep(n, carry):
            start = pl.multiple_of(n * TQ, TQ)
            kt = kv_ref[0, pl.ds(start, TQ), 0:kw]
            vt = kv_ref[0, pl.ds(start, TQ), kw:2 * kw]
            for j in range(kvh_n):
                kaug_ref[pl.ds(start, TQ), j * LANES:(j + 1) * LANES] = _kv_slot(kt, j, 0.0).astype(BF16)
                vaug_ref[pl.ds(start, TQ), j * LANES:(j + 1) * LANES] = _kv_slot(vt, j, 0.0).astype(BF16)
            return carry

        lax.fori_loop(0, n_chunks, prep, 0)

    q = q_ref[0]
    lane = _iota((TQ, LANES), 1)
    lo = jnp.maximum(i - _window_chunks(SWA_WINDOW), 0)
    for j in range(kvh_n):
        qa = [jnp.where(lane < HEAD_DIM, _head_slot(q, j * grp + gi) * ATTN_SCALE, 0.0) for gi in range(grp)]
        qa = jnp.concatenate(qa, axis=0).astype(BF16)
        acc, m, l = _flash(qa, kaug_ref, vaug_ref, j * LANES, i, lo, SWA_WINDOW, True)
        outs = []
        for gi in range(grp):
            h = j * grp + gi
            rs = slice(gi * TQ, (gi + 1) * TQ)
            sink = sink_ref[0:1, h:h + 1]
            m_f = jnp.maximum(m[rs], sink)
            a = jnp.exp(m[rs] - m_f)
            den = l[rs] * a + jnp.exp(sink - m_f)
            outs.append(acc[rs] * a / den)
        for gp in range(grp // 2):
            c0 = (j * grp // 2 + gp) * LANES
            o_ref[0, :, c0:c0 + LANES] = _pair_lanes(outs[2 * gp], outs[2 * gp + 1])


def _swa_prompt(q, kv, sink):
    b, s, _ = q.shape
    assert s % TQ == 0
    kw = 2 * SWA_KV_HEADS * HEAD_DIM
    return pl.pallas_call(
        functools.partial(_swa_prompt_kernel, n_chunks=s // TQ),
        grid=(b, s // TQ),
        in_specs=[pl.BlockSpec((1, TQ, Q_WIDTH), lambda a, i: (a, i, 0)),
                  pl.BlockSpec((1, s, kw), lambda a, i: (a, 0, 0)),
                  pl.BlockSpec((1, LANES), lambda a, i: (0, 0))],
        out_specs=pl.BlockSpec((1, TQ, Q_WIDTH), lambda a, i: (a, i, 0)),
        out_shape=jax.ShapeDtypeStruct((b, s, Q_WIDTH), F32),
        scratch_shapes=[pltpu.VMEM((s, SWA_KV_HEADS * LANES), BF16),
                        pltpu.VMEM((s, SWA_KV_HEADS * LANES), BF16)],
        compiler_params=_cparams(2),
        name="swa_prompt",
    )(q, kv, sink)


def _gelu_tanh(x):
    return x * (0.5 * (1.0 + jnp.tanh(0.7978845608028654 * (x + 0.044715 * (x * x * x)))))


def _compress_rows(xk_ref, xv_ref, n_blk, pe_ref, w1k_ref, w1v_ref, w2_ref):
    half = NSA_KV_HEADS * HEAD_DIM

    def body(p, carry):
        ak, av = carry
        pe = pe_ref[pl.ds(p, 1), :]
        xk = (xk_ref[pl.ds(p, n_blk, stride=NSA_BLOCK), :] + pe[:, 0:half]).astype(BF16)
        xv = (xv_ref[pl.ds(p, n_blk, stride=NSA_BLOCK), :] + pe[:, half:2 * half]).astype(BF16)
        ak = ak + _dot(xk, w1k_ref[p])
        av = av + _dot(xv, w1v_ref[p])
        return ak, av

    zero = jnp.zeros((n_blk, NSA_KV_HEADS * NSA_CMP_HIDDEN), F32)
    ak, av = lax.fori_loop(0, NSA_BLOCK, body, (zero, zero))
    hid = jnp.concatenate([_gelu_tanh(ak), _gelu_tanh(av)], axis=1).astype(BF16)
    return _dot(hid, w2_ref[...])


def _nsa_compress_seq_kernel(xk_ref, xv_ref, pe_ref, w1k_ref, w1v_ref, w2_ref, o_ref, *, n_blk):
    o_ref[...] = _compress_rows(xk_ref, xv_ref, n_blk, pe_ref, w1k_ref, w1v_ref, w2_ref)


def _nsa_compress_seq(rows2d, pe2, w1k, w1v, w2):
    t = rows2d.shape[0]
    n_blk = min(64, t // NSA_BLOCK)
    tr = n_blk * NSA_BLOCK
    half = 2 * NSA_KV_HEADS * HEAD_DIM
    full = lambda shape: pl.BlockSpec(shape, lambda j: (0,) * len(shape))
    return pl.pallas_call(
        functools.partial(_nsa_compress_seq_kernel, n_blk=n_blk),
        grid=(t // tr,),
        in_specs=[pl.BlockSpec((tr, LANES), lambda j: (j, 0)),
                  pl.BlockSpec((tr, LANES), lambda j: (j, 1)),
                  full(pe2.shape), full(w1k.shape), full(w1v.shape), full(w2.shape)],
        out_specs=pl.BlockSpec((n_blk, half), lambda j: (j, 0)),
        out_shape=jax.ShapeDtypeStruct((t // NSA_BLOCK, half), F32),
        compiler_params=_cparams(1),
        name="nsa_compress_seq",
    )(rows2d, rows2d, pe2, w1k, w1v, w2)


def _nsa_compress_pages_kernel(pt_ref, *refs, n_pages):
    page_refs = refs[:n_pages]
    pe_ref, w1k_ref, w1v_ref, w2_ref, o_ref, xk_sc, xv_sc = refs[n_pages:]
    for j in range(n_pages):
        xk_sc[j * PAGE_SIZE:(j + 1) * PAGE_SIZE, :] = page_refs[j][:, 0:LANES]
        xv_sc[j * PAGE_SIZE:(j + 1) * PAGE_SIZE, :] = page_refs[j][:, LANES:2 * LANES]
    n_blk = n_pages * PAGE_SIZE // NSA_BLOCK
    o_ref[0] = _compress_rows(xk_sc, xv_sc, n_blk, pe_ref, w1k_ref, w1v_ref, w2_ref)


def _page_spec(j, lanes, lane_block):
    return pl.BlockSpec((None, PAGE_SIZE, lanes), lambda b, pt: (pt[b, j], 0, lane_block))


def _nsa_compress_pages(pt, cache, pe2, w1k, w1v, w2):
    db, n_pages = pt.shape
    half = 2 * NSA_KV_HEADS * HEAD_DIM
    n_blk = n_pages * PAGE_SIZE // NSA_BLOCK
    full = lambda shape: pl.BlockSpec(shape, lambda b, pt: (0,) * len(shape))
    grid_spec = pltpu.PrefetchScalarGridSpec(
        num_scalar_prefetch=1,
        grid=(db,),
        in_specs=[_page_spec(j, half, 0) for j in range(n_pages)]
        + [full(pe2.shape), full(w1k.shape), full(w1v.shape), full(w2.shape)],
        out_specs=pl.BlockSpec((1, n_blk, half), lambda b, pt: (b, 0, 0)),
        scratch_shapes=[pltpu.VMEM((n_pages * PAGE_SIZE, LANES), F32),
                        pltpu.VMEM((n_pages * PAGE_SIZE, LANES), F32)],
    )
    return pl.pallas_call(
        functools.partial(_nsa_compress_pages_kernel, n_pages=n_pages),
        grid_spec=grid_spec,
        out_shape=jax.ShapeDtypeStruct((db, n_blk, half), F32),
        compiler_params=_cparams(1),
        name="nsa_compress_pages",
    )(pt, *([cache] * n_pages), pe2, w1k, w1v, w2)


def _nsa_select_kernel(q_ref, kc_ref, vc_ref, ocmp_ref, bias_ref, *, n_blocks):
    i = pl.program_id(1)
    grp = N_HEADS // NSA_KV_HEADS
    q = q_ref[0]
    kcb = kc_ref[0].astype(BF16)
    vcb = vc_ref[0].astype(BF16)
    lane = _iota((TQ, LANES), 1)
    n_io = lane % HEAD_DIM
    t = i * TQ + _iota((TQ, LANES), 0)
    avail = ((n_io + 1) * NSA_BLOCK - 1 <= t) & (n_io < n_blocks)
    score = jnp.zeros((TQ, LANES), F32)
    outs = []
    for h in range(N_HEADS):
        qh = jnp.where(lane < HEAD_DIM, _head_slot(q, h) * ATTN_SCALE, 0.0).astype(BF16)
        ok = avail & ((lane // HEAD_DIM) == (h // grp))
        s = jnp.where(ok, _dot_nt(qh, kcb), MASKED)
        m = jnp.max(s, axis=1, keepdims=True)
        p = jnp.exp(s - m) * ok.astype(F32)
        p = p / jnp.maximum(jnp.sum(p, axis=1, keepdims=True), 1e-30)
        score = score + p
        outs.append(_dot(p.astype(BF16), vcb))
    for hp in range(N_HEADS // 2):
        ocmp_ref[0, :, hp * LANES:(hp + 1) * LANES] = _pair_lanes(outs[2 * hp], outs[2 * hp + 1])
    own = t // NSA_BLOCK
    forced = ((n_io == 0) | (n_io == own) | (n_io == own - 1)) & (n_io < n_blocks)
    sc = jnp.where(avail, score, -jnp.inf)
    sc = jnp.where(forced, jnp.inf, sc)
    rank = jnp.zeros((TQ, LANES), jnp.int32)
    for m_idx in range(n_blocks):
        col = jnp.where(lane < HEAD_DIM, sc[:, m_idx:m_idx + 1], sc[:, HEAD_DIM + m_idx:HEAD_DIM + m_idx + 1])
        beats = (col > sc) | ((col == sc) & (m_idx < n_io))
        rank = rank + beats.astype(jnp.int32)
    sel = (rank < NSA_TOPN) & (sc > -jnp.inf)
    bias_ref[0] = jnp.where(sel, 0.0, MASKED)


def _nsa_select_prompt(q_raw, kc, vc):
    b, s, _ = q_raw.shape
    n_blocks = s // NSA_BLOCK
    assert n_blocks <= HEAD_DIM and NSA_KV_HEADS * HEAD_DIM == LANES
    return pl.pallas_call(
        functools.partial(_nsa_select_kernel, n_blocks=n_blocks),
        grid=(b, s // TQ),
        in_specs=[pl.BlockSpec((1, TQ, Q_WIDTH), lambda a, i: (a, i, 0)),
                  pl.BlockSpec((1, LANES, LANES), lambda a, i: (a, 0, 0)),
                  pl.BlockSpec((1, LANES, LANES), lambda a, i: (a, 0, 0))],
        out_specs=[pl.BlockSpec((1, TQ, Q_WIDTH), lambda a, i: (a, i, 0)),
                   pl.BlockSpec((1, TQ, LANES), lambda a, i: (a, i, 0))],
        out_shape=[jax.ShapeDtypeStruct((b, s, Q_WIDTH), F32), jax.ShapeDtypeStruct((b, s, LANES), F32)],
        compiler_params=_cparams(2),
        name="nsa_select_prompt",
    )(q_raw, kc, vc)


def _nsa_attend_kernel(q_ref, rows_ref, win_ref, bias_ref, ocmp_ref, gates_ref, o_ref,
                       kslc_ref, vslc_ref, kwin_ref, vwin_ref, *, n_chunks):
    i = pl.program_id(1)
    kvh_n = NSA_KV_HEADS
    grp = N_HEADS // kvh_n
    kw = kvh_n * HEAD_DIM

    @pl.when(i == 0)
    def _():
        def prep(n, carry):
            start = pl.multiple_of(n * TQ, TQ)
            blk = (n * TQ + _iota((TQ, LANES), 0)) // NSA_BLOCK
            onehot = (_iota((TQ, LANES), 1) - HEAD_DIM == blk).astype(F32)
            ks = rows_ref[0, pl.ds(start, TQ), 0:kw]
            vs = rows_ref[0, pl.ds(start, TQ), kw:2 * kw]
            kwn = win_ref[0, pl.ds(start, TQ), 0:kw]
            vwn = win_ref[0, pl.ds(start, TQ), kw:2 * kw]
            for j in range(kvh_n):
                cols = slice(j * LANES, (j + 1) * LANES)
                kslc_ref[pl.ds(start, TQ), cols] = _kv_slot(ks, j, onehot).astype(BF16)
                vslc_ref[pl.ds(start, TQ), cols] = _kv_slot(vs, j, 0.0).astype(BF16)
                kwin_ref[pl.ds(start, TQ), cols] = _kv_slot(kwn, j, 0.0).astype(BF16)
                vwin_ref[pl.ds(start, TQ), cols] = _kv_slot(vwn, j, 0.0).astype(BF16)
            return carry

        lax.fori_loop(0, n_chunks, prep, 0)

    q = q_ref[0]
    bias = bias_ref[0]
    gates = gates_ref[0]
    lane = _iota((TQ, LANES), 1)
    lo_win = jnp.maximum(i - _window_chunks(NSA_WINDOW), 0)
    for j in range(kvh_n):
        bias_j = bias if j == 1 else pltpu.roll(bias, HEAD_DIM, 1)
        qs = [_head_slot(q, j * grp + gi) * ATTN_SCALE for gi in range(grp)]
        qa_slc = jnp.concatenate([jnp.where(lane < HEAD_DIM, x, bias_j) for x in qs], axis=0).astype(BF16)
        qa_win = jnp.concatenate([jnp.where(lane < HEAD_DIM, x, 0.0) for x in qs], axis=0).astype(BF16)
        acc_s, _, l_s = _flash(qa_slc, kslc_ref, vslc_ref, j * LANES, i, 0, None, False)
        acc_w, _, l_w = _flash(qa_win, kwin_ref, vwin_ref, j * LANES, i, lo_win, NSA_WINDOW, True)
        o_s = acc_s / l_s
        o_w = acc_w / l_w
        for gp in range(grp // 2):
            h0 = j * grp + 2 * gp
            r0 = slice(2 * gp * TQ, (2 * gp + 1) * TQ)
            r1 = slice((2 * gp + 1) * TQ, (2 * gp + 2) * TQ)
            c0 = (h0 // 2) * LANES
            gate = [jnp.where(lane < HEAD_DIM, gates[:, 3 * h0 + c:3 * h0 + c + 1],
                              gates[:, 3 * h0 + 3 + c:3 * h0 + 4 + c]) for c in range(3)]
            o_ref[0, :, c0:c0 + LANES] = (gate[0] * ocmp_ref[0, :, c0:c0 + LANES]
                                          + gate[1] * _pair_lanes(o_s[r0], o_s[r1])
                                          + gate[2] * _pair_lanes(o_w[r0], o_w[r1]))


def _nsa_attend_prompt(q_rot, rows, win, bias, o_cmp, gates):
    b, s, _ = q_rot.shape
    kw = 2 * NSA_KV_HEADS * HEAD_DIM
    assert s % TQ == 0 and s // NSA_BLOCK <= HEAD_DIM
    tile = lambda w: pl.BlockSpec((1, TQ, w), lambda a, i: (a, i, 0))
    return pl.pallas_call(
        functools.partial(_nsa_attend_kernel, n_chunks=s // TQ),
        grid=(b, s // TQ),
        in_specs=[tile(Q_WIDTH),
                  pl.BlockSpec((1, s, kw), lambda a, i: (a, 0, 1)),
                  pl.BlockSpec((1, s, kw), lambda a, i: (a, 0, 0)),
                  tile(LANES), tile(Q_WIDTH), tile(LANES)],
        out_specs=tile(Q_WIDTH),
        out_shape=jax.ShapeDtypeStruct((b, s, Q_WIDTH), F32),
        scratch_shapes=[pltpu.VMEM((s, NSA_KV_HEADS * LANES), BF16) for _ in range(4)],
        compiler_params=_cparams(2),
        name="nsa_attend_prompt",
    )(q_rot, rows, win, bias, o_cmp, gates)


def _fold_heads(o, kvh_n):
    grp = N_HEADS // kvh_n
    own = (_iota(o.shape, 1) // HEAD_DIM) == (_iota(o.shape, 0) // grp)
    t = jnp.where(own, o, 0.0)
    width = kvh_n * HEAD_DIM
    while width > HEAD_DIM:
        width //= 2
        t = t + pltpu.roll(t, width, 1)
    return t


def _moba_decode_kernel(pt_ref, *refs, n_pages):
    page_refs = refs[:n_pages]
    q_ref, new_ref, o_ref, s_sc, v_sc, km_sc = refs[n_pages:]
    kw = MOBA_KV_HEADS * HEAD_DIM
    pages_per_block = MOBA_BLOCK // PAGE_SIZE
    n_blk = n_pages // pages_per_block
    qbd = q_ref[0]
    qs = qbd * ATTN_SCALE
    qb = qs.astype(BF16)
    km_sc[...] = jnp.zeros(km_sc.shape, F32)
    ksum = None
    for j in range(n_pages):
        kp = page_refs[j][:, 0:kw]
        s_sc[:, j * PAGE_SIZE:(j + 1) * PAGE_SIZE] = _dot_nt(qb, kp.astype(BF16))
        v_sc[j * PAGE_SIZE:(j + 1) * PAGE_SIZE, :] = page_refs[j][:, kw:2 * kw].astype(BF16)
        part = jnp.sum(kp, axis=0, keepdims=True)
        ksum = part if j % pages_per_block == 0 else ksum + part
        if j % pages_per_block == pages_per_block - 1:
            n = j // pages_per_block
            km_sc[n:n + 1, :] = ksum * (1.0 / MOBA_BLOCK)
    gate = _dot_nt(qbd, km_sc[...], precision=lax.Precision.HIGHEST)
    n_io = _iota(gate.shape, 1)
    eligible = n_io < n_blk
    gate = jnp.where(eligible, gate, GATE_FLOOR)
    rank = _rank_desc(gate, n_blk, 0, n_io)
    sel = eligible & (rank < MOBA_TOPK)
    bias = jnp.where(sel, 0.0, MASKED)
    new = new_ref[0]
    s_new = jnp.sum(qs * new[:, 0:kw], axis=1, keepdims=True)
    m = s_new
    for n in range(n_blk):
        cols = slice(n * MOBA_BLOCK, (n + 1) * MOBA_BLOCK)
        sb = s_sc[:, cols] + bias[:, n:n + 1]
        s_sc[:, cols] = sb
        m = jnp.maximum(m, jnp.max(sb, axis=1, keepdims=True))
    p = jnp.exp(s_sc[...] - m)
    p_new = jnp.exp(s_new - m)
    den = jnp.sum(p, axis=1, keepdims=True) + p_new
    o = (_dot(p.astype(BF16), v_sc[...]) + p_new * new[:, kw:2 * kw]) / den
    o_ref[0] = _fold_heads(o, MOBA_KV_HEADS)


def _moba_decode(pt, cache, qbd, new):
    db, n_pages = pt.shape
    kw = MOBA_KV_HEADS * HEAD_DIM
    assert (n_pages * PAGE_SIZE) % MOBA_BLOCK == 0 and n_pages * PAGE_SIZE // MOBA_BLOCK <= LANES
    grid_spec = pltpu.PrefetchScalarGridSpec(
        num_scalar_prefetch=1,
        grid=(db,),
        in_specs=[_page_spec(j, 2 * kw, 0) for j in range(n_pages)]
        + [pl.BlockSpec((1, N_HEADS, kw), lambda b, pt: (b, 0, 0)),
           pl.BlockSpec((1, 1, 2 * kw), lambda b, pt: (b, 0, 0))],
        out_specs=pl.BlockSpec((1, N_HEADS, kw), lambda b, pt: (b, 0, 0)),
        scratch_shapes=[pltpu.VMEM((N_HEADS, n_pages * PAGE_SIZE), F32),
                        pltpu.VMEM((n_pages * PAGE_SIZE, kw), BF16),
                        pltpu.VMEM((LANES, kw), F32)],
    )
    return pl.pallas_call(
        functools.partial(_moba_decode_kernel, n_pages=n_pages),
        grid_spec=grid_spec,
        out_shape=jax.ShapeDtypeStruct((db, N_HEADS, kw), F32),
        compiler_params=_cparams(1),
        name="moba_decode",
    )(pt, *([cache] * n_pages), qbd, new)


def _swa_decode_kernel(buf_ref, new_ref, q_ref, sink_ref, o_ref):
    kw = SWA_KV_HEADS * HEAD_DIM
    qs = q_ref[0] * ATTN_SCALE
    buf = buf_ref[0]
    new = new_ref[0]
    s = _dot_nt(qs.astype(BF16), buf[:, 0:kw].astype(BF16))
    s_new = jnp.sum(qs * new[:, 0:kw], axis=1, keepdims=True)
    sink = sink_ref[:, 0:1]
    m = jnp.maximum(jnp.maximum(jnp.max(s, axis=1, keepdims=True), s_new), sink)
    p = jnp.exp(s - m)
    p_new = jnp.exp(s_new - m)
    den = jnp.sum(p, axis=1, keepdims=True) + p_new + jnp.exp(sink - m)
    o = (_dot(p.astype(BF16), buf[:, kw:2 * kw].astype(BF16)) + p_new * new[:, kw:2 * kw]) / den
    o_ref[0] = _fold_heads(o, SWA_KV_HEADS)


def _swa_decode(buf, new, qbd, sink):
    db, wb, kw2 = buf.shape
    kw = kw2 // 2
    return pl.pallas_call(
        _swa_decode_kernel,
        grid=(db,),
        in_specs=[pl.BlockSpec((1, wb, kw2), lambda b: (b, 0, 0)),
                  pl.BlockSpec((1, 1, kw2), lambda b: (b, 0, 0)),
                  pl.BlockSpec((1, N_HEADS, kw), lambda b: (b, 0, 0)),
                  pl.BlockSpec((N_HEADS, LANES), lambda b: (0, 0))],
        out_specs=pl.BlockSpec((1, N_HEADS, kw), lambda b: (b, 0, 0)),
        out_shape=jax.ShapeDtypeStruct((db, N_HEADS, kw), F32),
        compiler_params=_cparams(1),
        name="swa_decode",
    )(buf, new, qbd, sink)


def _nsa_decode_kernel(pt_ref, *refs, n_pages):
    page_refs = refs[:n_pages]
    (kcvc_ref, win_ref, qraw_ref, qrot_ref, newslc_ref, newwin_ref, gates_ref, o_ref, s_sc, v_sc) = refs[n_pages:]
    kw = NSA_KV_HEADS * HEAD_DIM
    grp = N_HEADS // NSA_KV_HEADS
    blocks_per_page = PAGE_SIZE // NSA_BLOCK
    n_blk = n_pages * blocks_per_page
    q_raw = (qraw_ref[0] * ATTN_SCALE).astype(BF16)
    q_rot = qrot_ref[0] * ATTN_SCALE
    q_rot_b = q_rot.astype(BF16)

    kcvc = kcvc_ref[0]
    s_c = _dot_nt(q_raw, kcvc[:, 0:kw].astype(BF16))
    p_c = jnp.exp(s_c - jnp.max(s_c, axis=1, keepdims=True))
    p_c = p_c / jnp.sum(p_c, axis=1, keepdims=True)
    o_cmp = _dot(p_c.astype(BF16), kcvc[:, kw:2 * kw].astype(BF16))

    head = _iota(p_c.shape, 0)
    sc0 = jnp.sum(jnp.where(head < grp, p_c, 0.0), axis=0, keepdims=True)
    sc1 = jnp.sum(jnp.where(head >= grp, p_c, 0.0), axis=0, keepdims=True)
    r8 = _iota((SUBLANES, n_blk), 0)
    n_io = _iota((SUBLANES, n_blk), 1)
    sc = jnp.where(r8 == 0, sc0, jnp.where(r8 == 1, sc1, -jnp.inf))
    forced = (n_io == 0) | (n_io == n_blk - 1)
    sc = jnp.where(forced, jnp.inf, sc)
    rank = _rank_desc(sc, n_blk, 0, n_io)
    sel = forced | (rank < NSA_TOPN - 1)
    bias2 = jnp.where(sel, 0.0, MASKED)
    bias = jnp.where(head < grp, bias2[0:1], bias2[1:2])

    new_slc = newslc_ref[0]
    s_new = jnp.sum(q_rot * new_slc[:, 0:kw], axis=1, keepdims=True)
    m = s_new
    lane = _iota((N_HEADS, PAGE_SIZE), 1)
    for j in range(n_pages):
        sj = _dot_nt(q_rot_b, page_refs[j][:, 0:kw].astype(BF16))
        bj = bias[:, blocks_per_page * j:blocks_per_page * j + 1]
        for t in range(1, blocks_per_page):
            bj = jnp.where(lane < t * NSA_BLOCK, bj, bias[:, blocks_per_page * j + t:blocks_per_page * j + t + 1])
        sj = sj + bj
        s_sc[:, j * PAGE_SIZE:(j + 1) * PAGE_SIZE] = sj
        v_sc[j * PAGE_SIZE:(j + 1) * PAGE_SIZE, :] = page_refs[j][:, kw:2 * kw].astype(BF16)
        m = jnp.maximum(m, jnp.max(sj, axis=1, keepdims=True))
    p = jnp.exp(s_sc[...] - m)
    p_new = jnp.exp(s_new - m)
    den = jnp.sum(p, axis=1, keepdims=True) + p_new
    o_slc = (_dot(p.astype(BF16), v_sc[...]) + p_new * new_slc[:, kw:2 * kw]) / den

    win = win_ref[0]
    new_win = newwin_ref[0]
    s_w = _dot_nt(q_rot_b, win[:, 0:kw].astype(BF16))
    s_wn = jnp.sum(q_rot * new_win[:, 0:kw], axis=1, keepdims=True)
    m_w = jnp.maximum(jnp.max(s_w, axis=1, keepdims=True), s_wn)
    p_w = jnp.exp(s_w - m_w)
    p_wn = jnp.exp(s_wn - m_w)
    den_w = jnp.sum(p_w, axis=1, keepdims=True) + p_wn
    o_win = (_dot(p_w.astype(BF16), win[:, kw:2 * kw].astype(BF16)) + p_wn * new_win[:, kw:2 * kw]) / den_w

    g = gates_ref[0]
    o = g[:, 0:1] * o_cmp + g[:, 1:2] * o_slc + g[:, 2:3] * o_win
    o_ref[0] = _fold_heads(o, NSA_KV_HEADS)


def _nsa_decode(pt, cache, kcvc, win, q_raw_bd, q_rot_bd, new_rows, new_win, gates):
    db, n_pages = pt.shape
    kw = NSA_KV_HEADS * HEAD_DIM
    n_blk = kcvc.shape[1]
    wb = win.shape[1]
    per_b = lambda shape: pl.BlockSpec((1,) + shape, lambda b, pt: (b, 0, 0))
    grid_spec = pltpu.PrefetchScalarGridSpec(
        num_scalar_prefetch=1,
        grid=(db,),
        in_specs=[_page_spec(j, 2 * kw, 1) for j in range(n_pages)]
        + [per_b((n_blk, 2 * kw)), per_b((wb, 2 * kw)), per_b((N_HEADS, kw)), per_b((N_HEADS, kw)),
           pl.BlockSpec((1, 1, 2 * kw), lambda b, pt: (b, 0, 1)), per_b((1, 2 * kw)), per_b((N_HEADS, LANES))],
        out_specs=per_b((N_HEADS, kw)),
        scratch_shapes=[pltpu.VMEM((N_HEADS, n_pages * PAGE_SIZE), F32),
                        pltpu.VMEM((n_pages * PAGE_SIZE, kw), BF16)],
    )
    return pl.pallas_call(
        functools.partial(_nsa_decode_kernel, n_pages=n_pages),
        grid_spec=grid_spec,
        out_shape=jax.ShapeDtypeStruct((db, N_HEADS, kw), F32),
        compiler_params=_cparams(1),
        name="nsa_decode",
    )(pt, *([cache] * n_pages), kcvc, win, q_raw_bd, q_rot_bd, new_rows, new_win, gates)


def _rope_tables(pos):
    half = HEAD_DIM // 2
    inv = ROPE_THETA ** (-jnp.arange(half, dtype=F32) / half)
    ang = pos.astype(F32)[:, None] * inv[None, :]
    cos = jnp.cos(ang)
    sin = jnp.sin(ang)
    reps = LANES // HEAD_DIM
    return jnp.tile(cos, (1, 2 * reps)), jnp.tile(jnp.concatenate([-sin, sin], 1), (1, reps))


def _block_diag_heads(q, kvh_n):
    r = q.shape[0]
    grp = N_HEADS // kvh_n
    own = (jnp.arange(N_HEADS)[:, None] // grp == jnp.arange(kvh_n)[None, :]).astype(q.dtype)
    q3 = q.reshape(r, N_HEADS, 1, HEAD_DIM) * own[None, :, :, None]
    return q3.reshape(r, N_HEADS, kvh_n * HEAD_DIM)


def _heads_from_folded(o):
    return o[:, :, :HEAD_DIM].reshape(1, o.shape[0], Q_WIDTH)


def _pad_cols(w, n):
    return jnp.pad(w, ((0, 0), (0, n - w.shape[1])))


def _block_diag(blocks):
    rows = sum(b.shape[0] for b in blocks)
    cols = sum(b.shape[1] for b in blocks)
    out = jnp.zeros((rows, cols), blocks[0].dtype)
    r = c = 0
    for b in blocks:
        out = out.at[r:r + b.shape[0], c:c + b.shape[1]].set(b)
        r += b.shape[0]
        c += b.shape[1]
    return out


def _nsa_compress_weights(pe_k, pe_v, w1_k, w2_k, w1_v, w2_v):
    pe2 = jnp.concatenate([pe_k, pe_k, pe_v, pe_v], axis=1)

    def pair(w1):
        w = w1.reshape(NSA_BLOCK, HEAD_DIM, NSA_CMP_HIDDEN)
        z = jnp.zeros_like(w)
        return jnp.concatenate([jnp.concatenate([w, z], 2), jnp.concatenate([z, w], 2)], 1).astype(BF16)

    w2 = _block_diag([w2_k, w2_k, w2_v, w2_v]).astype(BF16)
    return pe2, pair(w1_k), pair(w1_v), w2


_MOBA_OUTS = (((0, Q_WIDTH, "rope"),),
              ((Q_WIDTH, 256, "rope"), (Q_WIDTH + 256, 256, "raw")))
_SWA_OUTS = (((0, Q_WIDTH, "rope"),),
             ((Q_WIDTH, 128, "rope"), (Q_WIDTH + 128, 128, "raw")))
_NSA_OUTS = (((0, Q_WIDTH, "raw"),),
             ((0, Q_WIDTH, "rope"),),
             ((Q_WIDTH, 128, "raw"), (Q_WIDTH + 128, 128, "raw"), (Q_WIDTH + 256, 128, "rope"),
              (Q_WIDTH + 384, 128, "raw")),
             ((Q_WIDTH + 512, 128, "rope"), (Q_WIDTH + 640, 128, "raw")),
             ((Q_WIDTH + 768, 128, "sigmoid"),))
_NSA_IN_PADDED = Q_WIDTH + 7 * 128


def kernel(x_prompt, x_sample, cache_moba_kv, state_swa_kv, cache_nsa_kv, state_nsa_win_kv, state_ffn_conv, page_table, c_prompt, c_sample, ada_w, ada_b, ln_g, ln_b, moba_w_in, moba_w_o, swa_w_in, swa_w_o, swa_sink, nsa_w_in, nsa_w_o, nsa_pe_k, nsa_pe_v, nsa_w1_k, nsa_w2_k, nsa_w1_v, nsa_w2_v, ffn_w_gate, ffn_w_up, ffn_conv_w, ffn_conv_b, ffn_w_down):
    b, s, d = x_prompt.shape
    db, dec_seq, _ = x_sample.shape
    depth = ada_w.shape[0]
    ff = ffn_w_gate.shape[2]
    n_pages = page_table.shape[1]
    n_phys = cache_moba_kv.shape[1]
    past_len = n_pages * PAGE_SIZE
    assert dec_seq == 1 and past_len % MOBA_BLOCK == 0 and s % 512 == 0
    assert state_swa_kv.shape[2] <= SWA_WINDOW and state_nsa_win_kv.shape[2] <= NSA_WINDOW
    alpha = (2 * depth) ** 0.25
    tm = 512
    tf = ff // 2

    rows = b + db
    rows_pad = -(-rows // SUBLANES) * SUBLANES
    c_all = jnp.concatenate([c_prompt, c_sample, jnp.zeros((rows_pad - rows, d), F32)], 0)
    mod = _adaln_all(c_all, ada_w.reshape(depth * 2, d, 3 * d), ada_b.reshape(depth * 2, 1, 3 * d))

    def modulation(i, sub):
        m = mod[i * 2 + sub]
        mp = m[:b].reshape(b, 1, 3 * d)
        ms = m[b:b + db].reshape(1, db, 3 * d)
        return ([mp[..., k * d:(k + 1) * d] for k in range(3)], [ms[..., k * d:(k + 1) * d] for k in range(3)])

    cos_p, sin_p = _rope_tables(jnp.arange(s, dtype=jnp.int32))
    cos_s, sin_s = _rope_tables(jnp.full((db,), past_len, jnp.int32))

    moba_cache = cache_moba_kv.reshape(-1, PAGE_SIZE, 2 * MOBA_KV_HEADS * HEAD_DIM)
    nsa_cache = cache_nsa_kv.reshape(-1, PAGE_SIZE, 4 * NSA_KV_HEADS * HEAD_DIM)

    xp = x_prompt
    xs = x_sample.reshape(1, db, d)
    moba_p, moba_s, swa_p, swa_s, nsa_p, nsa_s, nsaw_p, nsaw_s, conv_p, conv_s = ([] for _ in range(10))
    for i in range(depth):
        kind, j = i % N_MIXERS, i // N_MIXERS
        (sh_p, sc_p, gt_p), (sh_s, sc_s, gt_s) = modulation(i, 0)
        if kind == 0:
            w_in = moba_w_in[j].astype(BF16)
            q_p, kv_p = _inproj(xp, sh_p, sc_p, w_in, cos_p, sin_p, _MOBA_OUTS, tm)
            q_s, kv_s = _inproj(xs, sh_s, sc_s, w_in, cos_s, sin_s, _MOBA_OUTS, db)
            o_p = _moba_prompt(q_p, kv_p)
            o_s = _heads_from_folded(_moba_decode(page_table + j * n_phys, moba_cache,
                                                  _block_diag_heads(q_s[0], MOBA_KV_HEADS), kv_s.reshape(db, 1, -1)))
            moba_p.append(kv_p.reshape(b, s, 2, MOBA_KV_HEADS, HEAD_DIM))
            moba_s.append(kv_s.reshape(db, 1, 2, MOBA_KV_HEADS, HEAD_DIM))
            w_o = moba_w_o[j]
        elif kind == 1:
            w_in = swa_w_in[j].astype(BF16)
            q_p, kv_p = _inproj(xp, sh_p, sc_p, w_in, cos_p, sin_p, _SWA_OUTS, tm)
            q_s, kv_s = _inproj(xs, sh_s, sc_s, w_in, cos_s, sin_s, _SWA_OUTS, db)
            sink = swa_sink[j]
            o_p = _swa_prompt(q_p, kv_p, _pad_cols(sink[None, :], LANES))
            buf = state_swa_kv[j].reshape(db, -1, 2 * SWA_KV_HEADS * HEAD_DIM)
            new = kv_s.reshape(db, 1, -1)
            o_s = _heads_from_folded(_swa_decode(buf, new, _block_diag_heads(q_s[0], SWA_KV_HEADS),
                                                 jnp.broadcast_to(sink[:, None], (N_HEADS, LANES))))
            swa_p.append(kv_p[:, s - min(SWA_WINDOW, s):].reshape(b, -1, 2, SWA_KV_HEADS, HEAD_DIM))
            swa_s.append(jnp.concatenate([buf[:, 1:], new], 1).reshape(state_swa_kv.shape[1:]))
            w_o = swa_w_o[j]
        else:
            w_in = _pad_cols(nsa_w_in[j], _NSA_IN_PADDED).astype(BF16)
            cw = _nsa_compress_weights(nsa_pe_k[j], nsa_pe_v[j], nsa_w1_k[j], nsa_w2_k[j], nsa_w1_v[j], nsa_w2_v[j])
            qraw_p, qrot_p, rows_p, win_p, gates_p = _inproj(xp, sh_p, sc_p, w_in, cos_p, sin_p, _NSA_OUTS, tm)
            qraw_s, qrot_s, rows_s, win_s, gates_s = _inproj(xs, sh_s, sc_s, w_in, cos_s, sin_s, _NSA_OUTS, db)
            n_blk = s // NSA_BLOCK
            cmp_p = _nsa_compress_seq(rows_p.reshape(b * s, -1), *cw).reshape(b, n_blk, 2, NSA_KV_HEADS, HEAD_DIM)
            cmp_p = jnp.pad(cmp_p.transpose(2, 0, 3, 1, 4),
                            ((0, 0), (0, 0), (0, 0), (0, HEAD_DIM - n_blk), (0, LANES - HEAD_DIM)))
            cmp_p = cmp_p.reshape(2, b, NSA_KV_HEADS * HEAD_DIM, LANES)
            ocmp_p, bias_p = _nsa_select_prompt(qraw_p, cmp_p[0], cmp_p[1])
            o_p = _nsa_attend_prompt(qrot_p, rows_p, win_p, bias_p, ocmp_p, gates_p)
            pt = page_table + j * n_phys
            kcvc = _nsa_compress_pages(pt, nsa_cache, *cw)
            win_buf = state_nsa_win_kv[j].reshape(db, -1, 2 * NSA_KV_HEADS * HEAD_DIM)
            new_win = win_s.reshape(db, 1, -1)
            gates3 = jnp.pad(gates_s[0, :, :3 * N_HEADS].reshape(db, N_HEADS, 3), ((0, 0), (0, 0), (0, LANES - 3)))
            o_s = _heads_from_folded(_nsa_decode(pt, nsa_cache, kcvc, win_buf,
                                                 _block_diag_heads(qraw_s[0], NSA_KV_HEADS),
                                                 _block_diag_heads(qrot_s[0], NSA_KV_HEADS),
                                                 rows_s.reshape(db, 1, -1), new_win, gates3))
            nsa_p.append(rows_p.reshape(b, s, 4, NSA_KV_HEADS, HEAD_DIM))
            nsa_s.append(rows_s.reshape(db, 1, 4, NSA_KV_HEADS, HEAD_DIM))
            nsaw_p.append(win_p[:, s - min(NSA_WINDOW, s):].reshape(b, -1, 2, NSA_KV_HEADS, HEAD_DIM))
            nsaw_s.append(jnp.concatenate([win_buf[:, 1:], new_win], 1).reshape(state_nsa_win_kv.shape[1:]))
            w_o = nsa_w_o[j]
        w_o = w_o.astype(BF16)
        g0, b0 = ln_g[i, 0][None, :], ln_b[i, 0][None, :]
        xp = _outproj_ln(o_p, w_o, xp, gt_p, g0, b0, alpha, tm)
        xs = _outproj_ln(o_s, w_o, xs, gt_s, g0, b0, alpha, db)

        (sh_p, sc_p, gt_p), (sh_s, sc_s, gt_s) = modulation(i, 1)
        wg, wu, wd = ffn_w_gate[i].astype(BF16), ffn_w_up[i].astype(BF16), ffn_w_down[i].astype(BF16)
        cwt, cbs = ffn_conv_w[i], ffn_conv_b[i][None, :]
        g1, b1 = ln_g[i, 1][None, :], ln_b[i, 1][None, :]
        xp, st_p = _ffn_seq(xp, sh_p, sc_p, gt_p, jnp.zeros((b, CONV_WIDTH - 1, ff), F32), wg, wu, cwt, cbs, wd,
                            g1, b1, alpha, tm, tf)
        prev = state_ffn_conv[i]
        xs2, g_new = _ffn_step(xs[0], sh_s[0], sc_s[0], gt_s[0], prev[:, 0], prev[:, 1], wg, wu, cwt, cbs, wd,
                               g1, b1, alpha, tf)
        xs = xs2[None]
        conv_p.append(st_p)
        conv_s.append(jnp.stack([prev[:, 1], g_new], 1))
    return (xp, xs.reshape(db, 1, d), jnp.stack(moba_p), jnp.stack(moba_s), jnp.stack(swa_p), jnp.stack(swa_s),
            jnp.stack(nsa_p), jnp.stack(nsa_s), jnp.stack(nsaw_p), jnp.stack(nsaw_s),
            jnp.stack(conv_p), jnp.stack(conv_s))
```

```python
import functools

import jax
import jax.numpy as jnp
from jax import lax
from jax.experimental import pallas as pl
from jax.experimental.pallas import tpu as pltpu

F32 = jnp.float32
BF16 = jnp.bfloat16

HEAD_DIM = 64
N_HEADS = 16
Q_WIDTH = N_HEADS * HEAD_DIM
ROPE_THETA = 10000.0
N_MIXERS = 3
PAGE_SIZE = 128
MOBA_KV_HEADS = 4
MOBA_BLOCK = 256
MOBA_TOPK = 3
SWA_KV_HEADS = 2
SWA_WINDOW = 128
NSA_KV_HEADS = 2
NSA_BLOCK = 64
NSA_TOPN = 16
NSA_WINDOW = 512
NSA_CMP_HIDDEN = 128
CONV_WIDTH = 3
LN_EPS = 1e-5
ATTN_SCALE = HEAD_DIM ** -0.5

LANES = 128
SUBLANES = 8
TQ = 256
MASKED = -1e30
GATE_FLOOR = -3e38
COMPRESS_GROUP = 8
TILE_PITCH = HEAD_DIM + SUBLANES
VMEM_LIMIT = 56 * 1024 * 1024


def _cparams(n_axes):
    return pltpu.CompilerParams(dimension_semantics=("arbitrary",) * n_axes,
                                vmem_limit_bytes=VMEM_LIMIT)


def _dot(a, b):
    return jnp.dot(a, b, preferred_element_type=F32)


def _dot_nt(a, b, precision=None):
    return lax.dot_general(a, b, (((1,), (1,)), ((), ())), precision=precision,
                           preferred_element_type=F32)


def _iota(shape, axis):
    return lax.broadcasted_iota(jnp.int32, shape, axis)


def _adaln_kernel(c_ref, w_ref, b_ref, o_ref):
    c = c_ref[...]
    a = c * (1.0 / (1.0 + jnp.exp(-c)))
    o_ref[0] = _dot(a.astype(BF16), w_ref[0].astype(BF16)) + b_ref[0]


def _adaln_all(c_all, ada_w, ada_b):
    n_sub, d, d3 = ada_w.shape
    rows = c_all.shape[0]
    tn = 1024
    return pl.pallas_call(
        _adaln_kernel,
        grid=(n_sub, d3 // tn),
        in_specs=[pl.BlockSpec((rows, d), lambda l, j: (0, 0)),
                  pl.BlockSpec((1, d, tn), lambda l, j: (l, 0, j)),
                  pl.BlockSpec((1, 1, tn), lambda l, j: (l, 0, j))],
        out_specs=pl.BlockSpec((1, rows, tn), lambda l, j: (l, 0, j)),
        out_shape=jax.ShapeDtypeStruct((n_sub, rows, d3), F32),
        compiler_params=_cparams(2),
        name="adaln",
    )(c_all, ada_w, ada_b)


def _rope_chunk(y, cos, sin_signed):
    lo = (_iota(y.shape, 1) % HEAD_DIM) < (HEAD_DIM // 2)
    swapped = jnp.where(lo, pltpu.roll(y, LANES - HEAD_DIM // 2, 1), pltpu.roll(y, HEAD_DIM // 2, 1))
    return y * cos + swapped * sin_signed


def _inproj_kernel(x_ref, sh_ref, sc_ref, w_ref, cos_ref, sin_ref, *rest, outs, has_t):
    h = (x_ref[0] * (1.0 + sc_ref[0]) + sh_ref[0]).astype(BF16)
    y = _dot(h, w_ref[...])
    cos = cos_ref[...]
    sin = sin_ref[...]
    out_refs = rest
    if has_t:
        wt_ref, out_refs, ot_ref = rest[0], rest[1:-1], rest[-1]
        ot_ref[0] = _dot_nt(wt_ref[...], h)
    for o_ref, segs in zip(out_refs, outs):
        dst = 0
        for (start, width, op) in segs:
            for c in range(0, width, LANES):
                blk = y[:, start + c:start + c + LANES]
                if op == "rope":
                    blk = _rope_chunk(blk, cos, sin)
                elif op == "sigmoid":
                    blk = 1.0 / (1.0 + jnp.exp(-blk))
                o_ref[0, :, dst + c:dst + c + LANES] = blk
            dst += width


def _inproj(x, shift, scale, w, cos, sin, outs, tm, w_t=None):
    g, r, d = x.shape
    n = w.shape[1]
    rm = shift.shape[1]
    mod_spec = (pl.BlockSpec((1, 1, d), lambda a, b: (a, 0, 0)) if rm == 1
                else pl.BlockSpec((1, tm, d), lambda a, b: (a, b, 0)))
    widths = [sum(s[1] for s in segs) for segs in outs]
    in_specs = [pl.BlockSpec((1, tm, d), lambda a, b: (a, b, 0)), mod_spec, mod_spec,
                pl.BlockSpec((d, n), lambda a, b: (0, 0)),
                pl.BlockSpec((tm, LANES), lambda a, b: (b, 0)),
                pl.BlockSpec((tm, LANES), lambda a, b: (b, 0))]
    out_specs = [pl.BlockSpec((1, tm, wd), lambda a, b: (a, b, 0)) for wd in widths]
    out_shape = [jax.ShapeDtypeStruct((g, r, wd), F32) for wd in widths]
    args = [x, shift, scale, w, cos, sin]
    if w_t is not None:
        nt = w_t.shape[0]
        in_specs.append(pl.BlockSpec((nt, d), lambda a, b: (0, 0)))
        out_specs.append(pl.BlockSpec((1, nt, tm), lambda a, b: (a, 0, b)))
        out_shape.append(jax.ShapeDtypeStruct((g, nt, r), F32))
        args.append(w_t)
    return pl.pallas_call(
        functools.partial(_inproj_kernel, outs=outs, has_t=w_t is not None),
        grid=(g, r // tm),
        in_specs=in_specs,
        out_specs=out_specs,
        out_shape=out_shape,
        compiler_params=_cparams(2),
        name="inproj",
    )(*args)


def _residual_ln(x, y, gate, g, b, alpha):
    z = alpha * x + (1.0 + gate) * y
    mu = jnp.mean(z, -1, keepdims=True)
    zc = z - mu
    var = jnp.mean(zc * zc, -1, keepdims=True)
    return zc * lax.rsqrt(var + LN_EPS) * g + b


def _outproj_kernel(o_ref, w_ref, x_ref, gt_ref, g_ref, b_ref, out_ref, *, alpha):
    y = _dot(o_ref[0].astype(BF16), w_ref[...])
    out_ref[0] = _residual_ln(x_ref[0], y, gt_ref[0], g_ref[...], b_ref[...], alpha)


def _outproj_ln(o, w, x, gate, ln_g, ln_b, alpha, tm):
    g, r, d = x.shape
    k = o.shape[2]
    rm = gate.shape[1]
    mod_spec = (pl.BlockSpec((1, 1, d), lambda a, b: (a, 0, 0)) if rm == 1
                else pl.BlockSpec((1, tm, d), lambda a, b: (a, b, 0)))
    return pl.pallas_call(
        functools.partial(_outproj_kernel, alpha=alpha),
        grid=(g, r // tm),
        in_specs=[pl.BlockSpec((1, tm, k), lambda a, b: (a, b, 0)),
                  pl.BlockSpec((k, d), lambda a, b: (0, 0)),
                  pl.BlockSpec((1, tm, d), lambda a, b: (a, b, 0)), mod_spec,
                  pl.BlockSpec((1, d), lambda a, b: (0, 0)),
                  pl.BlockSpec((1, d), lambda a, b: (0, 0))],
        out_specs=pl.BlockSpec((1, tm, d), lambda a, b: (a, b, 0)),
        out_shape=jax.ShapeDtypeStruct((g, r, d), F32),
        compiler_params=_cparams(2),
        name="outproj_ln",
    )(o, w, x, gate, ln_g, ln_b)


def _silu(x):
    return x * (1.0 / (1.0 + jnp.exp(-x)))


def _ffn_seq_kernel(x_ref, xh_ref, sh_ref, sc_ref, gt_ref, prev_ref, wg_ref, wu_ref, cw_ref, cb_ref,
                    wd_ref, g_ref, b_ref, out_ref, st_ref, acc_ref, *, alpha, n_f):
    r = pl.program_id(1)
    f = pl.program_id(2)
    x = x_ref[0]
    tm = x.shape[0]
    sc = 1.0 + sc_ref[0]
    sh = sh_ref[0]
    h = (x * sc + sh).astype(BF16)
    hh = (xh_ref[0] * sc + sh).astype(BF16)
    wg = wg_ref[...]
    gcur = _dot(h, wg)
    ghalo = _dot(hh, wg)
    first = r == 0
    prev = prev_ref[0]
    pm1 = jnp.where(first, prev[1:2], ghalo[SUBLANES - 1:SUBLANES])
    pm2 = jnp.where(first, prev[0:1], ghalo[SUBLANES - 2:SUBLANES - 1])
    row = _iota(gcur.shape, 0)
    g1 = jnp.where(row == 0, pm1, pltpu.roll(gcur, 1, 0))
    g2 = jnp.where(row == 0, pm2, jnp.where(row == 1, pm1, pltpu.roll(gcur, 2, 0)))
    cw = cw_ref[...]
    conv = cb_ref[...] + cw[0:1] * g2 + cw[1:2] * g1 + cw[2:3] * gcur
    act = _silu(conv) * _dot(h, wu_ref[...])
    part = _dot(act.astype(BF16), wd_ref[...])
    st_ref[0, 0] = gcur[tm - 2:tm]

    @pl.when(f == 0)
    def _():
        acc_ref[...] = part

    @pl.when(f != 0)
    def _():
        acc_ref[...] += part

    @pl.when(f == n_f - 1)
    def _():
        out_ref[0] = _residual_ln(x, acc_ref[...], gt_ref[0], g_ref[...], b_ref[...], alpha)


def _ffn_seq(x, shift, scale, gate, prev, wg, wu, cw, cb, wd, ln_g, ln_b, alpha, tm, tf):
    g, r, d = x.shape
    ff = wg.shape[1]
    n_f = ff // tf
    n_r = r // tm
    hb = tm // SUBLANES
    mod = pl.BlockSpec((1, 1, d), lambda a, b, c: (a, 0, 0))
    out, st = pl.pallas_call(
        functools.partial(_ffn_seq_kernel, alpha=alpha, n_f=n_f),
        grid=(g, n_r, n_f),
        in_specs=[pl.BlockSpec((1, tm, d), lambda a, b, c: (a, b, 0)),
                  pl.BlockSpec((1, SUBLANES, d), lambda a, b, c: (a, jnp.maximum(b * hb - 1, 0), 0)),
                  mod, mod, mod,
                  pl.BlockSpec((1, 2, tf), lambda a, b, c: (a, 0, c)),
                  pl.BlockSpec((d, tf), lambda a, b, c: (0, c)),
                  pl.BlockSpec((d, tf), lambda a, b, c: (0, c)),
                  pl.BlockSpec((CONV_WIDTH, tf), lambda a, b, c: (0, c)),
                  pl.BlockSpec((1, tf), lambda a, b, c: (0, c)),
                  pl.BlockSpec((tf, d), lambda a, b, c: (c, 0)),
                  pl.BlockSpec((1, d), lambda a, b, c: (0, 0)),
                  pl.BlockSpec((1, d), lambda a, b, c: (0, 0))],
        out_specs=[pl.BlockSpec((1, tm, d), lambda a, b, c: (a, b, 0)),
                   pl.BlockSpec((1, 1, 2, tf), lambda a, b, c: (a, b, 0, c))],
        out_shape=[jax.ShapeDtypeStruct((g, r, d), F32),
                   jax.ShapeDtypeStruct((g, n_r, 2, ff), F32)],
        scratch_shapes=[pltpu.VMEM((tm, d), F32)],
        compiler_params=_cparams(3),
        name="ffn_seq",
    )(x, x, shift, scale, gate, prev, wg, wu, cw, cb, wd, ln_g, ln_b)
    return out, st[:, n_r - 1]


def _ffn_step_kernel(x_ref, sh_ref, sc_ref, gt_ref, p0_ref, p1_ref, wg_ref, wu_ref, cw_ref, cb_ref,
                     wd_ref, g_ref, b_ref, out_ref, st_ref, acc_ref, *, alpha, n_f):
    f = pl.program_id(0)
    x = x_ref[...]
    h = (x * (1.0 + sc_ref[...]) + sh_ref[...]).astype(BF16)
    gcur = _dot(h, wg_ref[...])
    cw = cw_ref[...]
    conv = cb_ref[...] + cw[0:1] * p0_ref[...] + cw[1:2] * p1_ref[...] + cw[2:3] * gcur
    act = _silu(conv) * _dot(h, wu_ref[...])
    part = _dot(act.astype(BF16), wd_ref[...])
    st_ref[...] = gcur

    @pl.when(f == 0)
    def _():
        acc_ref[...] = part

    @pl.when(f != 0)
    def _():
        acc_ref[...] += part

    @pl.when(f == n_f - 1)
    def _():
        out_ref[...] = _residual_ln(x, acc_ref[...], gt_ref[...], g_ref[...], b_ref[...], alpha)


def _ffn_step(x, shift, scale, gate, p0, p1, wg, wu, cw, cb, wd, ln_g, ln_b, alpha, tf):
    r, d = x.shape
    ff = wg.shape[1]
    n_f = ff // tf
    full = pl.BlockSpec((r, d), lambda c: (0, 0))
    return pl.pallas_call(
        functools.partial(_ffn_step_kernel, alpha=alpha, n_f=n_f),
        grid=(n_f,),
        in_specs=[full, full, full, full,
                  pl.BlockSpec((r, tf), lambda c: (0, c)),
                  pl.BlockSpec((r, tf), lambda c: (0, c)),
                  pl.BlockSpec((d, tf), lambda c: (0, c)),
                  pl.BlockSpec((d, tf), lambda c: (0, c)),
                  pl.BlockSpec((CONV_WIDTH, tf), lambda c: (0, c)),
                  pl.BlockSpec((1, tf), lambda c: (0, c)),
                  pl.BlockSpec((tf, d), lambda c: (c, 0)),
                  pl.BlockSpec((1, d), lambda c: (0, 0)),
                  pl.BlockSpec((1, d), lambda c: (0, 0))],
        out_specs=[full, pl.BlockSpec((r, tf), lambda c: (0, c))],
        out_shape=[jax.ShapeDtypeStruct((r, d), F32), jax.ShapeDtypeStruct((r, ff), F32)],
        scratch_shapes=[pltpu.VMEM((r, d), F32)],
        compiler_params=_cparams(1),
        name="ffn_step",
    )(x, shift, scale, gate, p0, p1, wg, wu, cw, cb, wd, ln_g, ln_b)


def _head_slot(x, h):
    pair = x[:, (h // 2) * LANES:(h // 2 + 1) * LANES]
    return pltpu.roll(pair, HEAD_DIM, 1) if h % 2 else pair


def _slot(x, j):
    lane = _iota((x.shape[0], LANES), 1)
    return jnp.where(lane < HEAD_DIM, _head_slot(x, j), 0.0)


def _flash_t(qg, k_ref, lane0, vt_ref, row0, i, lo, window, masked_past, bias_fn=None):
    rows = qg.shape[0]
    key_io = _iota((TQ, rows), 0)
    q_io = _iota((TQ, rows), 1) % TQ

    def scores(n, masked):
        start = pl.multiple_of(n * TQ, TQ)
        s = _dot_nt(k_ref[pl.ds(start, TQ), lane0:lane0 + LANES], qg)
        if bias_fn is not None:
            s = bias_fn(n, s)
        if masked:
            dist = (i - n) * TQ + q_io - key_io
            ok = dist >= 0
            if window is not None:
                ok = ok & (dist <= window)
            s = jnp.where(ok, s, MASKED)
        return s

    def values(n, p):
        return _dot(vt_ref[n, row0:row0 + HEAD_DIM, :], p.astype(BF16))

    s = scores(i, True)
    m = jnp.max(s, axis=0, keepdims=True)
    p = jnp.exp(s - m)
    l = jnp.sum(p, axis=0, keepdims=True)
    acc = values(i, p)

    def body(t, carry):
        m, l, acc = carry
        n = i - 1 - t
        s = scores(n, masked_past)
        m_new = jnp.maximum(m, jnp.max(s, axis=0, keepdims=True))
        a = jnp.exp(m - m_new)
        p = jnp.exp(s - m_new)
        l = a * l + jnp.sum(p, axis=0, keepdims=True)
        acc = a * acc + values(n, p)
        return m_new, l, acc

    m, l, acc = lax.fori_loop(0, i - lo, body, (m, l, acc))
    return acc, m, l


def _store_heads(o_ref, out_t, head0, n_heads):
    for gp in range(n_heads // 2):
        pair = jnp.concatenate([out_t[:, (2 * gp) * TQ:(2 * gp + 1) * TQ],
                                out_t[:, (2 * gp + 1) * TQ:(2 * gp + 2) * TQ]], axis=0)
        c0 = ((head0 + 2 * gp) // 2) * LANES
        o_ref[0, :, c0:c0 + LANES] = pair.T


def _stack_queries(q, heads):
    lane = _iota((TQ, LANES), 1)
    return jnp.concatenate([(jnp.where(lane < HEAD_DIM, _head_slot(q, h), 0.0) * ATTN_SCALE).astype(BF16)
                            for h in heads], axis=0)


def _window_chunks(window):
    return -(-window // TQ)


def _moba_prompt_kernel(q_ref, kv_ref, vt_ref, o_ref, kslot_ref, vt_sc, kmean_ref, bias_ref, *, n_blocks):
    i = pl.program_id(1)
    kvh_n = MOBA_KV_HEADS
    grp = N_HEADS // kvh_n
    kw = kvh_n * HEAD_DIM

    @pl.when(i == 0)
    def _():
        for n in range(n_blocks):
            vt_sc[n] = vt_ref[0, :, n * TQ:(n + 1) * TQ].astype(BF16)
        blk_rows = _iota(kmean_ref.shape, 0)

        def prep(n, kmean):
            start = pl.multiple_of(n * TQ, TQ)
            kt = kv_ref[0, pl.ds(start, TQ), 0:kw]
            means = []
            for j in range(kvh_n):
                ks = _slot(kt, j)
                kslot_ref[pl.ds(start, TQ), j * LANES:(j + 1) * LANES] = ks.astype(BF16)
                means.append(jnp.mean(ks, axis=0, keepdims=True))
            return jnp.where(blk_rows == n, jnp.concatenate(means, axis=1), kmean)

        kmean_ref[...] = lax.fori_loop(0, n_blocks, prep, jnp.zeros(kmean_ref.shape, F32))

    q = q_ref[0]
    lane = _iota((TQ, LANES), 1)
    blk_io = _iota((kmean_ref.shape[0], TQ), 0)
    eligible = blk_io < i
    for j in range(kvh_n):
        kmj = kmean_ref[:, j * LANES:(j + 1) * LANES]
        for gi in range(grp):
            qh = jnp.where(lane < HEAD_DIM, _head_slot(q, j * grp + gi), 0.0)
            gate = _dot_nt(kmj, qh, precision=lax.Precision.HIGHEST)
            gate = jnp.where(eligible, gate, GATE_FLOOR)
            rank = jnp.zeros(gate.shape, jnp.int32)
            for m_idx in range(n_blocks):
                row = gate[m_idx:m_idx + 1, :]
                beats = (row > gate) | ((row == gate) & (m_idx < blk_io))
                rank = rank + beats.astype(jnp.int32)
            sel = (eligible & (rank < MOBA_TOPK)) | (blk_io == i)
            bias_ref[:, gi * TQ:(gi + 1) * TQ] = jnp.where(sel, 0.0, MASKED)
        qg = _stack_queries(q, range(j * grp, (j + 1) * grp))
        acc, _, l = _flash_t(qg, kslot_ref, j * LANES, vt_sc, j * HEAD_DIM, i, 0, None, False,
                             lambda n, s: s + bias_ref[pl.ds(n, 1), :])
        _store_heads(o_ref, acc / l, j * grp, grp)


def _moba_prompt(q, kv, v_t):
    b, s, _ = q.shape
    n_blocks = s // MOBA_BLOCK
    assert MOBA_BLOCK == TQ and s % TQ == 0
    kw = MOBA_KV_HEADS * HEAD_DIM
    grp = N_HEADS // MOBA_KV_HEADS
    nbp = -(-n_blocks // SUBLANES) * SUBLANES
    return pl.pallas_call(
        functools.partial(_moba_prompt_kernel, n_blocks=n_blocks),
        grid=(b, s // TQ),
        in_specs=[pl.BlockSpec((1, TQ, Q_WIDTH), lambda a, i: (a, i, 0)),
                  pl.BlockSpec((1, s, 2 * kw), lambda a, i: (a, 0, 0)),
                  pl.BlockSpec((1, kw, s), lambda a, i: (a, 0, 0))],
        out_specs=pl.BlockSpec((1, TQ, Q_WIDTH), lambda a, i: (a, i, 0)),
        out_shape=jax.ShapeDtypeStruct((b, s, Q_WIDTH), F32),
        scratch_shapes=[pltpu.VMEM((s, MOBA_KV_HEADS * LANES), BF16),
                        pltpu.VMEM((n_blocks, kw, TQ), BF16),
                        pltpu.VMEM((nbp, MOBA_KV_HEADS * LANES), F32),
                        pltpu.VMEM((nbp, grp * TQ), F32)],
        compiler_params=_cparams(2),
        name="moba_prompt",
    )(q, kv, v_t)


def _swa_prompt_kernel(q_ref, kv_ref, vt_ref, sink_ref, o_ref, kslot_ref, vt_sc, *, n_chunks):
    i = pl.program_id(1)
    kvh_n = SWA_KV_HEADS
    grp = N_HEADS // kvh_n
    kw = kvh_n * HEAD_DIM

    @pl.when(i == 0)
    def _():
        for n in range(n_chunks):
            vt_sc[n] = vt_ref[0, :, n * TQ:(n + 1) * TQ].astype(BF16)

        def prep(n, carry):
            start = pl.multiple_of(n * TQ, TQ)
            kt = kv_ref[0, pl.ds(start, TQ), 0:kw]
            for j in range(kvh_n):
                kslot_ref[pl.ds(start, TQ), j * LANES:(j + 1) * LANES] = _slot(kt, j).astype(BF16)
            return carry

        lax.fori_loop(0, n_chunks, prep, 0)

    q = q_ref[0]
    lo = jnp.maximum(i - _window_chunks(SWA_WINDOW), 0)
    for j in range(kvh_n):
        heads = range(j * grp, (j + 1) * grp)
        acc, m, l = _flash_t(_stack_queries(q, heads), kslot_ref, j * LANES, vt_sc, j * HEAD_DIM, i, lo,
                             SWA_WINDOW, True)
        sink = jnp.concatenate([jnp.broadcast_to(sink_ref[0:1, h:h + 1], (1, TQ)) for h in heads], axis=1)
        m_f = jnp.maximum(m, sink)
        a = jnp.exp(m - m_f)
        den = l * a + jnp.exp(sink - m_f)
        _store_heads(o_ref, acc * (a / den), j * grp, grp)


def _swa_prompt(q, kv, v_t, sink):
    b, s, _ = q.shape
    assert s % TQ == 0
    kw = SWA_KV_HEADS * HEAD_DIM
    return pl.pallas_call(
        functools.partial(_swa_prompt_kernel, n_chunks=s // TQ),
        grid=(b, s // TQ),
        in_specs=[pl.BlockSpec((1, TQ, Q_WIDTH), lambda a, i: (a, i, 0)),
                  pl.BlockSpec((1, s, 2 * kw), lambda a, i: (a, 0, 0)),
                  pl.BlockSpec((1, kw, s), lambda a, i: (a, 0, 0)),
                  pl.BlockSpec((1, LANES), lambda a, i: (0, 0))],
        out_specs=pl.BlockSpec((1, TQ, Q_WIDTH), lambda a, i: (a, i, 0)),
        out_shape=jax.ShapeDtypeStruct((b, s, Q_WIDTH), F32),
        scratch_shapes=[pltpu.VMEM((s, SWA_KV_HEADS * LANES), BF16),
                        pltpu.VMEM((s // TQ, kw, TQ), BF16)],
        compiler_params=_cparams(2),
        name="swa_prompt",
    )(q, kv, v_t, sink)


def _gelu_tanh(x):
    return x * (0.5 * (1.0 + jnp.tanh(0.7978845608028654 * (x + 0.044715 * (x * x * x)))))


def _compress_rows(xk_ref, xv_ref, n_blk, pe_ref, w1k_ref, w1v_ref, w2_ref):
    half = NSA_KV_HEADS * HEAD_DIM

    def body(p, carry):
        ak, av = carry
        pe = pe_ref[pl.ds(p, 1), :]
        xk = (xk_ref[pl.ds(p, n_blk, stride=NSA_BLOCK), :] + pe[:, 0:half]).astype(BF16)
        xv = (xv_ref[pl.ds(p, n_blk, stride=NSA_BLOCK), :] + pe[:, half:2 * half]).astype(BF16)
        ak = ak + _dot(xk, w1k_ref[p])
        av = av + _dot(xv, w1v_ref[p])
        return ak, av

    zero = jnp.zeros((n_blk, NSA_KV_HEADS * NSA_CMP_HIDDEN), F32)
    ak, av = lax.fori_loop(0, NSA_BLOCK, body, (zero, zero))
    hid = jnp.concatenate([_gelu_tanh(ak), _gelu_tanh(av)], axis=1).astype(BF16)
    return _dot(hid, w2_ref[...])


def _nsa_compress_seq_kernel(xk_ref, xv_ref, pe_ref, w1k_ref, w1v_ref, w2_ref, o_ref, *, n_blk):
    o_ref[...] = _compress_rows(xk_ref, xv_ref, n_blk, pe_ref, w1k_ref, w1v_ref, w2_ref)


def _nsa_compress_seq(rows2d, pe2, w1k, w1v, w2):
    t = rows2d.shape[0]
    n_blk = min(64, t // NSA_BLOCK)
    tr = n_blk * NSA_BLOCK
    half = 2 * NSA_KV_HEADS * HEAD_DIM
    full = lambda shape: pl.BlockSpec(shape, lambda j: (0,) * len(shape))
    return pl.pallas_call(
        functools.partial(_nsa_compress_seq_kernel, n_blk=n_blk),
        grid=(t // tr,),
        in_specs=[pl.BlockSpec((tr, LANES), lambda j: (j, 0)),
                  pl.BlockSpec((tr, LANES), lambda j: (j, 1)),
                  full(pe2.shape), full(w1k.shape), full(w1v.shape), full(w2.shape)],
        out_specs=pl.BlockSpec((n_blk, half), lambda j: (j, 0)),
        out_shape=jax.ShapeDtypeStruct((t // NSA_BLOCK, half), F32),
        compiler_params=_cparams(1),
        name="nsa_compress_seq",
    )(rows2d, rows2d, pe2, w1k, w1v, w2)


def _nsa_compress_pages_kernel(pt_ref, *refs, n_pages):
    page_refs = refs[:n_pages]
    pek_ref, pev_ref, w1k_ref, w1v_ref, w2k_ref, w2v_ref, o_ref, xk_sc, xv_sc = refs[n_pages:]
    for j in range(n_pages):
        for u in range(NSA_KV_HEADS):
            r0 = (j * NSA_KV_HEADS + u) * TILE_PITCH
            xk_sc[r0:r0 + HEAD_DIM, :] = page_refs[j][0, u]
            xv_sc[r0:r0 + HEAD_DIM, :] = page_refs[j][1, u]
    n_rows = n_pages * NSA_KV_HEADS

    def hidden(x_sc, pe_ref, w1_ref):
        acc = None
        for d0 in range(0, HEAD_DIM, COMPRESS_GROUP):
            x = jnp.concatenate(
                [(x_sc[pl.ds(d, n_rows, stride=TILE_PITCH), :] + pe_ref[d:d + 1, :]).astype(BF16)
                 for d in range(d0, d0 + COMPRESS_GROUP)], axis=1)
            part = _dot(x, w1_ref[d0 * PAGE_SIZE:(d0 + COMPRESS_GROUP) * PAGE_SIZE, :])
            acc = part if acc is None else acc + part
        return _gelu_tanh(acc).astype(BF16)

    kc = _dot(hidden(xk_sc, pek_ref, w1k_ref), w2k_ref[...])
    vc = _dot(hidden(xv_sc, pev_ref, w1v_ref), w2v_ref[...])
    o_ref[0] = jnp.concatenate([kc, vc], axis=1)


def _nsa_page_spec(j, comp_block):
    return pl.BlockSpec((None, 2, NSA_KV_HEADS, HEAD_DIM, PAGE_SIZE), lambda b, pt: (pt[b, j], comp_block, 0, 0, 0))


def _nsa_compress_pages(pt, cache_t, pek_t, pev_t, w1k_t, w1v_t, w2k, w2v):
    db, n_pages = pt.shape
    n_rows = n_pages * NSA_KV_HEADS
    full = lambda shape: pl.BlockSpec(shape, lambda b, pt: (0,) * len(shape))
    consts = (pek_t, pev_t, w1k_t, w1v_t, w2k, w2v)
    grid_spec = pltpu.PrefetchScalarGridSpec(
        num_scalar_prefetch=1,
        grid=(db,),
        in_specs=[_nsa_page_spec(j, 0) for j in range(n_pages)] + [full(c.shape) for c in consts],
        out_specs=pl.BlockSpec((1, n_rows, 2 * LANES), lambda b, pt: (b, 0, 0)),
        scratch_shapes=[pltpu.VMEM((n_rows * TILE_PITCH, PAGE_SIZE), F32),
                        pltpu.VMEM((n_rows * TILE_PITCH, PAGE_SIZE), F32)],
    )
    return pl.pallas_call(
        functools.partial(_nsa_compress_pages_kernel, n_pages=n_pages),
        grid_spec=grid_spec,
        out_shape=jax.ShapeDtypeStruct((db, n_rows, 2 * LANES), F32),
        compiler_params=_cparams(1),
        name="nsa_compress_pages",
    )(pt, *([cache_t] * n_pages), *consts)


def _nsa_prompt_kernel(qraw_ref, qrot_ref, ks_ref, kw_ref, vt_ref, kc_ref, vct_ref, gates_ref, o_ref,
                       kslc_sc, kwin_sc, vt_sc, bias_sc, ocmp_sc, *, n_chunks, n_blocks):
    i = pl.program_id(1)
    kvh_n = NSA_KV_HEADS
    grp = N_HEADS // kvh_n
    sub = TQ // NSA_BLOCK

    @pl.when(i == 0)
    def _():
        for n in range(n_chunks):
            vt_sc[n] = vt_ref[0, :, n * TQ:(n + 1) * TQ].astype(BF16)

        def prep(n, carry):
            start = pl.multiple_of(n * TQ, TQ)
            ks = ks_ref[0, pl.ds(start, TQ), :]
            kwn = kw_ref[0, pl.ds(start, TQ), :]
            for j in range(kvh_n):
                kslc_sc[pl.ds(start, TQ), j * LANES:(j + 1) * LANES] = _slot(ks, j).astype(BF16)
                kwin_sc[pl.ds(start, TQ), j * LANES:(j + 1) * LANES] = _slot(kwn, j).astype(BF16)
            return carry

        lax.fori_loop(0, n_chunks, prep, 0)

    lane = _iota((TQ, LANES), 1)
    q_raw = qraw_ref[0]
    kcb = kc_ref[0].astype(BF16)
    vctb = vct_ref[0].astype(BF16)
    row_io = _iota((LANES, TQ), 0)
    n_io = row_io % HEAD_DIM
    t = i * TQ + _iota((LANES, TQ), 1)
    avail = ((n_io + 1) * NSA_BLOCK - 1 <= t) & (n_io < n_blocks)
    score = jnp.zeros((LANES, TQ), F32)
    for h in range(N_HEADS):
        qh = (jnp.where(lane < HEAD_DIM, _head_slot(q_raw, h), 0.0) * ATTN_SCALE).astype(BF16)
        ok = avail & ((row_io // HEAD_DIM) == (h // grp))
        s = jnp.where(ok, _dot_nt(kcb, qh), MASKED)
        m = jnp.max(s, axis=0, keepdims=True)
        p = jnp.exp(s - m) * ok.astype(F32)
        p = p / jnp.maximum(jnp.sum(p, axis=0, keepdims=True), 1e-30)
        score = score + p
        ocmp_sc[h * HEAD_DIM:(h + 1) * HEAD_DIM, :] = _dot(vctb, p.astype(BF16))

    own = t // NSA_BLOCK
    forced = ((n_io == 0) | (n_io == own) | (n_io == own - 1)) & (n_io < n_blocks)
    sc = jnp.where(avail, score, -jnp.inf)
    sc = jnp.where(forced, jnp.inf, sc)
    rank = jnp.zeros((LANES, TQ), jnp.int32)
    for m_idx in range(n_blocks):
        row = jnp.where(row_io < HEAD_DIM, sc[m_idx:m_idx + 1, :], sc[HEAD_DIM + m_idx:HEAD_DIM + m_idx + 1, :])
        beats = (row > sc) | ((row == sc) & (m_idx < n_io))
        rank = rank + beats.astype(jnp.int32)
    bias = jnp.where((rank < NSA_TOPN) & (sc > -jnp.inf), 0.0, MASKED)
    for gi in range(grp):
        bias_sc[:, gi * TQ:(gi + 1) * TQ] = bias

    q_rot = qrot_ref[0]
    gates_t = gates_ref[0].T
    lo_win = jnp.maximum(i - _window_chunks(NSA_WINDOW), 0)
    for j in range(kvh_n):
        qg = _stack_queries(q_rot, range(j * grp, (j + 1) * grp))

        def block_bias(n, s, j=j):
            return jnp.concatenate(
                [s[u * NSA_BLOCK:(u + 1) * NSA_BLOCK] + bias_sc[pl.ds(j * HEAD_DIM + n * sub + u, 1), :]
                 for u in range(sub)], axis=0)

        acc_s, _, l_s = _flash_t(qg, kslc_sc, j * LANES, vt_sc, j * HEAD_DIM, i, 0, None, False, block_bias)
        acc_w, _, l_w = _flash_t(qg, kwin_sc, j * LANES, vt_sc, (kvh_n + j) * HEAD_DIM, i, lo_win,
                                 NSA_WINDOW, True)
        o_s = acc_s / l_s
        o_w = acc_w / l_w
        mixed = []
        for gi in range(grp):
            h = j * grp + gi
            cols = slice(gi * TQ, (gi + 1) * TQ)
            mixed.append(gates_t[3 * h:3 * h + 1] * ocmp_sc[h * HEAD_DIM:(h + 1) * HEAD_DIM, :]
                         + gates_t[3 * h + 1:3 * h + 2] * o_s[:, cols]
                         + gates_t[3 * h + 2:3 * h + 3] * o_w[:, cols])
        _store_heads(o_ref, jnp.concatenate(mixed, axis=1), j * grp, grp)


def _nsa_prompt(q_raw, q_rot, rows, win, v_t, kc, vc_t, gates):
    b, s, _ = q_raw.shape
    kw = NSA_KV_HEADS * HEAD_DIM
    n_chunks = s // TQ
    n_blocks = s // NSA_BLOCK
    assert s % TQ == 0 and n_blocks <= HEAD_DIM and kw == LANES
    grp = N_HEADS // NSA_KV_HEADS
    tile = lambda w: pl.BlockSpec((1, TQ, w), lambda a, i: (a, i, 0))
    per_b = lambda shape, lane_block=0: pl.BlockSpec((1,) + shape, lambda a, i: (a, 0, lane_block))
    return pl.pallas_call(
        functools.partial(_nsa_prompt_kernel, n_chunks=n_chunks, n_blocks=n_blocks),
        grid=(b, n_chunks),
        in_specs=[tile(Q_WIDTH), tile(Q_WIDTH),
                  per_b((s, kw), 2), per_b((s, kw), 0), per_b((2 * kw, s)),
                  per_b((LANES, LANES)), per_b((HEAD_DIM, LANES)), tile(LANES)],
        out_specs=tile(Q_WIDTH),
        out_shape=jax.ShapeDtypeStruct((b, s, Q_WIDTH), F32),
        scratch_shapes=[pltpu.VMEM((s, NSA_KV_HEADS * LANES), BF16),
                        pltpu.VMEM((s, NSA_KV_HEADS * LANES), BF16),
                        pltpu.VMEM((n_chunks, 2 * kw, TQ), BF16),
                        pltpu.VMEM((LANES, grp * TQ), F32),
                        pltpu.VMEM((Q_WIDTH, TQ), F32)],
        compiler_params=_cparams(2),
        name="nsa_prompt",
    )(q_raw, q_rot, rows, win, v_t, kc, vc_t, gates)


def _fold_heads(o, kvh_n):
    grp = N_HEADS // kvh_n
    own = (_iota(o.shape, 1) // HEAD_DIM) == (_iota(o.shape, 0) // grp)
    t = jnp.where(own, o, 0.0)
    width = kvh_n * HEAD_DIM
    while width > HEAD_DIM:
        width //= 2
        t = t + pltpu.roll(t, width, 1)
    return t


def _rank_desc(vals, n_cand, n_io):
    rank = jnp.zeros(vals.shape, jnp.int32)
    for m_idx in range(n_cand):
        col = vals[:, m_idx:m_idx + 1]
        beats = (col > vals) | ((col == vals) & (m_idx < n_io))
        rank = rank + beats.astype(jnp.int32)
    return rank


def _moba_decode_kernel(pt_ref, *refs, n_pages):
    page_refs = refs[:n_pages]
    q_ref, new_ref, o_ref, s_sc, v_sc = refs[n_pages:]
    kw = MOBA_KV_HEADS * HEAD_DIM
    pages_per_block = MOBA_BLOCK // PAGE_SIZE
    n_blk = n_pages // pages_per_block
    qs = q_ref[0] * ATTN_SCALE
    qb = qs.astype(BF16)
    lane = _iota((N_HEADS, LANES), 1)
    gate = jnp.zeros((N_HEADS, LANES), F32)
    gsum = None
    for j in range(n_pages):
        s = _dot(qb, page_refs[j][0].reshape(kw, PAGE_SIZE).astype(BF16))
        s_sc[:, j * PAGE_SIZE:(j + 1) * PAGE_SIZE] = s
        v_sc[j] = page_refs[j][1].reshape(kw, PAGE_SIZE).astype(BF16)
        part = jnp.sum(s, axis=1, keepdims=True)
        gsum = part if j % pages_per_block == 0 else gsum + part
        if j % pages_per_block == pages_per_block - 1:
            gate = jnp.where(lane == j // pages_per_block, gsum, gate)
    eligible = lane < n_blk
    gate = jnp.where(eligible, gate, GATE_FLOOR)
    sel = eligible & (_rank_desc(gate, n_blk, lane) < MOBA_TOPK)
    bias = jnp.where(sel, 0.0, MASKED)
    new = new_ref[0]
    s_new = jnp.sum(qs * new[:, 0:kw], axis=1, keepdims=True)
    m = s_new
    for n in range(n_blk):
        cols = slice(n * MOBA_BLOCK, (n + 1) * MOBA_BLOCK)
        sb = s_sc[:, cols] + bias[:, n:n + 1]
        s_sc[:, cols] = sb
        m = jnp.maximum(m, jnp.max(sb, axis=1, keepdims=True))
    den = jnp.exp(s_new - m)
    o = den * new[:, kw:2 * kw]
    for j in range(n_pages):
        p = jnp.exp(s_sc[:, j * PAGE_SIZE:(j + 1) * PAGE_SIZE] - m)
        den = den + jnp.sum(p, axis=1, keepdims=True)
        o = o + _dot_nt(p.astype(BF16), v_sc[j])
    o_ref[0] = _fold_heads(o / den, MOBA_KV_HEADS)


def _moba_decode(pt, cache_t, qbd, new):
    db, n_pages = pt.shape
    kw = MOBA_KV_HEADS * HEAD_DIM
    assert (n_pages * PAGE_SIZE) % MOBA_BLOCK == 0 and n_pages * PAGE_SIZE // MOBA_BLOCK <= LANES
    page = lambda j: pl.BlockSpec((None, 2, MOBA_KV_HEADS, HEAD_DIM, PAGE_SIZE), lambda b, pt: (pt[b, j], 0, 0, 0, 0))
    grid_spec = pltpu.PrefetchScalarGridSpec(
        num_scalar_prefetch=1,
        grid=(db,),
        in_specs=[page(j) for j in range(n_pages)]
        + [pl.BlockSpec((1, N_HEADS, kw), lambda b, pt: (b, 0, 0)),
           pl.BlockSpec((1, 1, 2 * kw), lambda b, pt: (b, 0, 0))],
        out_specs=pl.BlockSpec((1, N_HEADS, kw), lambda b, pt: (b, 0, 0)),
        scratch_shapes=[pltpu.VMEM((N_HEADS, n_pages * PAGE_SIZE), F32),
                        pltpu.VMEM((n_pages, kw, PAGE_SIZE), BF16)],
    )
    return pl.pallas_call(
        functools.partial(_moba_decode_kernel, n_pages=n_pages),
        grid_spec=grid_spec,
        out_shape=jax.ShapeDtypeStruct((db, N_HEADS, kw), F32),
        compiler_params=_cparams(1),
        name="moba_decode",
    )(pt, *([cache_t] * n_pages), qbd, new)


def _swa_decode_kernel(buf_ref, new_ref, q_ref, sink_ref, o_ref):
    kw = SWA_KV_HEADS * HEAD_DIM
    wb = buf_ref.shape[-1]
    qs = q_ref[0] * ATTN_SCALE
    new = new_ref[0]
    s = _dot(qs.astype(BF16), buf_ref[0, 0].reshape(kw, wb).astype(BF16))
    s_new = jnp.sum(qs * new[:, 0:kw], axis=1, keepdims=True)
    sink = sink_ref[:, 0:1]
    m = jnp.maximum(jnp.maximum(jnp.max(s, axis=1, keepdims=True), s_new), sink)
    p = jnp.exp(s - m)
    p_new = jnp.exp(s_new - m)
    den = jnp.sum(p, axis=1, keepdims=True) + p_new + jnp.exp(sink - m)
    o = (_dot_nt(p.astype(BF16), buf_ref[0, 1].reshape(kw, wb).astype(BF16)) + p_new * new[:, kw:2 * kw]) / den
    o_ref[0] = _fold_heads(o, SWA_KV_HEADS)


def _swa_decode(buf_t, new, qbd, sink):
    db = buf_t.shape[0]
    wb = buf_t.shape[-1]
    kw = SWA_KV_HEADS * HEAD_DIM
    return pl.pallas_call(
        _swa_decode_kernel,
        grid=(db,),
        in_specs=[pl.BlockSpec((1, 2, SWA_KV_HEADS, HEAD_DIM, wb), lambda b: (b, 0, 0, 0, 0)),
                  pl.BlockSpec((1, 1, 2 * kw), lambda b: (b, 0, 0)),
                  pl.BlockSpec((1, N_HEADS, kw), lambda b: (b, 0, 0)),
                  pl.BlockSpec((N_HEADS, LANES), lambda b: (0, 0))],
        out_specs=pl.BlockSpec((1, N_HEADS, kw), lambda b: (b, 0, 0)),
        out_shape=jax.ShapeDtypeStruct((db, N_HEADS, kw), F32),
        compiler_params=_cparams(1),
        name="swa_decode",
    )(buf_t, new, qbd, sink)


def _nsa_decode_kernel(pt_ref, *refs, n_pages):
    page_refs = refs[:n_pages]
    (kcvc_ref, win_ref, qraw_ref, qrot_ref, newslc_ref, newwin_ref, gates_ref, o_ref, s_sc, v_sc) = refs[n_pages:]
    kw = NSA_KV_HEADS * HEAD_DIM
    grp = N_HEADS // NSA_KV_HEADS
    blocks_per_page = PAGE_SIZE // NSA_BLOCK
    n_blk = n_pages * blocks_per_page
    wb = win_ref.shape[-1]
    q_raw = (qraw_ref[0] * ATTN_SCALE).astype(BF16)
    q_rot = qrot_ref[0] * ATTN_SCALE
    q_rot_b = q_rot.astype(BF16)

    kcvc = kcvc_ref[0]
    s_c = _dot_nt(q_raw, kcvc[:, 0:kw].astype(BF16))
    p_c = jnp.exp(s_c - jnp.max(s_c, axis=1, keepdims=True))
    p_c = p_c / jnp.sum(p_c, axis=1, keepdims=True)
    o_cmp = _dot(p_c.astype(BF16), kcvc[:, kw:2 * kw].astype(BF16))

    head = _iota(p_c.shape, 0)
    sc0 = jnp.sum(jnp.where(head < grp, p_c, 0.0), axis=0, keepdims=True)
    sc1 = jnp.sum(jnp.where(head >= grp, p_c, 0.0), axis=0, keepdims=True)
    r8 = _iota((SUBLANES, n_blk), 0)
    n_io = _iota((SUBLANES, n_blk), 1)
    sc = jnp.where(r8 == 0, sc0, jnp.where(r8 == 1, sc1, -jnp.inf))
    forced = (n_io == 0) | (n_io == n_blk - 1)
    sc = jnp.where(forced, jnp.inf, sc)
    sel = forced | (_rank_desc(sc, n_blk, n_io) < NSA_TOPN - 1)
    bias2 = jnp.where(sel, 0.0, MASKED)
    bias = jnp.where(head < grp, bias2[0:1], bias2[1:2])

    new_slc = newslc_ref[0]
    s_new = jnp.sum(q_rot * new_slc[:, 0:kw], axis=1, keepdims=True)
    m = s_new
    lane = _iota((N_HEADS, PAGE_SIZE), 1)
    for j in range(n_pages):
        sj = _dot(q_rot_b, page_refs[j][0].reshape(kw, PAGE_SIZE).astype(BF16))
        bj = bias[:, blocks_per_page * j:blocks_per_page * j + 1]
        for u in range(1, blocks_per_page):
            bj = jnp.where(lane < u * NSA_BLOCK, bj, bias[:, blocks_per_page * j + u:blocks_per_page * j + u + 1])
        sj = sj + bj
        s_sc[:, j * PAGE_SIZE:(j + 1) * PAGE_SIZE] = sj
        v_sc[j] = page_refs[j][1].reshape(kw, PAGE_SIZE).astype(BF16)
        m = jnp.maximum(m, jnp.max(sj, axis=1, keepdims=True))
    den = jnp.exp(s_new - m)
    o_slc = den * new_slc[:, kw:2 * kw]
    for j in range(n_pages):
        p = jnp.exp(s_sc[:, j * PAGE_SIZE:(j + 1) * PAGE_SIZE] - m)
        den = den + jnp.sum(p, axis=1, keepdims=True)
        o_slc = o_slc + _dot_nt(p.astype(BF16), v_sc[j])
    o_slc = o_slc / den

    new_win = newwin_ref[0]
    s_w = _dot(q_rot_b, win_ref[0, 0].reshape(kw, wb).astype(BF16))
    s_wn = jnp.sum(q_rot * new_win[:, 0:kw], axis=1, keepdims=True)
    m_w = jnp.maximum(jnp.max(s_w, axis=1, keepdims=True), s_wn)
    p_w = jnp.exp(s_w - m_w)
    p_wn = jnp.exp(s_wn - m_w)
    den_w = jnp.sum(p_w, axis=1, keepdims=True) + p_wn
    o_win = (_dot_nt(p_w.astype(BF16), win_ref[0, 1].reshape(kw, wb).astype(BF16))
             + p_wn * new_win[:, kw:2 * kw]) / den_w

    g = gates_ref[0]
    o = g[:, 0:1] * o_cmp + g[:, 1:2] * o_slc + g[:, 2:3] * o_win
    o_ref[0] = _fold_heads(o, NSA_KV_HEADS)


def _nsa_decode(pt, cache_t, kcvc, win_t, q_raw_bd, q_rot_bd, new_rows, new_win, gates):
    db, n_pages = pt.shape
    kw = NSA_KV_HEADS * HEAD_DIM
    n_blk = kcvc.shape[1]
    wb = win_t.shape[-1]
    per_b = lambda shape: pl.BlockSpec((1,) + shape, lambda b, pt: (b,) + (0,) * len(shape))
    grid_spec = pltpu.PrefetchScalarGridSpec(
        num_scalar_prefetch=1,
        grid=(db,),
        in_specs=[_nsa_page_spec(j, 1) for j in range(n_pages)]
        + [per_b((n_blk, 2 * kw)), per_b((2, NSA_KV_HEADS, HEAD_DIM, wb)), per_b((N_HEADS, kw)), per_b((N_HEADS, kw)),
           pl.BlockSpec((1, 1, 2 * kw), lambda b, pt: (b, 0, 1)), per_b((1, 2 * kw)), per_b((N_HEADS, LANES))],
        out_specs=per_b((N_HEADS, kw)),
        scratch_shapes=[pltpu.VMEM((N_HEADS, n_pages * PAGE_SIZE), F32),
                        pltpu.VMEM((n_pages, kw, PAGE_SIZE), BF16)],
    )
    return pl.pallas_call(
        functools.partial(_nsa_decode_kernel, n_pages=n_pages),
        grid_spec=grid_spec,
        out_shape=jax.ShapeDtypeStruct((db, N_HEADS, kw), F32),
        compiler_params=_cparams(1),
        name="nsa_decode",
    )(pt, *([cache_t] * n_pages), kcvc, win_t, q_raw_bd, q_rot_bd, new_rows, new_win, gates)


def _rope_tables(pos):
    half = HEAD_DIM // 2
    inv = ROPE_THETA ** (-jnp.arange(half, dtype=F32) / half)
    ang = pos.astype(F32)[:, None] * inv[None, :]
    cos = jnp.cos(ang)
    sin = jnp.sin(ang)
    reps = LANES // HEAD_DIM
    return jnp.tile(cos, (1, 2 * reps)), jnp.tile(jnp.concatenate([-sin, sin], 1), (1, reps))


def _block_diag_heads(q, kvh_n):
    r = q.shape[0]
    grp = N_HEADS // kvh_n
    own = (jnp.arange(N_HEADS)[:, None] // grp == jnp.arange(kvh_n)[None, :]).astype(q.dtype)
    q3 = q.reshape(r, N_HEADS, 1, HEAD_DIM) * own[None, :, :, None]
    return q3.reshape(r, N_HEADS, kvh_n * HEAD_DIM)


def _heads_from_folded(o):
    return o[:, :, :HEAD_DIM].reshape(1, o.shape[0], Q_WIDTH)


def _pad_cols(w, n):
    return jnp.pad(w, ((0, 0), (0, n - w.shape[1])))


def _block_diag(blocks):
    rows = sum(b.shape[0] for b in blocks)
    cols = sum(b.shape[1] for b in blocks)
    out = jnp.zeros((rows, cols), blocks[0].dtype)
    r = c = 0
    for b in blocks:
        out = out.at[r:r + b.shape[0], c:c + b.shape[1]].set(b)
        r += b.shape[0]
        c += b.shape[1]
    return out


def _positions_minor(x):
    nd = x.ndim
    return jnp.transpose(x, tuple(range(nd - 4)) + (nd - 3, nd - 2, nd - 1, nd - 4))


def _nsa_compress_weights_seq(pe_k, pe_v, w1_k, w2_k, w1_v, w2_v):
    pe2 = jnp.concatenate([pe_k, pe_k, pe_v, pe_v], axis=1)

    def pair(w1):
        w = w1.reshape(NSA_BLOCK, HEAD_DIM, NSA_CMP_HIDDEN)
        z = jnp.zeros_like(w)
        return jnp.concatenate([jnp.concatenate([w, z], 2), jnp.concatenate([z, w], 2)], 1).astype(BF16)

    w2 = _block_diag([w2_k, w2_k, w2_v, w2_v]).astype(BF16)
    return pe2, pair(w1_k), pair(w1_v), w2


def _nsa_compress_weights_pages(pe_k, pe_v, w1_k, w2_k, w1_v, w2_v):
    reps = PAGE_SIZE // NSA_BLOCK

    def pair(w1):
        w = w1.reshape(NSA_BLOCK, HEAD_DIM, NSA_CMP_HIDDEN).transpose(1, 0, 2)
        z = jnp.zeros_like(w)
        w = jnp.concatenate([jnp.concatenate([w, z], 2), jnp.concatenate([z, w], 2)], 1)
        return w.reshape(HEAD_DIM * PAGE_SIZE, -1).astype(BF16)

    return (jnp.tile(pe_k.T, (1, reps)), jnp.tile(pe_v.T, (1, reps)), pair(w1_k), pair(w1_v),
            _block_diag([w2_k] * reps).astype(BF16), _block_diag([w2_v] * reps).astype(BF16))


_MOBA_OUTS = (((0, Q_WIDTH, "rope"),),
              ((Q_WIDTH, 256, "rope"), (Q_WIDTH + 256, 256, "raw")))
_SWA_OUTS = (((0, Q_WIDTH, "rope"),),
             ((Q_WIDTH, 128, "rope"), (Q_WIDTH + 128, 128, "raw")))
_NSA_OUTS = (((0, Q_WIDTH, "raw"),),
             ((0, Q_WIDTH, "rope"),),
             ((Q_WIDTH, 128, "raw"), (Q_WIDTH + 128, 128, "raw"), (Q_WIDTH + 256, 128, "rope"),
              (Q_WIDTH + 384, 128, "raw")),
             ((Q_WIDTH + 512, 128, "rope"), (Q_WIDTH + 640, 128, "raw")),
             ((Q_WIDTH + 768, 128, "sigmoid"),))
_NSA_IN_PADDED = Q_WIDTH + 7 * 128


def kernel(x_prompt, x_sample, cache_moba_kv, state_swa_kv, cache_nsa_kv, state_nsa_win_kv, state_ffn_conv, page_table, c_prompt, c_sample, ada_w, ada_b, ln_g, ln_b, moba_w_in, moba_w_o, swa_w_in, swa_w_o, swa_sink, nsa_w_in, nsa_w_o, nsa_pe_k, nsa_pe_v, nsa_w1_k, nsa_w2_k, nsa_w1_v, nsa_w2_v, ffn_w_gate, ffn_w_up, ffn_conv_w, ffn_conv_b, ffn_w_down):
    b, s, d = x_prompt.shape
    db, dec_seq, _ = x_sample.shape
    depth = ada_w.shape[0]
    ff = ffn_w_gate.shape[2]
    n_pages = page_table.shape[1]
    n_phys = cache_moba_kv.shape[1]
    past_len = n_pages * PAGE_SIZE
    assert dec_seq == 1 and past_len % MOBA_BLOCK == 0 and s % 512 == 0
    assert state_swa_kv.shape[2] <= SWA_WINDOW and state_nsa_win_kv.shape[2] <= NSA_WINDOW
    alpha = (2 * depth) ** 0.25
    tm = 512
    tf = ff // 2

    rows = b + db
    rows_pad = -(-rows // SUBLANES) * SUBLANES
    c_all = jnp.concatenate([c_prompt, c_sample, jnp.zeros((rows_pad - rows, d), F32)], 0)
    mod = _adaln_all(c_all, ada_w.reshape(depth * 2, d, 3 * d), ada_b.reshape(depth * 2, 1, 3 * d))

    def modulation(i, sub):
        m = mod[i * 2 + sub]
        mp = m[:b].reshape(b, 1, 3 * d)
        ms = m[b:b + db].reshape(1, db, 3 * d)
        return ([mp[..., k * d:(k + 1) * d] for k in range(3)], [ms[..., k * d:(k + 1) * d] for k in range(3)])

    cos_p, sin_p = _rope_tables(jnp.arange(s, dtype=jnp.int32))
    cos_s, sin_s = _rope_tables(jnp.full((db,), past_len, jnp.int32))

    moba_cache_t = _positions_minor(cache_moba_kv).reshape(-1, 2, MOBA_KV_HEADS, HEAD_DIM, PAGE_SIZE)
    nsa_cache_t = _positions_minor(cache_nsa_kv).reshape(-1, 4, NSA_KV_HEADS, HEAD_DIM, PAGE_SIZE)

    xp = x_prompt
    xs = x_sample.reshape(1, db, d)
    moba_p, moba_s, swa_p, swa_s, nsa_p, nsa_s, nsaw_p, nsaw_s, conv_p, conv_s = ([] for _ in range(10))
    for i in range(depth):
        kind, j = i % N_MIXERS, i // N_MIXERS
        (sh_p, sc_p, gt_p), (sh_s, sc_s, gt_s) = modulation(i, 0)
        if kind == 0:
            w_in = moba_w_in[j].astype(BF16)
            q_p, kv_p, vt_p = _inproj(xp, sh_p, sc_p, w_in, cos_p, sin_p, _MOBA_OUTS, tm,
                                      w_in[:, Q_WIDTH + 256:Q_WIDTH + 512].T)
            q_s, kv_s = _inproj(xs, sh_s, sc_s, w_in, cos_s, sin_s, _MOBA_OUTS, db)
            o_p = _moba_prompt(q_p, kv_p, vt_p)
            o_s = _heads_from_folded(_moba_decode(page_table + j * n_phys, moba_cache_t,
                                                  _block_diag_heads(q_s[0], MOBA_KV_HEADS), kv_s.reshape(db, 1, -1)))
            moba_p.append(kv_p.reshape(b, s, 2, MOBA_KV_HEADS, HEAD_DIM))
            moba_s.append(kv_s.reshape(db, 1, 2, MOBA_KV_HEADS, HEAD_DIM))
            w_o = moba_w_o[j]
        elif kind == 1:
            w_in = swa_w_in[j].astype(BF16)
            q_p, kv_p, vt_p = _inproj(xp, sh_p, sc_p, w_in, cos_p, sin_p, _SWA_OUTS, tm,
                                      w_in[:, Q_WIDTH + 128:Q_WIDTH + 256].T)
            q_s, kv_s = _inproj(xs, sh_s, sc_s, w_in, cos_s, sin_s, _SWA_OUTS, db)
            sink = swa_sink[j]
            o_p = _swa_prompt(q_p, kv_p, vt_p, _pad_cols(sink[None, :], LANES))
            buf = state_swa_kv[j]
            new = kv_s.reshape(db, 1, -1)
            o_s = _heads_from_folded(_swa_decode(_positions_minor(buf), new, _block_diag_heads(q_s[0], SWA_KV_HEADS),
                                                 jnp.broadcast_to(sink[:, None], (N_HEADS, LANES))))
            swa_p.append(kv_p[:, s - min(SWA_WINDOW, s):].reshape(b, -1, 2, SWA_KV_HEADS, HEAD_DIM))
            swa_s.append(jnp.concatenate([buf[:, 1:], new.reshape(db, 1, 2, SWA_KV_HEADS, HEAD_DIM)], 1))
            w_o = swa_w_o[j]
        else:
            w_in = _pad_cols(nsa_w_in[j], _NSA_IN_PADDED).astype(BF16)
            nsa_w = (nsa_pe_k[j], nsa_pe_v[j], nsa_w1_k[j], nsa_w2_k[j], nsa_w1_v[j], nsa_w2_v[j])
            w_t = jnp.concatenate([w_in[:, Q_WIDTH + 384:Q_WIDTH + 512], w_in[:, Q_WIDTH + 640:Q_WIDTH + 768]], 1).T
            qraw_p, qrot_p, rows_p, win_p, gates_p, vt_p = _inproj(xp, sh_p, sc_p, w_in, cos_p, sin_p, _NSA_OUTS,
                                                                   tm, w_t)
            qraw_s, qrot_s, rows_s, win_s, gates_s = _inproj(xs, sh_s, sc_s, w_in, cos_s, sin_s, _NSA_OUTS, db)
            n_blk = s // NSA_BLOCK
            cmp_p = _nsa_compress_seq(rows_p.reshape(b * s, -1), *_nsa_compress_weights_seq(*nsa_w))
            cmp_p = cmp_p.reshape(b, n_blk, 2, NSA_KV_HEADS, HEAD_DIM).transpose(2, 0, 3, 1, 4)
            cmp_p = jnp.pad(cmp_p, ((0, 0), (0, 0), (0, 0), (0, HEAD_DIM - n_blk), (0, 0)))
            cmp_p = cmp_p.reshape(2, b, NSA_KV_HEADS * HEAD_DIM, HEAD_DIM)
            kc_p = jnp.pad(cmp_p[0], ((0, 0), (0, 0), (0, LANES - HEAD_DIM)))
            vct_p = cmp_p[1].transpose(0, 2, 1)
            o_p = _nsa_prompt(qraw_p, qrot_p, rows_p, win_p, vt_p, kc_p, vct_p, gates_p)
            pt = page_table + j * n_phys
            kcvc = _nsa_compress_pages(pt, nsa_cache_t, *_nsa_compress_weights_pages(*nsa_w))
            kcvc = kcvc.reshape(db, n_pages, NSA_KV_HEADS, 2, PAGE_SIZE // NSA_BLOCK, HEAD_DIM)
            kcvc = kcvc.transpose(0, 1, 4, 3, 2, 5).reshape(db, -1, 2 * NSA_KV_HEADS * HEAD_DIM)
            win_buf = state_nsa_win_kv[j]
            new_win = win_s.reshape(db, 1, -1)
            gates3 = jnp.pad(gates_s[0, :, :3 * N_HEADS].reshape(db, N_HEADS, 3), ((0, 0), (0, 0), (0, LANES - 3)))
            o_s = _heads_from_folded(_nsa_decode(pt, nsa_cache_t, kcvc, _positions_minor(win_buf),
                                                 _block_diag_heads(qraw_s[0], NSA_KV_HEADS),
                                                 _block_diag_heads(qrot_s[0], NSA_KV_HEADS),
                                                 rows_s.reshape(db, 1, -1), new_win, gates3))
            nsa_p.append(rows_p.reshape(b, s, 4, NSA_KV_HEADS, HEAD_DIM))
            nsa_s.append(rows_s.reshape(db, 1, 4, NSA_KV_HEADS, HEAD_DIM))
            nsaw_p.append(win_p[:, s - min(NSA_WINDOW, s):].reshape(b, -1, 2, NSA_KV_HEADS, HEAD_DIM))
            nsaw_s.append(jnp.concatenate([win_buf[:, 1:], new_win.reshape(db, 1, 2, NSA_KV_HEADS, HEAD_DIM)], 1))
            w_o = nsa_w_o[j]
        w_o = w_o.astype(BF16)
        g0, b0 = ln_g[i, 0][None, :], ln_b[i, 0][None, :]
        xp = _outproj_ln(o_p, w_o, xp, gt_p, g0, b0, alpha, tm)
        xs = _outproj_ln(o_s, w_o, xs, gt_s, g0, b0, alpha, db)

        (sh_p, sc_p, gt_p), (sh_s, sc_s, gt_s) = modulation(i, 1)
        wg, wu, wd = ffn_w_gate[i].astype(BF16), ffn_w_up[i].astype(BF16), ffn_w_down[i].astype(BF16)
        cwt, cbs = ffn_conv_w[i], ffn_conv_b[i][None, :]
        g1, b1 = ln_g[i, 1][None, :], ln_b[i, 1][None, :]
        xp, st_p = _ffn_seq(xp, sh_p, sc_p, gt_p, jnp.zeros((b, CONV_WIDTH - 1, ff), F32), wg, wu, cwt, cbs, wd,
                            g1, b1, alpha, tm, tf)
        prev = state_ffn_conv[i]
        xs2, g_new = _ffn_step(xs[0], sh_s[0], sc_s[0], gt_s[0], prev[:, 0], prev[:, 1], wg, wu, cwt, cbs, wd,
                               g1, b1, alpha, tf)
        xs = xs2[None]
        conv_p.append(st_p)
        conv_s.append(jnp.stack([prev[:, 1], g_new], 1))
    return (xp, xs.reshape(db, 1, d), jnp.stack(moba_p), jnp.stack(moba_s), jnp.stack(swa_p), jnp.stack(swa_s),
            jnp.stack(nsa_p), jnp.stack(nsa_s), jnp.stack(nsaw_p), jnp.stack(nsaw_s),
            jnp.stack(conv_p), jnp.stack(conv_s))
```

```python
import functools

import jax
import jax.numpy as jnp
from jax import lax
from jax.experimental import pallas as pl
from jax.experimental.pallas import tpu as pltpu

F32 = jnp.float32
BF16 = jnp.bfloat16

HEAD_DIM = 64
N_HEADS = 16
Q_WIDTH = N_HEADS * HEAD_DIM
ROPE_THETA = 10000.0
N_MIXERS = 3
PAGE_SIZE = 128
MOBA_KV_HEADS = 4
MOBA_BLOCK = 256
MOBA_TOPK = 3
SWA_KV_HEADS = 2
SWA_WINDOW = 128
NSA_KV_HEADS = 2
NSA_BLOCK = 64
NSA_TOPN = 16
NSA_WINDOW = 512
NSA_CMP_HIDDEN = 128
CONV_WIDTH = 3
LN_EPS = 1e-5
ATTN_SCALE = HEAD_DIM ** -0.5
LOG2E = 1.4426950408889634

LANES = 128
SUBLANES = 8
TQ = 256
MASKED = -1e30
GATE_FLOOR = -3e38
FFN_HALO = 16
COMPRESS_GROUP = 8
TILE_PITCH = HEAD_DIM + SUBLANES
VMEM_LIMIT = 56 * 1024 * 1024


def _cparams(n_axes):
    return pltpu.CompilerParams(dimension_semantics=("arbitrary",) * n_axes,
                                vmem_limit_bytes=VMEM_LIMIT)


def _dot(a, b):
    return jnp.dot(a, b, preferred_element_type=F32)


def _dot_nt(a, b, precision=None):
    return lax.dot_general(a, b, (((1,), (1,)), ((), ())), precision=precision,
                           preferred_element_type=F32)


def _iota(shape, axis):
    return lax.broadcasted_iota(jnp.int32, shape, axis)


def _adaln_kernel(c_ref, w_ref, b_ref, o_ref):
    c = c_ref[...]
    a = c * (1.0 / (1.0 + jnp.exp(-c)))
    o_ref[0] = _dot(a.astype(BF16), w_ref[0].astype(BF16)) + b_ref[0]


def _adaln_all(c_all, ada_w, ada_b):
    n_sub, d, d3 = ada_w.shape
    rows = c_all.shape[0]
    tn = 1024
    return pl.pallas_call(
        _adaln_kernel,
        grid=(n_sub, d3 // tn),
        in_specs=[pl.BlockSpec((rows, d), lambda l, j: (0, 0)),
                  pl.BlockSpec((1, d, tn), lambda l, j: (l, 0, j)),
                  pl.BlockSpec((1, 1, tn), lambda l, j: (l, 0, j))],
        out_specs=pl.BlockSpec((1, rows, tn), lambda l, j: (l, 0, j)),
        out_shape=jax.ShapeDtypeStruct((n_sub, rows, d3), F32),
        compiler_params=_cparams(2),
        name="adaln",
    )(c_all, ada_w, ada_b)


def _rope_chunk(y, cos, sin_signed):
    lo = (_iota(y.shape, 1) % HEAD_DIM) < (HEAD_DIM // 2)
    swapped = jnp.where(lo, pltpu.roll(y, LANES - HEAD_DIM // 2, 1), pltpu.roll(y, HEAD_DIM // 2, 1))
    return y * cos + swapped * sin_signed


def _inproj_kernel(x_ref, sh_ref, sc_ref, w_ref, cos_ref, sin_ref, *rest, outs, has_t):
    h = (x_ref[0] * (1.0 + sc_ref[0]) + sh_ref[0]).astype(BF16)
    y = _dot(h, w_ref[...])
    cos = cos_ref[...]
    sin = sin_ref[...]
    out_refs = rest
    if has_t:
        wt_ref, out_refs, ot_ref = rest[0], rest[1:-1], rest[-1]
        ot_ref[0] = _dot_nt(wt_ref[...], h)
    for o_ref, segs in zip(out_refs, outs):
        dst = 0
        for (start, width, op) in segs:
            for c in range(0, width, LANES):
                blk = y[:, start + c:start + c + LANES]
                if op == "rope":
                    blk = _rope_chunk(blk, cos, sin)
                elif op == "sigmoid":
                    blk = 1.0 / (1.0 + jnp.exp(-blk))
                o_ref[0, :, dst + c:dst + c + LANES] = blk
            dst += width


def _inproj(x, shift, scale, w, cos, sin, outs, tm, w_t=None):
    g, r, d = x.shape
    n = w.shape[1]
    rm = shift.shape[1]
    mod_spec = (pl.BlockSpec((1, 1, d), lambda a, b: (a, 0, 0)) if rm == 1
                else pl.BlockSpec((1, tm, d), lambda a, b: (a, b, 0)))
    widths = [sum(s[1] for s in segs) for segs in outs]
    in_specs = [pl.BlockSpec((1, tm, d), lambda a, b: (a, b, 0)), mod_spec, mod_spec,
                pl.BlockSpec((d, n), lambda a, b: (0, 0)),
                pl.BlockSpec((tm, LANES), lambda a, b: (b, 0)),
                pl.BlockSpec((tm, LANES), lambda a, b: (b, 0))]
    out_specs = [pl.BlockSpec((1, tm, wd), lambda a, b: (a, b, 0)) for wd in widths]
    out_shape = [jax.ShapeDtypeStruct((g, r, wd), F32) for wd in widths]
    args = [x, shift, scale, w, cos, sin]
    if w_t is not None:
        nt = w_t.shape[0]
        in_specs.append(pl.BlockSpec((nt, d), lambda a, b: (0, 0)))
        out_specs.append(pl.BlockSpec((1, nt, tm), lambda a, b: (a, 0, b)))
        out_shape.append(jax.ShapeDtypeStruct((g, nt, r), F32))
        args.append(w_t)
    return pl.pallas_call(
        functools.partial(_inproj_kernel, outs=outs, has_t=w_t is not None),
        grid=(g, r // tm),
        in_specs=in_specs,
        out_specs=out_specs,
        out_shape=out_shape,
        compiler_params=_cparams(2),
        name="inproj",
    )(*args)


def _residual_ln(x, y, gate, g, b, alpha):
    z = alpha * x + (1.0 + gate) * y
    mu = jnp.mean(z, -1, keepdims=True)
    zc = z - mu
    var = jnp.mean(zc * zc, -1, keepdims=True)
    return zc * lax.rsqrt(var + LN_EPS) * g + b


def _outproj_kernel(o_ref, w_ref, x_ref, gt_ref, g_ref, b_ref, out_ref, *, alpha):
    y = _dot(o_ref[0].astype(BF16), w_ref[...])
    out_ref[0] = _residual_ln(x_ref[0], y, gt_ref[0], g_ref[...], b_ref[...], alpha)


def _outproj_ln(o, w, x, gate, ln_g, ln_b, alpha, tm):
    g, r, d = x.shape
    k = o.shape[2]
    rm = gate.shape[1]
    mod_spec = (pl.BlockSpec((1, 1, d), lambda a, b: (a, 0, 0)) if rm == 1
                else pl.BlockSpec((1, tm, d), lambda a, b: (a, b, 0)))
    return pl.pallas_call(
        functools.partial(_outproj_kernel, alpha=alpha),
        grid=(g, r // tm),
        in_specs=[pl.BlockSpec((1, tm, k), lambda a, b: (a, b, 0)),
                  pl.BlockSpec((k, d), lambda a, b: (0, 0)),
                  pl.BlockSpec((1, tm, d), lambda a, b: (a, b, 0)), mod_spec,
                  pl.BlockSpec((1, d), lambda a, b: (0, 0)),
                  pl.BlockSpec((1, d), lambda a, b: (0, 0))],
        out_specs=pl.BlockSpec((1, tm, d), lambda a, b: (a, b, 0)),
        out_shape=jax.ShapeDtypeStruct((g, r, d), F32),
        compiler_params=_cparams(2),
        name="outproj_ln",
    )(o, w, x, gate, ln_g, ln_b)


def _silu(x):
    return x * (1.0 / (1.0 + jnp.exp(-x)))


def _ffn_seq_kernel(x_ref, xh_ref, sh_ref, sc_ref, gt_ref, prev_ref, wg_ref, wu_ref, cw_ref, cb_ref,
                    wd_ref, g_ref, b_ref, out_ref, st_ref, *, alpha):
    r = pl.program_id(1)
    x = x_ref[0]
    tm = x.shape[0]
    he = (jnp.concatenate([xh_ref[0], x], axis=0) * (1.0 + sc_ref[0]) + sh_ref[0]).astype(BF16)
    h = he[FFN_HALO:]
    ge = _dot(he, wg_ref[...])
    gcur = ge[FFN_HALO:]
    first = r == 0
    prev = prev_ref[0]
    pm1 = jnp.where(first, prev[1:2], ge[FFN_HALO - 1:FFN_HALO])
    pm2 = jnp.where(first, prev[0:1], ge[FFN_HALO - 2:FFN_HALO - 1])
    row = _iota(gcur.shape, 0)
    g1 = jnp.where(row == 0, pm1, pltpu.roll(gcur, 1, 0))
    g2 = jnp.where(row == 0, pm2, jnp.where(row == 1, pm1, pltpu.roll(gcur, 2, 0)))
    cw = cw_ref[...]
    conv = cb_ref[...] + cw[0:1] * g2 + cw[1:2] * g1 + cw[2:3] * gcur
    act = _silu(conv) * _dot(h, wu_ref[...])
    y = _dot(act.astype(BF16), wd_ref[...])
    st_ref[0, 0] = gcur[tm - 2:tm]
    out_ref[0] = _residual_ln(x, y, gt_ref[0], g_ref[...], b_ref[...], alpha)


def _ffn_seq(x, shift, scale, gate, prev, wg, wu, cw, cb, wd, ln_g, ln_b, alpha, tm):
    g, r, d = x.shape
    ff = wg.shape[1]
    n_r = r // tm
    hb = tm // FFN_HALO
    mod = pl.BlockSpec((1, 1, d), lambda a, b: (a, 0, 0))
    const = lambda shape: pl.BlockSpec(shape, lambda a, b: (0, 0), pipeline_mode=pl.Buffered(1))
    out, st = pl.pallas_call(
        functools.partial(_ffn_seq_kernel, alpha=alpha),
        grid=(g, n_r),
        in_specs=[pl.BlockSpec((1, tm, d), lambda a, b: (a, b, 0)),
                  pl.BlockSpec((1, FFN_HALO, d), lambda a, b: (a, jnp.maximum(b * hb - 1, 0), 0)),
                  mod, mod, mod,
                  pl.BlockSpec((1, 2, ff), lambda a, b: (a, 0, 0)),
                  const((d, ff)), const((d, ff)), const((CONV_WIDTH, ff)), const((1, ff)), const((ff, d)),
                  const((1, d)), const((1, d))],
        out_specs=[pl.BlockSpec((1, tm, d), lambda a, b: (a, b, 0)),
                   pl.BlockSpec((1, 1, 2, ff), lambda a, b: (a, b, 0, 0))],
        out_shape=[jax.ShapeDtypeStruct((g, r, d), F32),
                   jax.ShapeDtypeStruct((g, n_r, 2, ff), F32)],
        compiler_params=_cparams(2),
        name="ffn_seq",
    )(x, x, shift, scale, gate, prev, wg, wu, cw, cb, wd, ln_g, ln_b)
    return out, st[:, n_r - 1]


def _ffn_step_kernel(x_ref, sh_ref, sc_ref, gt_ref, p0_ref, p1_ref, wg_ref, wu_ref, cw_ref, cb_ref,
                     wd_ref, g_ref, b_ref, out_ref, st_ref, acc_ref, *, alpha, n_f):
    f = pl.program_id(0)
    x = x_ref[...]
    h = (x * (1.0 + sc_ref[...]) + sh_ref[...]).astype(BF16)
    gcur = _dot(h, wg_ref[...])
    cw = cw_ref[...]
    conv = cb_ref[...] + cw[0:1] * p0_ref[...] + cw[1:2] * p1_ref[...] + cw[2:3] * gcur
    act = _silu(conv) * _dot(h, wu_ref[...])
    part = _dot(act.astype(BF16), wd_ref[...])
    st_ref[...] = gcur

    @pl.when(f == 0)
    def _():
        acc_ref[...] = part

    @pl.when(f != 0)
    def _():
        acc_ref[...] += part

    @pl.when(f == n_f - 1)
    def _():
        out_ref[...] = _residual_ln(x, acc_ref[...], gt_ref[...], g_ref[...], b_ref[...], alpha)


def _ffn_step(x, shift, scale, gate, p0, p1, wg, wu, cw, cb, wd, ln_g, ln_b, alpha, tf):
    r, d = x.shape
    ff = wg.shape[1]
    n_f = ff // tf
    full = pl.BlockSpec((r, d), lambda c: (0, 0))
    return pl.pallas_call(
        functools.partial(_ffn_step_kernel, alpha=alpha, n_f=n_f),
        grid=(n_f,),
        in_specs=[full, full, full, full,
                  pl.BlockSpec((r, tf), lambda c: (0, c)),
                  pl.BlockSpec((r, tf), lambda c: (0, c)),
                  pl.BlockSpec((d, tf), lambda c: (0, c)),
                  pl.BlockSpec((d, tf), lambda c: (0, c)),
                  pl.BlockSpec((CONV_WIDTH, tf), lambda c: (0, c)),
                  pl.BlockSpec((1, tf), lambda c: (0, c)),
                  pl.BlockSpec((tf, d), lambda c: (c, 0)),
                  pl.BlockSpec((1, d), lambda c: (0, 0)),
                  pl.BlockSpec((1, d), lambda c: (0, 0))],
        out_specs=[full, pl.BlockSpec((r, tf), lambda c: (0, c))],
        out_shape=[jax.ShapeDtypeStruct((r, d), F32), jax.ShapeDtypeStruct((r, ff), F32)],
        scratch_shapes=[pltpu.VMEM((r, d), F32)],
        compiler_params=_cparams(1),
        name="ffn_step",
    )(x, shift, scale, gate, p0, p1, wg, wu, cw, cb, wd, ln_g, ln_b)


def _head_slot(x, h):
    pair = x[:, (h // 2) * LANES:(h // 2 + 1) * LANES]
    return pltpu.roll(pair, HEAD_DIM, 1) if h % 2 else pair


def _slot(x, j, fill=0.0):
    lane = _iota((x.shape[0], LANES), 1)
    return jnp.where(lane < HEAD_DIM, _head_slot(x, j), fill)


def _flash_t(qg, k_ref, lane0, vt_ref, row0, i, lo, window, masked_past):
    rows = qg.shape[0]
    key_io = _iota((TQ, rows), 0)
    q_io = _iota((TQ, rows), 1) % TQ

    def scores(n, masked):
        start = pl.multiple_of(n * TQ, TQ)
        s = _dot_nt(k_ref[pl.ds(start, TQ), lane0:lane0 + LANES], qg)
        if masked:
            dist = (i - n) * TQ + q_io - key_io
            ok = dist >= 0
            if window is not None:
                ok = ok & (dist <= window)
            s = jnp.where(ok, s, MASKED)
        return s

    def values(n, p):
        return _dot(vt_ref[n, row0:row0 + HEAD_DIM, :], p.astype(BF16))

    s = scores(i, True)
    m = jnp.max(s, axis=0, keepdims=True)
    p = jnp.exp2(s - m)
    l = jnp.sum(p, axis=0, keepdims=True)
    acc = values(i, p)

    def body(t, carry):
        m, l, acc = carry
        n = i - 1 - t
        s = scores(n, masked_past)
        m_new = jnp.maximum(m, jnp.max(s, axis=0, keepdims=True))
        a = jnp.exp2(m - m_new)
        p = jnp.exp2(s - m_new)
        l = a * l + jnp.sum(p, axis=0, keepdims=True)
        acc = a * acc + values(n, p)
        return m_new, l, acc

    m, l, acc = lax.fori_loop(0, i - lo, body, (m, l, acc))
    return acc, m, l


def _store_heads(o_ref, out_t, head0, n_heads):
    for gp in range(n_heads // 2):
        pair = jnp.concatenate([out_t[:, (2 * gp) * TQ:(2 * gp + 1) * TQ],
                                out_t[:, (2 * gp + 1) * TQ:(2 * gp + 2) * TQ]], axis=0)
        c0 = ((head0 + 2 * gp) // 2) * LANES
        o_ref[0, :, c0:c0 + LANES] = pair.T


def _stack_queries(q, heads, bias=None):
    lane = _iota((TQ, LANES), 1)
    out = []
    for k, h in enumerate(heads):
        qh = jnp.where(lane < HEAD_DIM, _head_slot(q, h), 0.0) * (ATTN_SCALE * LOG2E)
        out.append((qh if bias is None else qh + bias[k]).astype(BF16))
    return jnp.concatenate(out, axis=0)


def _bias_lanes(bias_t):
    n = bias_t.shape[0]
    parts = [jnp.zeros((HEAD_DIM, TQ), F32), bias_t]
    if n < HEAD_DIM:
        parts.append(jnp.zeros((HEAD_DIM - n, TQ), F32))
    return jnp.concatenate(parts, axis=0).T


def _window_chunks(window):
    return -(-window // TQ)


def _moba_prompt_kernel(q_ref, kv_ref, vt_ref, o_ref, kslot_ref, vt_sc, kmean_ref, *, n_blocks):
    i = pl.program_id(1)
    kvh_n = MOBA_KV_HEADS
    grp = N_HEADS // kvh_n
    kw = kvh_n * HEAD_DIM

    @pl.when(i == 0)
    def _():
        for n in range(n_blocks):
            vt_sc[n] = vt_ref[0, :, n * TQ:(n + 1) * TQ].astype(BF16)
        blk_rows = _iota(kmean_ref.shape, 0)

        def prep(n, kmean):
            start = pl.multiple_of(n * TQ, TQ)
            kt = kv_ref[0, pl.ds(start, TQ), 0:kw]
            block_lane = (_iota((TQ, LANES), 1) - HEAD_DIM == n).astype(F32)
            means = []
            for j in range(kvh_n):
                kslot_ref[pl.ds(start, TQ), j * LANES:(j + 1) * LANES] = _slot(kt, j, block_lane).astype(BF16)
                means.append(jnp.mean(_slot(kt, j), axis=0, keepdims=True))
            return jnp.where(blk_rows == n, jnp.concatenate(means, axis=1), kmean)

        kmean_ref[...] = lax.fori_loop(0, n_blocks, prep, jnp.zeros(kmean_ref.shape, F32))

    q = q_ref[0]
    lane = _iota((TQ, LANES), 1)
    blk_io = _iota((kmean_ref.shape[0], TQ), 0)
    eligible = blk_io < i
    for j in range(kvh_n):
        kmj = kmean_ref[:, j * LANES:(j + 1) * LANES]
        bias = []
        for gi in range(grp):
            qh = jnp.where(lane < HEAD_DIM, _head_slot(q, j * grp + gi), 0.0)
            gate = _dot_nt(kmj, qh, precision=lax.Precision.HIGHEST)
            gate = jnp.where(eligible, gate, GATE_FLOOR)
            rank = jnp.zeros(gate.shape, jnp.int32)
            for m_idx in range(n_blocks):
                row = gate[m_idx:m_idx + 1, :]
                beats = (row > gate) | ((row == gate) & (m_idx < blk_io))
                rank = rank + beats.astype(jnp.int32)
            sel = (eligible & (rank < MOBA_TOPK)) | (blk_io == i)
            bias.append(_bias_lanes(jnp.where(sel, 0.0, MASKED)))
        qg = _stack_queries(q, range(j * grp, (j + 1) * grp), bias)
        acc, _, l = _flash_t(qg, kslot_ref, j * LANES, vt_sc, j * HEAD_DIM, i, 0, None, False)
        _store_heads(o_ref, acc / l, j * grp, grp)


def _moba_prompt(q, kv, v_t):
    b, s, _ = q.shape
    n_blocks = s // MOBA_BLOCK
    assert MOBA_BLOCK == TQ and s % TQ == 0 and n_blocks <= HEAD_DIM
    kw = MOBA_KV_HEADS * HEAD_DIM
    nbp = -(-n_blocks // SUBLANES) * SUBLANES
    return pl.pallas_call(
        functools.partial(_moba_prompt_kernel, n_blocks=n_blocks),
        grid=(b, s // TQ),
        in_specs=[pl.BlockSpec((1, TQ, Q_WIDTH), lambda a, i: (a, i, 0)),
                  pl.BlockSpec((1, s, 2 * kw), lambda a, i: (a, 0, 0)),
                  pl.BlockSpec((1, kw, s), lambda a, i: (a, 0, 0))],
        out_specs=pl.BlockSpec((1, TQ, Q_WIDTH), lambda a, i: (a, i, 0)),
        out_shape=jax.ShapeDtypeStruct((b, s, Q_WIDTH), F32),
        scratch_shapes=[pltpu.VMEM((s, MOBA_KV_HEADS * LANES), BF16),
                        pltpu.VMEM((n_blocks, kw, TQ), BF16),
                        pltpu.VMEM((nbp, MOBA_KV_HEADS * LANES), F32)],
        compiler_params=_cparams(2),
        name="moba_prompt",
    )(q, kv, v_t)


def _swa_prompt_kernel(q_ref, kv_ref, vt_ref, sink_ref, o_ref, kslot_ref, vt_sc, *, n_chunks):
    i = pl.program_id(1)
    kvh_n = SWA_KV_HEADS
    grp = N_HEADS // kvh_n
    kw = kvh_n * HEAD_DIM

    @pl.when(i == 0)
    def _():
        for n in range(n_chunks):
            vt_sc[n] = vt_ref[0, :, n * TQ:(n + 1) * TQ].astype(BF16)

        def prep(n, carry):
            start = pl.multiple_of(n * TQ, TQ)
            kt = kv_ref[0, pl.ds(start, TQ), 0:kw]
            for j in range(kvh_n):
                kslot_ref[pl.ds(start, TQ), j * LANES:(j + 1) * LANES] = _slot(kt, j).astype(BF16)
            return carry

        lax.fori_loop(0, n_chunks, prep, 0)

    q = q_ref[0]
    lo = jnp.maximum(i - _window_chunks(SWA_WINDOW), 0)
    for j in range(kvh_n):
        heads = range(j * grp, (j + 1) * grp)
        acc, m, l = _flash_t(_stack_queries(q, heads), kslot_ref, j * LANES, vt_sc, j * HEAD_DIM, i, lo,
                             SWA_WINDOW, True)
        sink = jnp.concatenate([jnp.broadcast_to(sink_ref[0:1, h:h + 1], (1, TQ)) for h in heads], axis=1) * LOG2E
        m_f = jnp.maximum(m, sink)
        a = jnp.exp2(m - m_f)
        den = l * a + jnp.exp2(sink - m_f)
        _store_heads(o_ref, acc * (a / den), j * grp, grp)


def _swa_prompt(q, kv, v_t, sink):
    b, s, _ = q.shape
    assert s % TQ == 0
    kw = SWA_KV_HEADS * HEAD_DIM
    return pl.pallas_call(
        functools.partial(_swa_prompt_kernel, n_chunks=s // TQ),
        grid=(b, s // TQ),
        in_specs=[pl.BlockSpec((1, TQ, Q_WIDTH), lambda a, i: (a, i, 0)),
                  pl.BlockSpec((1, s, 2 * kw), lambda a, i: (a, 0, 0)),
                  pl.BlockSpec((1, kw, s), lambda a, i: (a, 0, 0)),
                  pl.BlockSpec((1, LANES), lambda a, i: (0, 0))],
        out_specs=pl.BlockSpec((1, TQ, Q_WIDTH), lambda a, i: (a, i, 0)),
        out_shape=jax.ShapeDtypeStruct((b, s, Q_WIDTH), F32),
        scratch_shapes=[pltpu.VMEM((s, SWA_KV_HEADS * LANES), BF16),
                        pltpu.VMEM((s // TQ, kw, TQ), BF16)],
        compiler_params=_cparams(2),
        name="swa_prompt",
    )(q, kv, v_t, sink)


def _gelu_tanh(x):
    return x * (0.5 * (1.0 + jnp.tanh(0.7978845608028654 * (x + 0.044715 * (x * x * x)))))


def _compress_rows(xk_ref, xv_ref, n_blk, pe_ref, w1k_ref, w1v_ref, w2_ref):
    half = NSA_KV_HEADS * HEAD_DIM

    def body(p, carry):
        ak, av = carry
        pe = pe_ref[pl.ds(p, 1), :]
        xk = (xk_ref[pl.ds(p, n_blk, stride=NSA_BLOCK), :] + pe[:, 0:half]).astype(BF16)
        xv = (xv_ref[pl.ds(p, n_blk, stride=NSA_BLOCK), :] + pe[:, half:2 * half]).astype(BF16)
        ak = ak + _dot(xk, w1k_ref[p])
        av = av + _dot(xv, w1v_ref[p])
        return ak, av

    zero = jnp.zeros((n_blk, NSA_KV_HEADS * NSA_CMP_HIDDEN), F32)
    ak, av = lax.fori_loop(0, NSA_BLOCK, body, (zero, zero))
    hid = jnp.concatenate([_gelu_tanh(ak), _gelu_tanh(av)], axis=1).astype(BF16)
    return _dot(hid, w2_ref[...])


def _nsa_compress_seq_kernel(xk_ref, xv_ref, pe_ref, w1k_ref, w1v_ref, w2_ref, o_ref, *, n_blk):
    o_ref[...] = _compress_rows(xk_ref, xv_ref, n_blk, pe_ref, w1k_ref, w1v_ref, w2_ref)


def _nsa_compress_seq(rows2d, pe2, w1k, w1v, w2):
    t = rows2d.shape[0]
    n_blk = min(64, t // NSA_BLOCK)
    tr = n_blk * NSA_BLOCK
    half = 2 * NSA_KV_HEADS * HEAD_DIM
    full = lambda shape: pl.BlockSpec(shape, lambda j: (0,) * len(shape))
    return pl.pallas_call(
        functools.partial(_nsa_compress_seq_kernel, n_blk=n_blk),
        grid=(t // tr,),
        in_specs=[pl.BlockSpec((tr, LANES), lambda j: (j, 0)),
                  pl.BlockSpec((tr, LANES), lambda j: (j, 1)),
                  full(pe2.shape), full(w1k.shape), full(w1v.shape), full(w2.shape)],
        out_specs=pl.BlockSpec((n_blk, half), lambda j: (j, 0)),
        out_shape=jax.ShapeDtypeStruct((t // NSA_BLOCK, half), F32),
        compiler_params=_cparams(1),
        name="nsa_compress_seq",
    )(rows2d, rows2d, pe2, w1k, w1v, w2)


def _nsa_compress_pages_kernel(pt_ref, *refs, n_pages):
    page_refs = refs[:n_pages]
    pek_ref, pev_ref, w1k_ref, w1v_ref, w2k_ref, w2v_ref, o_ref, xk_sc, xv_sc = refs[n_pages:]
    for j in range(n_pages):
        for u in range(NSA_KV_HEADS):
            r0 = (j * NSA_KV_HEADS + u) * TILE_PITCH
            xk_sc[r0:r0 + HEAD_DIM, :] = page_refs[j][0, u]
            xv_sc[r0:r0 + HEAD_DIM, :] = page_refs[j][1, u]
    n_rows = n_pages * NSA_KV_HEADS

    def hidden(x_sc, pe_ref, w1_ref):
        acc = None
        for d0 in range(0, HEAD_DIM, COMPRESS_GROUP):
            x = jnp.concatenate(
                [(x_sc[pl.ds(d, n_rows, stride=TILE_PITCH), :] + pe_ref[d:d + 1, :]).astype(BF16)
                 for d in range(d0, d0 + COMPRESS_GROUP)], axis=1)
            part = _dot(x, w1_ref[d0 * PAGE_SIZE:(d0 + COMPRESS_GROUP) * PAGE_SIZE, :])
            acc = part if acc is None else acc + part
        return _gelu_tanh(acc).astype(BF16)

    kc = _dot(hidden(xk_sc, pek_ref, w1k_ref), w2k_ref[...])
    vc = _dot(hidden(xv_sc, pev_ref, w1v_ref), w2v_ref[...])
    o_ref[0] = jnp.concatenate([kc, vc], axis=1)


def _nsa_page_spec(j, comp_block):
    return pl.BlockSpec((None, 2, NSA_KV_HEADS, HEAD_DIM, PAGE_SIZE), lambda b, pt: (pt[b, j], comp_block, 0, 0, 0))


def _nsa_compress_pages(pt, cache_t, pek_t, pev_t, w1k_t, w1v_t, w2k, w2v):
    db, n_pages = pt.shape
    n_rows = n_pages * NSA_KV_HEADS
    full = lambda shape: pl.BlockSpec(shape, lambda b, pt: (0,) * len(shape))
    consts = (pek_t, pev_t, w1k_t, w1v_t, w2k, w2v)
    grid_spec = pltpu.PrefetchScalarGridSpec(
        num_scalar_prefetch=1,
        grid=(db,),
        in_specs=[_nsa_page_spec(j, 0) for j in range(n_pages)] + [full(c.shape) for c in consts],
        out_specs=pl.BlockSpec((1, n_rows, 2 * LANES), lambda b, pt: (b, 0, 0)),
        scratch_shapes=[pltpu.VMEM((n_rows * TILE_PITCH, PAGE_SIZE), F32),
                        pltpu.VMEM((n_rows * TILE_PITCH, PAGE_SIZE), F32)],
    )
    return pl.pallas_call(
        functools.partial(_nsa_compress_pages_kernel, n_pages=n_pages),
        grid_spec=grid_spec,
        out_shape=jax.ShapeDtypeStruct((db, n_rows, 2 * LANES), F32),
        compiler_params=_cparams(1),
        name="nsa_compress_pages",
    )(pt, *([cache_t] * n_pages), *consts)


def _nsa_prompt_kernel(qraw_ref, qrot_ref, ks_ref, kw_ref, vt_ref, kc_ref, vct_ref, gates_ref, o_ref,
                       kslc_sc, kwin_sc, vt_sc, ocmp_sc, *, n_chunks, n_blocks):
    i = pl.program_id(1)
    kvh_n = NSA_KV_HEADS
    grp = N_HEADS // kvh_n

    @pl.when(i == 0)
    def _():
        for n in range(n_chunks):
            vt_sc[n] = vt_ref[0, :, n * TQ:(n + 1) * TQ].astype(BF16)

        def prep(n, carry):
            start = pl.multiple_of(n * TQ, TQ)
            ks = ks_ref[0, pl.ds(start, TQ), :]
            kwn = kw_ref[0, pl.ds(start, TQ), :]
            blk = (n * TQ + _iota((TQ, LANES), 0)) // NSA_BLOCK
            block_lane = (_iota((TQ, LANES), 1) - HEAD_DIM == blk).astype(F32)
            for j in range(kvh_n):
                kslc_sc[pl.ds(start, TQ), j * LANES:(j + 1) * LANES] = _slot(ks, j, block_lane).astype(BF16)
                kwin_sc[pl.ds(start, TQ), j * LANES:(j + 1) * LANES] = _slot(kwn, j).astype(BF16)
            return carry

        lax.fori_loop(0, n_chunks, prep, 0)

    lane = _iota((TQ, LANES), 1)
    q_raw = qraw_ref[0]
    kcb = kc_ref[0].astype(BF16)
    vctb = vct_ref[0].astype(BF16)
    row_io = _iota((LANES, TQ), 0)
    n_io = row_io % HEAD_DIM
    t = i * TQ + _iota((LANES, TQ), 1)
    avail = ((n_io + 1) * NSA_BLOCK - 1 <= t) & (n_io < n_blocks)
    score = jnp.zeros((LANES, TQ), F32)
    for h in range(N_HEADS):
        qh = (jnp.where(lane < HEAD_DIM, _head_slot(q_raw, h), 0.0) * ATTN_SCALE).astype(BF16)
        ok = avail & ((row_io // HEAD_DIM) == (h // grp))
        s = jnp.where(ok, _dot_nt(kcb, qh), MASKED)
        m = jnp.max(s, axis=0, keepdims=True)
        p = jnp.exp(s - m) * ok.astype(F32)
        p = p / jnp.maximum(jnp.sum(p, axis=0, keepdims=True), 1e-30)
        score = score + p
        ocmp_sc[h * HEAD_DIM:(h + 1) * HEAD_DIM, :] = _dot(vctb, p.astype(BF16))

    own = t // NSA_BLOCK
    forced = ((n_io == 0) | (n_io == own) | (n_io == own - 1)) & (n_io < n_blocks)
    sc = jnp.where(avail, score, -jnp.inf)
    sc = jnp.where(forced, jnp.inf, sc)
    rank = jnp.zeros((LANES, TQ), jnp.int32)
    for m_idx in range(n_blocks):
        row = jnp.where(row_io < HEAD_DIM, sc[m_idx:m_idx + 1, :], sc[HEAD_DIM + m_idx:HEAD_DIM + m_idx + 1, :])
        beats = (row > sc) | ((row == sc) & (m_idx < n_io))
        rank = rank + beats.astype(jnp.int32)
    bias = jnp.where((rank < NSA_TOPN) & (sc > -jnp.inf), 0.0, MASKED)

    q_rot = qrot_ref[0]
    gates_t = gates_ref[0].T
    lo_win = jnp.maximum(i - _window_chunks(NSA_WINDOW), 0)
    for j in range(kvh_n):
        heads = range(j * grp, (j + 1) * grp)
        bias_j = _bias_lanes(bias[j * HEAD_DIM:(j + 1) * HEAD_DIM])
        acc_s, _, l_s = _flash_t(_stack_queries(q_rot, heads, [bias_j] * grp), kslc_sc, j * LANES, vt_sc,
                                 j * HEAD_DIM, i, 0, None, False)
        acc_w, _, l_w = _flash_t(_stack_queries(q_rot, heads), kwin_sc, j * LANES, vt_sc,
                                 (kvh_n + j) * HEAD_DIM, i, lo_win, NSA_WINDOW, True)
        o_s = acc_s / l_s
        o_w = acc_w / l_w
        mixed = []
        for gi in range(grp):
            h = j * grp + gi
            cols = slice(gi * TQ, (gi + 1) * TQ)
            mixed.append(gates_t[3 * h:3 * h + 1] * ocmp_sc[h * HEAD_DIM:(h + 1) * HEAD_DIM, :]
                         + gates_t[3 * h + 1:3 * h + 2] * o_s[:, cols]
                         + gates_t[3 * h + 2:3 * h + 3] * o_w[:, cols])
        _store_heads(o_ref, jnp.concatenate(mixed, axis=1), j * grp, grp)


def _nsa_prompt(q_raw, q_rot, rows, win, v_t, kc, vc_t, gates):
    b, s, _ = q_raw.shape
    kw = NSA_KV_HEADS * HEAD_DIM
    n_chunks = s // TQ
    n_blocks = s // NSA_BLOCK
    assert s % TQ == 0 and n_blocks <= HEAD_DIM and kw == LANES
    grp = N_HEADS // NSA_KV_HEADS
    tile = lambda w: pl.BlockSpec((1, TQ, w), lambda a, i: (a, i, 0))
    per_b = lambda shape, lane_block=0: pl.BlockSpec((1,) + shape, lambda a, i: (a, 0, lane_block))
    return pl.pallas_call(
        functools.partial(_nsa_prompt_kernel, n_chunks=n_chunks, n_blocks=n_blocks),
        grid=(b, n_chunks),
        in_specs=[tile(Q_WIDTH), tile(Q_WIDTH),
                  per_b((s, kw), 2), per_b((s, kw), 0), per_b((2 * kw, s)),
                  per_b((LANES, LANES)), per_b((HEAD_DIM, LANES)), tile(LANES)],
        out_specs=tile(Q_WIDTH),
        out_shape=jax.ShapeDtypeStruct((b, s, Q_WIDTH), F32),
        scratch_shapes=[pltpu.VMEM((s, NSA_KV_HEADS * LANES), BF16),
                        pltpu.VMEM((s, NSA_KV_HEADS * LANES), BF16),
                        pltpu.VMEM((n_chunks, 2 * kw, TQ), BF16),
                        pltpu.VMEM((Q_WIDTH, TQ), F32)],
        compiler_params=_cparams(2),
        name="nsa_prompt",
    )(q_raw, q_rot, rows, win, v_t, kc, vc_t, gates)


def _fold_heads(o, kvh_n):
    grp = N_HEADS // kvh_n
    own = (_iota(o.shape, 1) // HEAD_DIM) == (_iota(o.shape, 0) // grp)
    t = jnp.where(own, o, 0.0)
    width = kvh_n * HEAD_DIM
    while width > HEAD_DIM:
        width //= 2
        t = t + pltpu.roll(t, width, 1)
    return t


def _rank_desc(vals, n_cand, n_io):
    rank = jnp.zeros(vals.shape, jnp.int32)
    for m_idx in range(n_cand):
        col = vals[:, m_idx:m_idx + 1]
        beats = (col > vals) | ((col == vals) & (m_idx < n_io))
        rank = rank + beats.astype(jnp.int32)
    return rank


def _moba_decode_kernel(pt_ref, *refs, n_pages):
    page_refs = refs[:n_pages]
    q_ref, new_ref, o_ref, s_sc, v_sc = refs[n_pages:]
    kw = MOBA_KV_HEADS * HEAD_DIM
    pages_per_block = MOBA_BLOCK // PAGE_SIZE
    n_blk = n_pages // pages_per_block
    qs = q_ref[0] * ATTN_SCALE
    qb = qs.astype(BF16)
    lane = _iota((N_HEADS, LANES), 1)
    gate = jnp.zeros((N_HEADS, LANES), F32)
    gsum = None
    for j in range(n_pages):
        s = _dot(qb, page_refs[j][0].reshape(kw, PAGE_SIZE).astype(BF16))
        s_sc[:, j * PAGE_SIZE:(j + 1) * PAGE_SIZE] = s
        v_sc[j] = page_refs[j][1].reshape(kw, PAGE_SIZE).astype(BF16)
        part = jnp.sum(s, axis=1, keepdims=True)
        gsum = part if j % pages_per_block == 0 else gsum + part
        if j % pages_per_block == pages_per_block - 1:
            gate = jnp.where(lane == j // pages_per_block, gsum, gate)
    eligible = lane < n_blk
    gate = jnp.where(eligible, gate, GATE_FLOOR)
    sel = eligible & (_rank_desc(gate, n_blk, lane) < MOBA_TOPK)
    bias = jnp.where(sel, 0.0, MASKED)
    new = new_ref[0]
    s_new = jnp.sum(qs * new[:, 0:kw], axis=1, keepdims=True)
    m = s_new
    for n in range(n_blk):
        cols = slice(n * MOBA_BLOCK, (n + 1) * MOBA_BLOCK)
        sb = s_sc[:, cols] + bias[:, n:n + 1]
        s_sc[:, cols] = sb
        m = jnp.maximum(m, jnp.max(sb, axis=1, keepdims=True))
    den = jnp.exp(s_new - m)
    o = den * new[:, kw:2 * kw]
    for j in range(n_pages):
        p = jnp.exp(s_sc[:, j * PAGE_SIZE:(j + 1) * PAGE_SIZE] - m)
        den = den + jnp.sum(p, axis=1, keepdims=True)
        o = o + _dot_nt(p.astype(BF16), v_sc[j])
    o_ref[0] = _fold_heads(o / den, MOBA_KV_HEADS)


def _moba_decode(pt, cache_t, qbd, new):
    db, n_pages = pt.shape
    kw = MOBA_KV_HEADS * HEAD_DIM
    assert (n_pages * PAGE_SIZE) % MOBA_BLOCK == 0 and n_pages * PAGE_SIZE // MOBA_BLOCK <= LANES
    page = lambda j: pl.BlockSpec((None, 2, MOBA_KV_HEADS, HEAD_DIM, PAGE_SIZE), lambda b, pt: (pt[b, j], 0, 0, 0, 0))
    grid_spec = pltpu.PrefetchScalarGridSpec(
        num_scalar_prefetch=1,
        grid=(db,),
        in_specs=[page(j) for j in range(n_pages)]
        + [pl.BlockSpec((1, N_HEADS, kw), lambda b, pt: (b, 0, 0)),
           pl.BlockSpec((1, 1, 2 * kw), lambda b, pt: (b, 0, 0))],
        out_specs=pl.BlockSpec((1, N_HEADS, kw), lambda b, pt: (b, 0, 0)),
        scratch_shapes=[pltpu.VMEM((N_HEADS, n_pages * PAGE_SIZE), F32),
                        pltpu.VMEM((n_pages, kw, PAGE_SIZE), BF16)],
    )
    return pl.pallas_call(
        functools.partial(_moba_decode_kernel, n_pages=n_pages),
        grid_spec=grid_spec,
        out_shape=jax.ShapeDtypeStruct((db, N_HEADS, kw), F32),
        compiler_params=_cparams(1),
        name="moba_decode",
    )(pt, *([cache_t] * n_pages), qbd, new)


def _swa_decode_kernel(buf_ref, new_ref, q_ref, sink_ref, o_ref):
    kw = SWA_KV_HEADS * HEAD_DIM
    wb = buf_ref.shape[-1]
    qs = q_ref[0] * ATTN_SCALE
    new = new_ref[0]
    s = _dot(qs.astype(BF16), buf_ref[0, 0].reshape(kw, wb).astype(BF16))
    s_new = jnp.sum(qs * new[:, 0:kw], axis=1, keepdims=True)
    sink = sink_ref[:, 0:1]
    m = jnp.maximum(jnp.maximum(jnp.max(s, axis=1, keepdims=True), s_new), sink)
    p = jnp.exp(s - m)
    p_new = jnp.exp(s_new - m)
    den = jnp.sum(p, axis=1, keepdims=True) + p_new + jnp.exp(sink - m)
    o = (_dot_nt(p.astype(BF16), buf_ref[0, 1].reshape(kw, wb).astype(BF16)) + p_new * new[:, kw:2 * kw]) / den
    o_ref[0] = _fold_heads(o, SWA_KV_HEADS)


def _swa_decode(buf_t, new, qbd, sink):
    db = buf_t.shape[0]
    wb = buf_t.shape[-1]
    kw = SWA_KV_HEADS * HEAD_DIM
    return pl.pallas_call(
        _swa_decode_kernel,
        grid=(db,),
        in_specs=[pl.BlockSpec((1, 2, SWA_KV_HEADS, HEAD_DIM, wb), lambda b: (b, 0, 0, 0, 0)),
                  pl.BlockSpec((1, 1, 2 * kw), lambda b: (b, 0, 0)),
                  pl.BlockSpec((1, N_HEADS, kw), lambda b: (b, 0, 0)),
                  pl.BlockSpec((N_HEADS, LANES), lambda b: (0, 0))],
        out_specs=pl.BlockSpec((1, N_HEADS, kw), lambda b: (b, 0, 0)),
        out_shape=jax.ShapeDtypeStruct((db, N_HEADS, kw), F32),
        compiler_params=_cparams(1),
        name="swa_decode",
    )(buf_t, new, qbd, sink)


def _nsa_decode_kernel(pt_ref, *refs, n_pages):
    page_refs = refs[:n_pages]
    (kcvc_ref, win_ref, qraw_ref, qrot_ref, newslc_ref, newwin_ref, gates_ref, o_ref, s_sc, v_sc) = refs[n_pages:]
    kw = NSA_KV_HEADS * HEAD_DIM
    grp = N_HEADS // NSA_KV_HEADS
    blocks_per_page = PAGE_SIZE // NSA_BLOCK
    n_blk = n_pages * blocks_per_page
    wb = win_ref.shape[-1]
    q_raw = (qraw_ref[0] * ATTN_SCALE).astype(BF16)
    q_rot = qrot_ref[0] * ATTN_SCALE
    q_rot_b = q_rot.astype(BF16)

    kcvc = kcvc_ref[0]
    s_c = _dot_nt(q_raw, kcvc[:, 0:kw].astype(BF16))
    p_c = jnp.exp(s_c - jnp.max(s_c, axis=1, keepdims=True))
    p_c = p_c / jnp.sum(p_c, axis=1, keepdims=True)
    o_cmp = _dot(p_c.astype(BF16), kcvc[:, kw:2 * kw].astype(BF16))

    head = _iota(p_c.shape, 0)
    sc0 = jnp.sum(jnp.where(head < grp, p_c, 0.0), axis=0, keepdims=True)
    sc1 = jnp.sum(jnp.where(head >= grp, p_c, 0.0), axis=0, keepdims=True)
    r8 = _iota((SUBLANES, n_blk), 0)
    n_io = _iota((SUBLANES, n_blk), 1)
    sc = jnp.where(r8 == 0, sc0, jnp.where(r8 == 1, sc1, -jnp.inf))
    forced = (n_io == 0) | (n_io == n_blk - 1)
    sc = jnp.where(forced, jnp.inf, sc)
    sel = forced | (_rank_desc(sc, n_blk, n_io) < NSA_TOPN - 1)
    bias2 = jnp.where(sel, 0.0, MASKED)
    bias = jnp.where(head < grp, bias2[0:1], bias2[1:2])

    new_slc = newslc_ref[0]
    s_new = jnp.sum(q_rot * new_slc[:, 0:kw], axis=1, keepdims=True)
    m = s_new
    lane = _iota((N_HEADS, PAGE_SIZE), 1)
    for j in range(n_pages):
        sj = _dot(q_rot_b, page_refs[j][0].reshape(kw, PAGE_SIZE).astype(BF16))
        bj = bias[:, blocks_per_page * j:blocks_per_page * j + 1]
        for u in range(1, blocks_per_page):
            bj = jnp.where(lane < u * NSA_BLOCK, bj, bias[:, blocks_per_page * j + u:blocks_per_page * j + u + 1])
        sj = sj + bj
        s_sc[:, j * PAGE_SIZE:(j + 1) * PAGE_SIZE] = sj
        v_sc[j] = page_refs[j][1].reshape(kw, PAGE_SIZE).astype(BF16)
        m = jnp.maximum(m, jnp.max(sj, axis=1, keepdims=True))
    den = jnp.exp(s_new - m)
    o_slc = den * new_slc[:, kw:2 * kw]
    for j in range(n_pages):
        p = jnp.exp(s_sc[:, j * PAGE_SIZE:(j + 1) * PAGE_SIZE] - m)
        den = den + jnp.sum(p, axis=1, keepdims=True)
        o_slc = o_slc + _dot_nt(p.astype(BF16), v_sc[j])
    o_slc = o_slc / den

    new_win = newwin_ref[0]
    s_w = _dot(q_rot_b, win_ref[0, 0].reshape(kw, wb).astype(BF16))
    s_wn = jnp.sum(q_rot * new_win[:, 0:kw], axis=1, keepdims=True)
    m_w = jnp.maximum(jnp.max(s_w, axis=1, keepdims=True), s_wn)
    p_w = jnp.exp(s_w - m_w)
    p_wn = jnp.exp(s_wn - m_w)
    den_w = jnp.sum(p_w, axis=1, keepdims=True) + p_wn
    o_win = (_dot_nt(p_w.astype(BF16), win_ref[0, 1].reshape(kw, wb).astype(BF16))
             + p_wn * new_win[:, kw:2 * kw]) / den_w

    g = gates_ref[0]
    o = g[:, 0:1] * o_cmp + g[:, 1:2] * o_slc + g[:, 2:3] * o_win
    o_ref[0] = _fold_heads(o, NSA_KV_HEADS)


def _nsa_decode(pt, cache_t, kcvc, win_t, q_raw_bd, q_rot_bd, new_rows, new_win, gates):
    db, n_pages = pt.shape
    kw = NSA_KV_HEADS * HEAD_DIM
    n_blk = kcvc.shape[1]
    wb = win_t.shape[-1]
    per_b = lambda shape: pl.BlockSpec((1,) + shape, lambda b, pt: (b,) + (0,) * len(shape))
    grid_spec = pltpu.PrefetchScalarGridSpec(
        num_scalar_prefetch=1,
        grid=(db,),
        in_specs=[_nsa_page_spec(j, 1) for j in range(n_pages)]
        + [per_b((n_blk, 2 * kw)), per_b((2, NSA_KV_HEADS, HEAD_DIM, wb)), per_b((N_HEADS, kw)), per_b((N_HEADS, kw)),
           pl.BlockSpec((1, 1, 2 * kw), lambda b, pt: (b, 0, 1)), per_b((1, 2 * kw)), per_b((N_HEADS, LANES))],
        out_specs=per_b((N_HEADS, kw)),
        scratch_shapes=[pltpu.VMEM((N_HEADS, n_pages * PAGE_SIZE), F32),
                        pltpu.VMEM((n_pages, kw, PAGE_SIZE), BF16)],
    )
    return pl.pallas_call(
        functools.partial(_nsa_decode_kernel, n_pages=n_pages),
        grid_spec=grid_spec,
        out_shape=jax.ShapeDtypeStruct((db, N_HEADS, kw), F32),
        compiler_params=_cparams(1),
        name="nsa_decode",
    )(pt, *([cache_t] * n_pages), kcvc, win_t, q_raw_bd, q_rot_bd, new_rows, new_win, gates)


def _rope_tables(pos):
    half = HEAD_DIM // 2
    inv = ROPE_THETA ** (-jnp.arange(half, dtype=F32) / half)
    ang = pos.astype(F32)[:, None] * inv[None, :]
    cos = jnp.cos(ang)
    sin = jnp.sin(ang)
    reps = LANES // HEAD_DIM
    return jnp.tile(cos, (1, 2 * reps)), jnp.tile(jnp.concatenate([-sin, sin], 1), (1, reps))


def _block_diag_heads(q, kvh_n):
    r = q.shape[0]
    grp = N_HEADS // kvh_n
    own = (jnp.arange(N_HEADS)[:, None] // grp == jnp.arange(kvh_n)[None, :]).astype(q.dtype)
    q3 = q.reshape(r, N_HEADS, 1, HEAD_DIM) * own[None, :, :, None]
    return q3.reshape(r, N_HEADS, kvh_n * HEAD_DIM)


def _heads_from_folded(o):
    return o[:, :, :HEAD_DIM].reshape(1, o.shape[0], Q_WIDTH)


def _pad_cols(w, n):
    return jnp.pad(w, ((0, 0), (0, n - w.shape[1])))


def _block_diag(blocks):
    rows = sum(b.shape[0] for b in blocks)
    cols = sum(b.shape[1] for b in blocks)
    out = jnp.zeros((rows, cols), blocks[0].dtype)
    r = c = 0
    for b in blocks:
        out = out.at[r:r + b.shape[0], c:c + b.shape[1]].set(b)
        r += b.shape[0]
        c += b.shape[1]
    return out


def _positions_minor(x):
    nd = x.ndim
    return jnp.transpose(x, tuple(range(nd - 4)) + (nd - 3, nd - 2, nd - 1, nd - 4))


def _nsa_compress_weights_seq(pe_k, pe_v, w1_k, w2_k, w1_v, w2_v):
    pe2 = jnp.concatenate([pe_k, pe_k, pe_v, pe_v], axis=1)

    def pair(w1):
        w = w1.reshape(NSA_BLOCK, HEAD_DIM, NSA_CMP_HIDDEN)
        z = jnp.zeros_like(w)
        return jnp.concatenate([jnp.concatenate([w, z], 2), jnp.concatenate([z, w], 2)], 1).astype(BF16)

    w2 = _block_diag([w2_k, w2_k, w2_v, w2_v]).astype(BF16)
    return pe2, pair(w1_k), pair(w1_v), w2


def _nsa_compress_weights_pages(pe_k, pe_v, w1_k, w2_k, w1_v, w2_v):
    reps = PAGE_SIZE // NSA_BLOCK

    def pair(w1):
        w = w1.reshape(NSA_BLOCK, HEAD_DIM, NSA_CMP_HIDDEN).transpose(1, 0, 2)
        z = jnp.zeros_like(w)
        w = jnp.concatenate([jnp.concatenate([w, z], 2), jnp.concatenate([z, w], 2)], 1)
        return w.reshape(HEAD_DIM * PAGE_SIZE, -1).astype(BF16)

    return (jnp.tile(pe_k.T, (1, reps)), jnp.tile(pe_v.T, (1, reps)), pair(w1_k), pair(w1_v),
            _block_diag([w2_k] * reps).astype(BF16), _block_diag([w2_v] * reps).astype(BF16))


_MOBA_OUTS = (((0, Q_WIDTH, "rope"),),
              ((Q_WIDTH, 256, "rope"), (Q_WIDTH + 256, 256, "raw")))
_SWA_OUTS = (((0, Q_WIDTH, "rope"),),
             ((Q_WIDTH, 128, "rope"), (Q_WIDTH + 128, 128, "raw")))
_NSA_OUTS = (((0, Q_WIDTH, "raw"),),
             ((0, Q_WIDTH, "rope"),),
             ((Q_WIDTH, 128, "raw"), (Q_WIDTH + 128, 128, "raw"), (Q_WIDTH + 256, 128, "rope"),
              (Q_WIDTH + 384, 128, "raw")),
             ((Q_WIDTH + 512, 128, "rope"), (Q_WIDTH + 640, 128, "raw")),
             ((Q_WIDTH + 768, 128, "sigmoid"),))
_NSA_IN_PADDED = Q_WIDTH + 7 * 128


def kernel(x_prompt, x_sample, cache_moba_kv, state_swa_kv, cache_nsa_kv, state_nsa_win_kv, state_ffn_conv, page_table, c_prompt, c_sample, ada_w, ada_b, ln_g, ln_b, moba_w_in, moba_w_o, swa_w_in, swa_w_o, swa_sink, nsa_w_in, nsa_w_o, nsa_pe_k, nsa_pe_v, nsa_w1_k, nsa_w2_k, nsa_w1_v, nsa_w2_v, ffn_w_gate, ffn_w_up, ffn_conv_w, ffn_conv_b, ffn_w_down):
    b, s, d = x_prompt.shape
    db, dec_seq, _ = x_sample.shape
    depth = ada_w.shape[0]
    ff = ffn_w_gate.shape[2]
    n_pages = page_table.shape[1]
    n_phys = cache_moba_kv.shape[1]
    past_len = n_pages * PAGE_SIZE
    assert dec_seq == 1 and past_len % MOBA_BLOCK == 0 and s % 512 == 0
    assert state_swa_kv.shape[2] <= SWA_WINDOW and state_nsa_win_kv.shape[2] <= NSA_WINDOW
    alpha = (2 * depth) ** 0.25
    tm = 512
    tf = ff // 2

    rows = b + db
    rows_pad = -(-rows // SUBLANES) * SUBLANES
    c_all = jnp.concatenate([c_prompt, c_sample, jnp.zeros((rows_pad - rows, d), F32)], 0)
    mod = _adaln_all(c_all, ada_w.reshape(depth * 2, d, 3 * d), ada_b.reshape(depth * 2, 1, 3 * d))

    def modulation(i, sub):
        m = mod[i * 2 + sub]
        mp = m[:b].reshape(b, 1, 3 * d)
        ms = m[b:b + db].reshape(1, db, 3 * d)
        return ([mp[..., k * d:(k + 1) * d] for k in range(3)], [ms[..., k * d:(k + 1) * d] for k in range(3)])

    cos_p, sin_p = _rope_tables(jnp.arange(s, dtype=jnp.int32))
    cos_s, sin_s = _rope_tables(jnp.full((db,), past_len, jnp.int32))

    moba_cache_t = _positions_minor(cache_moba_kv).reshape(-1, 2, MOBA_KV_HEADS, HEAD_DIM, PAGE_SIZE)
    nsa_cache_t = _positions_minor(cache_nsa_kv).reshape(-1, 4, NSA_KV_HEADS, HEAD_DIM, PAGE_SIZE)

    xp = x_prompt
    xs = x_sample.reshape(1, db, d)
    moba_p, moba_s, swa_p, swa_s, nsa_p, nsa_s, nsaw_p, nsaw_s, conv_p, conv_s = ([] for _ in range(10))
    for i in range(depth):
        kind, j = i % N_MIXERS, i // N_MIXERS
        (sh_p, sc_p, gt_p), (sh_s, sc_s, gt_s) = modulation(i, 0)
        if kind == 0:
            w_in = moba_w_in[j].astype(BF16)
            q_p, kv_p, vt_p = _inproj(xp, sh_p, sc_p, w_in, cos_p, sin_p, _MOBA_OUTS, tm,
                                      w_in[:, Q_WIDTH + 256:Q_WIDTH + 512].T)
            q_s, kv_s = _inproj(xs, sh_s, sc_s, w_in, cos_s, sin_s, _MOBA_OUTS, db)
            o_p = _moba_prompt(q_p, kv_p, vt_p)
            o_s = _heads_from_folded(_moba_decode(page_table + j * n_phys, moba_cache_t,
                                                  _block_diag_heads(q_s[0], MOBA_KV_HEADS), kv_s.reshape(db, 1, -1)))
            moba_p.append(kv_p.reshape(b, s, 2, MOBA_KV_HEADS, HEAD_DIM))
            moba_s.append(kv_s.reshape(db, 1, 2, MOBA_KV_HEADS, HEAD_DIM))
            w_o = moba_w_o[j]
        elif kind == 1:
            w_in = swa_w_in[j].astype(BF16)
            q_p, kv_p, vt_p = _inproj(xp, sh_p, sc_p, w_in, cos_p, sin_p, _SWA_OUTS, tm,
                                      w_in[:, Q_WIDTH + 128:Q_WIDTH + 256].T)
            q_s, kv_s = _inproj(xs, sh_s, sc_s, w_in, cos_s, sin_s, _SWA_OUTS, db)
            sink = swa_sink[j]
            o_p = _swa_prompt(q_p, kv_p, vt_p, _pad_cols(sink[None, :], LANES))
            buf = state_swa_kv[j]
            new = kv_s.reshape(db, 1, -1)
            o_s = _heads_from_folded(_swa_decode(_positions_minor(buf), new, _block_diag_heads(q_s[0], SWA_KV_HEADS),
                                                 jnp.broadcast_to(sink[:, None], (N_HEADS, LANES))))
            swa_p.append(kv_p[:, s - min(SWA_WINDOW, s):].reshape(b, -1, 2, SWA_KV_HEADS, HEAD_DIM))
            swa_s.append(jnp.concatenate([buf[:, 1:], new.reshape(db, 1, 2, SWA_KV_HEADS, HEAD_DIM)], 1))
            w_o = swa_w_o[j]
        else:
            w_in = _pad_cols(nsa_w_in[j], _NSA_IN_PADDED).astype(BF16)
            nsa_w = (nsa_pe_k[j], nsa_pe_v[j], nsa_w1_k[j], nsa_w2_k[j], nsa_w1_v[j], nsa_w2_v[j])
            w_t = jnp.concatenate([w_in[:, Q_WIDTH + 384:Q_WIDTH + 512], w_in[:, Q_WIDTH + 640:Q_WIDTH + 768]], 1).T
            qraw_p, qrot_p, rows_p, win_p, gates_p, vt_p = _inproj(xp, sh_p, sc_p, w_in, cos_p, sin_p, _NSA_OUTS,
                                                                   tm, w_t)
            qraw_s, qrot_s, rows_s, win_s, gates_s = _inproj(xs, sh_s, sc_s, w_in, cos_s, sin_s, _NSA_OUTS, db)
            n_blk = s // NSA_BLOCK
            cmp_p = _nsa_compress_seq(rows_p.reshape(b * s, -1), *_nsa_compress_weights_seq(*nsa_w))
            cmp_p = cmp_p.reshape(b, n_blk, 2, NSA_KV_HEADS, HEAD_DIM).transpose(2, 0, 3, 1, 4)
            cmp_p = jnp.pad(cmp_p, ((0, 0), (0, 0), (0, 0), (0, HEAD_DIM - n_blk), (0, 0)))
            cmp_p = cmp_p.reshape(2, b, NSA_KV_HEADS * HEAD_DIM, HEAD_DIM)
            kc_p = jnp.pad(cmp_p[0], ((0, 0), (0, 0), (0, LANES - HEAD_DIM)))
            vct_p = cmp_p[1].transpose(0, 2, 1)
            o_p = _nsa_prompt(qraw_p, qrot_p, rows_p, win_p, vt_p, kc_p, vct_p, gates_p)
            pt = page_table + j * n_phys
            kcvc = _nsa_compress_pages(pt, nsa_cache_t, *_nsa_compress_weights_pages(*nsa_w))
            kcvc = kcvc.reshape(db, n_pages, NSA_KV_HEADS, 2, PAGE_SIZE // NSA_BLOCK, HEAD_DIM)
            kcvc = kcvc.transpose(0, 1, 4, 3, 2, 5).reshape(db, -1, 2 * NSA_KV_HEADS * HEAD_DIM)
            win_buf = state_nsa_win_kv[j]
            new_win = win_s.reshape(db, 1, -1)
            gates3 = jnp.pad(gates_s[0, :, :3 * N_HEADS].reshape(db, N_HEADS, 3), ((0, 0), (0, 0), (0, LANES - 3)))
            o_s = _heads_from_folded(_nsa_decode(pt, nsa_cache_t, kcvc, _positions_minor(win_buf),
                                                 _block_diag_heads(qraw_s[0], NSA_KV_HEADS),
                                                 _block_diag_heads(qrot_s[0], NSA_KV_HEADS),
                                                 rows_s.reshape(db, 1, -1), new_win, gates3))
            nsa_p.append(rows_p.reshape(b, s, 4, NSA_KV_HEADS, HEAD_DIM))
            nsa_s.append(rows_s.reshape(db, 1, 4, NSA_KV_HEADS, HEAD_DIM))
            nsaw_p.append(win_p[:, s - min(NSA_WINDOW, s):].reshape(b, -1, 2, NSA_KV_HEADS, HEAD_DIM))
            nsaw_s.append(jnp.concatenate([win_buf[:, 1:], new_win.reshape(db, 1, 2, NSA_KV_HEADS, HEAD_DIM)], 1))
            w_o = nsa_w_o[j]
        w_o = w_o.astype(BF16)
        g0, b0 = ln_g[i, 0][None, :], ln_b[i, 0][None, :]
        xp = _outproj_ln(o_p, w_o, xp, gt_p, g0, b0, alpha, tm)
        xs = _outproj_ln(o_s, w_o, xs, gt_s, g0, b0, alpha, db)

        (sh_p, sc_p, gt_p), (sh_s, sc_s, gt_s) = modulation(i, 1)
        wg, wu, wd = ffn_w_gate[i].astype(BF16), ffn_w_up[i].astype(BF16), ffn_w_down[i].astype(BF16)
        cwt, cbs = ffn_conv_w[i], ffn_conv_b[i][None, :]
        g1, b1 = ln_g[i, 1][None, :], ln_b[i, 1][None, :]
        xp, st_p = _ffn_seq(xp, sh_p, sc_p, gt_p, jnp.zeros((b, CONV_WIDTH - 1, ff), F32), wg, wu, cwt, cbs, wd,
                            g1, b1, alpha, tm)
        prev = state_ffn_conv[i]
        xs2, g_new = _ffn_step(xs[0], sh_s[0], sc_s[0], gt_s[0], prev[:, 0], prev[:, 1], wg, wu, cwt, cbs, wd,
                               g1, b1, alpha, tf)
        xs = xs2[None]
        conv_p.append(st_p)
        conv_s.append(jnp.stack([prev[:, 1], g_new], 1))
    return (xp, xs.reshape(db, 1, d), jnp.stack(moba_p), jnp.stack(moba_s), jnp.stack(swa_p), jnp.stack(swa_s),
            jnp.stack(nsa_p), jnp.stack(nsa_s), jnp.stack(nsaw_p), jnp.stack(nsaw_s),
            jnp.stack(conv_p), jnp.stack(conv_s))
```

```python
import functools

import jax
import jax.numpy as jnp
from jax import lax
from jax.experimental import pallas as pl
from jax.experimental.pallas import tpu as pltpu

F32 = jnp.float32
BF16 = jnp.bfloat16

HEAD_DIM = 64
N_HEADS = 16
Q_WIDTH = N_HEADS * HEAD_DIM
ROPE_THETA = 10000.0
N_MIXERS = 3
PAGE_SIZE = 128
MOBA_KV_HEADS = 4
MOBA_BLOCK = 256
MOBA_TOPK = 3
SWA_KV_HEADS = 2
SWA_WINDOW = 128
NSA_KV_HEADS = 2
NSA_BLOCK = 64
NSA_TOPN = 16
NSA_WINDOW = 512
NSA_CMP_HIDDEN = 128
CONV_WIDTH = 3
LN_EPS = 1e-5
ATTN_SCALE = HEAD_DIM ** -0.5
LOG2E = 1.4426950408889634

LANES = 128
SUBLANES = 8
TQ = 256
MASKED = -1e30
GATE_FLOOR = -3e38
FLASH_SPAN = 2
FFN_HALO = 16
COMPRESS_GROUP = 8
TILE_PITCH = HEAD_DIM + SUBLANES
VMEM_LIMIT = 56 * 1024 * 1024


def _cparams(n_axes):
    return pltpu.CompilerParams(dimension_semantics=("arbitrary",) * n_axes,
                                vmem_limit_bytes=VMEM_LIMIT)


def _dot(a, b):
    return jnp.dot(a, b, preferred_element_type=F32)


def _dot_nt(a, b, precision=None):
    return lax.dot_general(a, b, (((1,), (1,)), ((), ())), precision=precision,
                           preferred_element_type=F32)


def _iota(shape, axis):
    return lax.broadcasted_iota(jnp.int32, shape, axis)


def _adaln_kernel(c_ref, w_ref, b_ref, o_ref):
    c = c_ref[...]
    a = c * (1.0 / (1.0 + jnp.exp(-c)))
    o_ref[0] = _dot(a.astype(BF16), w_ref[0].astype(BF16)) + b_ref[0]


def _adaln_all(c_all, ada_w, ada_b):
    n_sub, d, d3 = ada_w.shape
    rows = c_all.shape[0]
    tn = 1024
    return pl.pallas_call(
        _adaln_kernel,
        grid=(n_sub, d3 // tn),
        in_specs=[pl.BlockSpec((rows, d), lambda l, j: (0, 0)),
                  pl.BlockSpec((1, d, tn), lambda l, j: (l, 0, j)),
                  pl.BlockSpec((1, 1, tn), lambda l, j: (l, 0, j))],
        out_specs=pl.BlockSpec((1, rows, tn), lambda l, j: (l, 0, j)),
        out_shape=jax.ShapeDtypeStruct((n_sub, rows, d3), F32),
        compiler_params=_cparams(2),
        name="adaln",
    )(c_all, ada_w, ada_b)


def _rope_chunk(y, cos, sin_signed):
    lo = (_iota(y.shape, 1) % HEAD_DIM) < (HEAD_DIM // 2)
    swapped = jnp.where(lo, pltpu.roll(y, LANES - HEAD_DIM // 2, 1), pltpu.roll(y, HEAD_DIM // 2, 1))
    return y * cos + swapped * sin_signed


def _inproj_kernel(x_ref, sh_ref, sc_ref, w_ref, cos_ref, sin_ref, *rest, outs, has_t):
    h = (x_ref[0] * (1.0 + sc_ref[0]) + sh_ref[0]).astype(BF16)
    y = _dot(h, w_ref[...])
    cos = cos_ref[...]
    sin = sin_ref[...]
    out_refs = rest
    if has_t:
        wt_ref, out_refs, ot_ref = rest[0], rest[1:-1], rest[-1]
        ot_ref[0] = _dot_nt(wt_ref[...], h)
    for o_ref, segs in zip(out_refs, outs):
        dst = 0
        for (start, width, op) in segs:
            for c in range(0, width, LANES):
                blk = y[:, start + c:start + c + LANES]
                if op == "rope":
                    blk = _rope_chunk(blk, cos, sin)
                elif op == "sigmoid":
                    blk = 1.0 / (1.0 + jnp.exp(-blk))
                o_ref[0, :, dst + c:dst + c + LANES] = blk
            dst += width


def _inproj(x, shift, scale, w, cos, sin, outs, tm, w_t=None):
    g, r, d = x.shape
    n = w.shape[1]
    rm = shift.shape[1]
    mod_spec = (pl.BlockSpec((1, 1, d), lambda a, b: (a, 0, 0)) if rm == 1
                else pl.BlockSpec((1, tm, d), lambda a, b: (a, b, 0)))
    widths = [sum(s[1] for s in segs) for segs in outs]
    in_specs = [pl.BlockSpec((1, tm, d), lambda a, b: (a, b, 0)), mod_spec, mod_spec,
                pl.BlockSpec((d, n), lambda a, b: (0, 0)),
                pl.BlockSpec((tm, LANES), lambda a, b: (b, 0)),
                pl.BlockSpec((tm, LANES), lambda a, b: (b, 0))]
    out_specs = [pl.BlockSpec((1, tm, wd), lambda a, b: (a, b, 0)) for wd in widths]
    out_shape = [jax.ShapeDtypeStruct((g, r, wd), F32) for wd in widths]
    args = [x, shift, scale, w, cos, sin]
    if w_t is not None:
        nt = w_t.shape[0]
        in_specs.append(pl.BlockSpec((nt, d), lambda a, b: (0, 0)))
        out_specs.append(pl.BlockSpec((1, nt, tm), lambda a, b: (a, 0, b)))
        out_shape.append(jax.ShapeDtypeStruct((g, nt, r), F32))
        args.append(w_t)
    return pl.pallas_call(
        functools.partial(_inproj_kernel, outs=outs, has_t=w_t is not None),
        grid=(g, r // tm),
        in_specs=in_specs,
        out_specs=out_specs,
        out_shape=out_shape,
        compiler_params=_cparams(2),
        name="inproj",
    )(*args)


def _residual_ln(x, y, gate, g, b, alpha):
    z = alpha * x + (1.0 + gate) * y
    mu = jnp.mean(z, -1, keepdims=True)
    zc = z - mu
    var = jnp.mean(zc * zc, -1, keepdims=True)
    return zc * lax.rsqrt(var + LN_EPS) * g + b


def _outproj_kernel(o_ref, w_ref, x_ref, gt_ref, g_ref, b_ref, out_ref, *, alpha):
    y = _dot(o_ref[0].astype(BF16), w_ref[...])
    out_ref[0] = _residual_ln(x_ref[0], y, gt_ref[0], g_ref[...], b_ref[...], alpha)


def _outproj_ln(o, w, x, gate, ln_g, ln_b, alpha, tm):
    g, r, d = x.shape
    k = o.shape[2]
    rm = gate.shape[1]
    mod_spec = (pl.BlockSpec((1, 1, d), lambda a, b: (a, 0, 0)) if rm == 1
                else pl.BlockSpec((1, tm, d), lambda a, b: (a, b, 0)))
    return pl.pallas_call(
        functools.partial(_outproj_kernel, alpha=alpha),
        grid=(g, r // tm),
        in_specs=[pl.BlockSpec((1, tm, k), lambda a, b: (a, b, 0)),
                  pl.BlockSpec((k, d), lambda a, b: (0, 0)),
                  pl.BlockSpec((1, tm, d), lambda a, b: (a, b, 0)), mod_spec,
                  pl.BlockSpec((1, d), lambda a, b: (0, 0)),
                  pl.BlockSpec((1, d), lambda a, b: (0, 0))],
        out_specs=pl.BlockSpec((1, tm, d), lambda a, b: (a, b, 0)),
        out_shape=jax.ShapeDtypeStruct((g, r, d), F32),
        compiler_params=_cparams(2),
        name="outproj_ln",
    )(o, w, x, gate, ln_g, ln_b)


def _silu(x):
    return x * (1.0 / (1.0 + jnp.exp(-x)))


def _ffn_seq_kernel(x_ref, xh_ref, sh_ref, sc_ref, gt_ref, prev_ref, wg_ref, wu_ref, cw_ref, cb_ref,
                    wd_ref, g_ref, b_ref, out_ref, st_ref, *, alpha):
    r = pl.program_id(1)
    x = x_ref[0]
    tm = x.shape[0]
    he = (jnp.concatenate([xh_ref[0], x], axis=0) * (1.0 + sc_ref[0]) + sh_ref[0]).astype(BF16)
    h = he[FFN_HALO:]
    ge = _dot(he, wg_ref[...])
    gcur = ge[FFN_HALO:]
    first = r == 0
    prev = prev_ref[0]
    pm1 = jnp.where(first, prev[1:2], ge[FFN_HALO - 1:FFN_HALO])
    pm2 = jnp.where(first, prev[0:1], ge[FFN_HALO - 2:FFN_HALO - 1])
    row = _iota(gcur.shape, 0)
    g1 = jnp.where(row == 0, pm1, pltpu.roll(gcur, 1, 0))
    g2 = jnp.where(row == 0, pm2, jnp.where(row == 1, pm1, pltpu.roll(gcur, 2, 0)))
    cw = cw_ref[...]
    conv = cb_ref[...] + cw[0:1] * g2 + cw[1:2] * g1 + cw[2:3] * gcur
    act = _silu(conv) * _dot(h, wu_ref[...])
    y = _dot(act.astype(BF16), wd_ref[...])
    st_ref[0, 0] = gcur[tm - 2:tm]
    out_ref[0] = _residual_ln(x, y, gt_ref[0], g_ref[...], b_ref[...], alpha)


def _ffn_seq(x, shift, scale, gate, prev, wg, wu, cw, cb, wd, ln_g, ln_b, alpha, tm):
    g, r, d = x.shape
    ff = wg.shape[1]
    n_r = r // tm
    hb = tm // FFN_HALO
    mod = pl.BlockSpec((1, 1, d), lambda a, b: (a, 0, 0))
    const = lambda shape: pl.BlockSpec(shape, lambda a, b: (0, 0), pipeline_mode=pl.Buffered(1))
    out, st = pl.pallas_call(
        functools.partial(_ffn_seq_kernel, alpha=alpha),
        grid=(g, n_r),
        in_specs=[pl.BlockSpec((1, tm, d), lambda a, b: (a, b, 0)),
                  pl.BlockSpec((1, FFN_HALO, d), lambda a, b: (a, jnp.maximum(b * hb - 1, 0), 0)),
                  mod, mod, mod,
                  pl.BlockSpec((1, 2, ff), lambda a, b: (a, 0, 0)),
                  const((d, ff)), const((d, ff)), const((CONV_WIDTH, ff)), const((1, ff)), const((ff, d)),
                  const((1, d)), const((1, d))],
        out_specs=[pl.BlockSpec((1, tm, d), lambda a, b: (a, b, 0)),
                   pl.BlockSpec((1, 1, 2, ff), lambda a, b: (a, b, 0, 0))],
        out_shape=[jax.ShapeDtypeStruct((g, r, d), F32),
                   jax.ShapeDtypeStruct((g, n_r, 2, ff), F32)],
        compiler_params=_cparams(2),
        name="ffn_seq",
    )(x, x, shift, scale, gate, prev, wg, wu, cw, cb, wd, ln_g, ln_b)
    return out, st[:, n_r - 1]


def _ffn_step_kernel(x_ref, sh_ref, sc_ref, gt_ref, p0_ref, p1_ref, wg_ref, wu_ref, cw_ref, cb_ref,
                     wd_ref, g_ref, b_ref, out_ref, st_ref, acc_ref, *, alpha, n_f):
    f = pl.program_id(0)
    x = x_ref[...]
    h = (x * (1.0 + sc_ref[...]) + sh_ref[...]).astype(BF16)
    gcur = _dot(h, wg_ref[...])
    cw = cw_ref[...]
    conv = cb_ref[...] + cw[0:1] * p0_ref[...] + cw[1:2] * p1_ref[...] + cw[2:3] * gcur
    act = _silu(conv) * _dot(h, wu_ref[...])
    part = _dot(act.astype(BF16), wd_ref[...])
    st_ref[...] = gcur

    @pl.when(f == 0)
    def _():
        acc_ref[...] = part

    @pl.when(f != 0)
    def _():
        acc_ref[...] += part

    @pl.when(f == n_f - 1)
    def _():
        out_ref[...] = _residual_ln(x, acc_ref[...], gt_ref[...], g_ref[...], b_ref[...], alpha)


def _ffn_step(x, shift, scale, gate, p0, p1, wg, wu, cw, cb, wd, ln_g, ln_b, alpha, tf):
    r, d = x.shape
    ff = wg.shape[1]
    n_f = ff // tf
    full = pl.BlockSpec((r, d), lambda c: (0, 0))
    return pl.pallas_call(
        functools.partial(_ffn_step_kernel, alpha=alpha, n_f=n_f),
        grid=(n_f,),
        in_specs=[full, full, full, full,
                  pl.BlockSpec((r, tf), lambda c: (0, c)),
                  pl.BlockSpec((r, tf), lambda c: (0, c)),
                  pl.BlockSpec((d, tf), lambda c: (0, c)),
                  pl.BlockSpec((d, tf), lambda c: (0, c)),
                  pl.BlockSpec((CONV_WIDTH, tf), lambda c: (0, c)),
                  pl.BlockSpec((1, tf), lambda c: (0, c)),
                  pl.BlockSpec((tf, d), lambda c: (c, 0)),
                  pl.BlockSpec((1, d), lambda c: (0, 0)),
                  pl.BlockSpec((1, d), lambda c: (0, 0))],
        out_specs=[full, pl.BlockSpec((r, tf), lambda c: (0, c))],
        out_shape=[jax.ShapeDtypeStruct((r, d), F32), jax.ShapeDtypeStruct((r, ff), F32)],
        scratch_shapes=[pltpu.VMEM((r, d), F32)],
        compiler_params=_cparams(1),
        name="ffn_step",
    )(x, shift, scale, gate, p0, p1, wg, wu, cw, cb, wd, ln_g, ln_b)


def _head_slot(x, h):
    pair = x[:, (h // 2) * LANES:(h // 2 + 1) * LANES]
    return pltpu.roll(pair, HEAD_DIM, 1) if h % 2 else pair


def _slot(x, j, fill=0.0):
    lane = _iota((x.shape[0], LANES), 1)
    return jnp.where(lane < HEAD_DIM, _head_slot(x, j), fill)


def _flash_t(qg, k_ref, lane0, vt_ref, row0, i, lo, window, masked_past):
    rows = qg.shape[0]

    def scores(n_lo, span, masked):
        start = pl.multiple_of(n_lo * TQ, TQ)
        s = _dot_nt(k_ref[pl.ds(start, span * TQ), lane0:lane0 + LANES], qg)
        if masked:
            dist = (i - n_lo) * TQ + _iota(s.shape, 1) % TQ - _iota(s.shape, 0)
            ok = dist >= 0
            if window is not None:
                ok = ok & (dist <= window)
            s = jnp.where(ok, s, MASKED)
        return s

    def values(n_lo, span, p):
        vt = [vt_ref[n_lo + u, row0:row0 + HEAD_DIM, :] for u in range(span)]
        return _dot(vt[0] if span == 1 else jnp.concatenate(vt, axis=1), p.astype(BF16))

    s = scores(i, 1, True)
    m = jnp.max(s, axis=0, keepdims=True)
    p = jnp.exp2(s - m)
    l = jnp.sum(p, axis=0, keepdims=True)
    acc = values(i, 1, p)

    def step(n_lo, span, carry):
        m, l, acc = carry
        s = scores(n_lo, span, masked_past)
        m_new = jnp.maximum(m, jnp.max(s, axis=0, keepdims=True))
        a = jnp.exp2(m - m_new)
        p = jnp.exp2(s - m_new)
        l = a * l + jnp.sum(p, axis=0, keepdims=True)
        acc = a * acc + values(n_lo, span, p)
        return m_new, l, acc

    n_big = (i - lo) // FLASH_SPAN
    carry = lax.fori_loop(0, n_big, lambda t, c: step(i - (t + 1) * FLASH_SPAN, FLASH_SPAN, c), (m, l, acc))
    top = i - n_big * FLASH_SPAN
    m, l, acc = lax.fori_loop(0, top - lo, lambda t, c: step(top - 1 - t, 1, c), carry)
    return acc, m, l


def _store_heads(o_ref, out_t, head0, n_heads):
    for gp in range(n_heads // 2):
        pair = jnp.concatenate([out_t[:, (2 * gp) * TQ:(2 * gp + 1) * TQ],
                                out_t[:, (2 * gp + 1) * TQ:(2 * gp + 2) * TQ]], axis=0)
        c0 = ((head0 + 2 * gp) // 2) * LANES
        o_ref[0, :, c0:c0 + LANES] = pair.T


def _stack_queries(q, heads, bias=None):
    lane = _iota((TQ, LANES), 1)
    out = []
    for k, h in enumerate(heads):
        qh = jnp.where(lane < HEAD_DIM, _head_slot(q, h), 0.0) * (ATTN_SCALE * LOG2E)
        out.append((qh if bias is None else qh + bias[k]).astype(BF16))
    return jnp.concatenate(out, axis=0)


def _bias_lanes(bias_t):
    n = bias_t.shape[0]
    parts = [jnp.zeros((HEAD_DIM, TQ), F32), bias_t]
    if n < HEAD_DIM:
        parts.append(jnp.zeros((HEAD_DIM - n, TQ), F32))
    return jnp.concatenate(parts, axis=0).T


def _window_chunks(window):
    return -(-window // TQ)


def _moba_prompt_kernel(q_ref, kv_ref, vt_ref, o_ref, kslot_ref, vt_sc, kmean_ref, *, n_blocks):
    i = pl.program_id(1)
    kvh_n = MOBA_KV_HEADS
    grp = N_HEADS // kvh_n
    kw = kvh_n * HEAD_DIM

    @pl.when(i == 0)
    def _():
        for n in range(n_blocks):
            vt_sc[n] = vt_ref[0, :, n * TQ:(n + 1) * TQ].astype(BF16)
        blk_rows = _iota(kmean_ref.shape, 0)

        def prep(n, kmean):
            start = pl.multiple_of(n * TQ, TQ)
            kt = kv_ref[0, pl.ds(start, TQ), 0:kw]
            block_lane = (_iota((TQ, LANES), 1) - HEAD_DIM == n).astype(F32)
            means = []
            for j in range(kvh_n):
                kslot_ref[pl.ds(start, TQ), j * LANES:(j + 1) * LANES] = _slot(kt, j, block_lane).astype(BF16)
                means.append(jnp.mean(_slot(kt, j), axis=0, keepdims=True))
            return jnp.where(blk_rows == n, jnp.concatenate(means, axis=1), kmean)

        kmean_ref[...] = lax.fori_loop(0, n_blocks, prep, jnp.zeros(kmean_ref.shape, F32))

    q = q_ref[0]
    lane = _iota((TQ, LANES), 1)
    blk_io = _iota((kmean_ref.shape[0], TQ), 0)
    eligible = blk_io < i
    for j in range(kvh_n):
        kmj = kmean_ref[:, j * LANES:(j + 1) * LANES]
        bias = []
        for gi in range(grp):
            qh = jnp.where(lane < HEAD_DIM, _head_slot(q, j * grp + gi), 0.0)
            gate = _dot_nt(kmj, qh, precision=lax.Precision.HIGHEST)
            gate = jnp.where(eligible, gate, GATE_FLOOR)
            rank = jnp.zeros(gate.shape, jnp.int32)
            for m_idx in range(n_blocks):
                row = gate[m_idx:m_idx + 1, :]
                beats = (row > gate) | ((row == gate) & (m_idx < blk_io))
                rank = rank + beats.astype(jnp.int32)
            sel = (eligible & (rank < MOBA_TOPK)) | (blk_io == i)
            bias.append(_bias_lanes(jnp.where(sel, 0.0, MASKED)))
        qg = _stack_queries(q, range(j * grp, (j + 1) * grp), bias)
        acc, _, l = _flash_t(qg, kslot_ref, j * LANES, vt_sc, j * HEAD_DIM, i, 0, None, False)
        _store_heads(o_ref, acc / l, j * grp, grp)


def _moba_prompt(q, kv, v_t):
    b, s, _ = q.shape
    n_blocks = s // MOBA_BLOCK
    assert MOBA_BLOCK == TQ and s % TQ == 0 and n_blocks <= HEAD_DIM
    kw = MOBA_KV_HEADS * HEAD_DIM
    nbp = -(-n_blocks // SUBLANES) * SUBLANES
    return pl.pallas_call(
        functools.partial(_moba_prompt_kernel, n_blocks=n_blocks),
        grid=(b, s // TQ),
        in_specs=[pl.BlockSpec((1, TQ, Q_WIDTH), lambda a, i: (a, i, 0)),
                  pl.BlockSpec((1, s, 2 * kw), lambda a, i: (a, 0, 0)),
                  pl.BlockSpec((1, kw, s), lambda a, i: (a, 0, 0))],
        out_specs=pl.BlockSpec((1, TQ, Q_WIDTH), lambda a, i: (a, i, 0)),
        out_shape=jax.ShapeDtypeStruct((b, s, Q_WIDTH), F32),
        scratch_shapes=[pltpu.VMEM((s, MOBA_KV_HEADS * LANES), BF16),
                        pltpu.VMEM((n_blocks, kw, TQ), BF16),
                        pltpu.VMEM((nbp, MOBA_KV_HEADS * LANES), F32)],
        compiler_params=_cparams(2),
        name="moba_prompt",
    )(q, kv, v_t)


def _swa_prompt_kernel(q_ref, kv_ref, vt_ref, sink_ref, o_ref, kslot_ref, vt_sc, *, n_chunks):
    i = pl.program_id(1)
    kvh_n = SWA_KV_HEADS
    grp = N_HEADS // kvh_n
    kw = kvh_n * HEAD_DIM

    @pl.when(i == 0)
    def _():
        for n in range(n_chunks):
            vt_sc[n] = vt_ref[0, :, n * TQ:(n + 1) * TQ].astype(BF16)

        def prep(n, carry):
            start = pl.multiple_of(n * TQ, TQ)
            kt = kv_ref[0, pl.ds(start, TQ), 0:kw]
            for j in range(kvh_n):
                kslot_ref[pl.ds(start, TQ), j * LANES:(j + 1) * LANES] = _slot(kt, j).astype(BF16)
            return carry

        lax.fori_loop(0, n_chunks, prep, 0)

    q = q_ref[0]
    lo = jnp.maximum(i - _window_chunks(SWA_WINDOW), 0)
    for j in range(kvh_n):
        heads = range(j * grp, (j + 1) * grp)
        acc, m, l = _flash_t(_stack_queries(q, heads), kslot_ref, j * LANES, vt_sc, j * HEAD_DIM, i, lo,
                             SWA_WINDOW, True)
        sink = jnp.concatenate([jnp.broadcast_to(sink_ref[0:1, h:h + 1], (1, TQ)) for h in heads], axis=1) * LOG2E
        m_f = jnp.maximum(m, sink)
        a = jnp.exp2(m - m_f)
        den = l * a + jnp.exp2(sink - m_f)
        _store_heads(o_ref, acc * (a / den), j * grp, grp)


def _swa_prompt(q, kv, v_t, sink):
    b, s, _ = q.shape
    assert s % TQ == 0
    kw = SWA_KV_HEADS * HEAD_DIM
    return pl.pallas_call(
        functools.partial(_swa_prompt_kernel, n_chunks=s // TQ),
        grid=(b, s // TQ),
        in_specs=[pl.BlockSpec((1, TQ, Q_WIDTH), lambda a, i: (a, i, 0)),
                  pl.BlockSpec((1, s, 2 * kw), lambda a, i: (a, 0, 0)),
                  pl.BlockSpec((1, kw, s), lambda a, i: (a, 0, 0)),
                  pl.BlockSpec((1, LANES), lambda a, i: (0, 0))],
        out_specs=pl.BlockSpec((1, TQ, Q_WIDTH), lambda a, i: (a, i, 0)),
        out_shape=jax.ShapeDtypeStruct((b, s, Q_WIDTH), F32),
        scratch_shapes=[pltpu.VMEM((s, SWA_KV_HEADS * LANES), BF16),
                        pltpu.VMEM((s // TQ, kw, TQ), BF16)],
        compiler_params=_cparams(2),
        name="swa_prompt",
    )(q, kv, v_t, sink)


def _gelu_tanh(x):
    return x * (0.5 * (1.0 + jnp.tanh(0.7978845608028654 * (x + 0.044715 * (x * x * x)))))


def _compress_rows(xk_ref, xv_ref, n_blk, pe_ref, w1k_ref, w1v_ref, w2_ref):
    half = NSA_KV_HEADS * HEAD_DIM

    def body(p, carry):
        ak, av = carry
        pe = pe_ref[pl.ds(p, 1), :]
        xk = (xk_ref[pl.ds(p, n_blk, stride=NSA_BLOCK), :] + pe[:, 0:half]).astype(BF16)
        xv = (xv_ref[pl.ds(p, n_blk, stride=NSA_BLOCK), :] + pe[:, half:2 * half]).astype(BF16)
        ak = ak + _dot(xk, w1k_ref[p])
        av = av + _dot(xv, w1v_ref[p])
        return ak, av

    zero = jnp.zeros((n_blk, NSA_KV_HEADS * NSA_CMP_HIDDEN), F32)
    ak, av = lax.fori_loop(0, NSA_BLOCK, body, (zero, zero))
    hid = jnp.concatenate([_gelu_tanh(ak), _gelu_tanh(av)], axis=1).astype(BF16)
    return _dot(hid, w2_ref[...])


def _nsa_compress_seq_kernel(xk_ref, xv_ref, pe_ref, w1k_ref, w1v_ref, w2_ref, o_ref, *, n_blk):
    o_ref[...] = _compress_rows(xk_ref, xv_ref, n_blk, pe_ref, w1k_ref, w1v_ref, w2_ref)


def _nsa_compress_seq(rows2d, pe2, w1k, w1v, w2):
    t = rows2d.shape[0]
    n_blk = min(64, t // NSA_BLOCK)
    tr = n_blk * NSA_BLOCK
    half = 2 * NSA_KV_HEADS * HEAD_DIM
    full = lambda shape: pl.BlockSpec(shape, lambda j: (0,) * len(shape))
    return pl.pallas_call(
        functools.partial(_nsa_compress_seq_kernel, n_blk=n_blk),
        grid=(t // tr,),
        in_specs=[pl.BlockSpec((tr, LANES), lambda j: (j, 0)),
                  pl.BlockSpec((tr, LANES), lambda j: (j, 1)),
                  full(pe2.shape), full(w1k.shape), full(w1v.shape), full(w2.shape)],
        out_specs=pl.BlockSpec((n_blk, half), lambda j: (j, 0)),
        out_shape=jax.ShapeDtypeStruct((t // NSA_BLOCK, half), F32),
        compiler_params=_cparams(1),
        name="nsa_compress_seq",
    )(rows2d, rows2d, pe2, w1k, w1v, w2)


def _nsa_compress_pages_kernel(pt_ref, *refs, n_pages):
    page_refs = refs[:n_pages]
    pek_ref, pev_ref, w1k_ref, w1v_ref, w2k_ref, w2v_ref, o_ref, xk_sc, xv_sc = refs[n_pages:]
    for j in range(n_pages):
        for u in range(NSA_KV_HEADS):
            r0 = (j * NSA_KV_HEADS + u) * TILE_PITCH
            xk_sc[r0:r0 + HEAD_DIM, :] = page_refs[j][0, u]
            xv_sc[r0:r0 + HEAD_DIM, :] = page_refs[j][1, u]
    n_rows = n_pages * NSA_KV_HEADS

    def hidden(x_sc, pe_ref, w1_ref):
        acc = None
        for d0 in range(0, HEAD_DIM, COMPRESS_GROUP):
            x = jnp.concatenate(
                [(x_sc[pl.ds(d, n_rows, stride=TILE_PITCH), :] + pe_ref[d:d + 1, :]).astype(BF16)
                 for d in range(d0, d0 + COMPRESS_GROUP)], axis=1)
            part = _dot(x, w1_ref[d0 * PAGE_SIZE:(d0 + COMPRESS_GROUP) * PAGE_SIZE, :])
            acc = part if acc is None else acc + part
        return _gelu_tanh(acc).astype(BF16)

    kc = _dot(hidden(xk_sc, pek_ref, w1k_ref), w2k_ref[...])
    vc = _dot(hidden(xv_sc, pev_ref, w1v_ref), w2v_ref[...])
    o_ref[0] = jnp.concatenate([kc, vc], axis=1)


def _nsa_page_spec(j, comp_block):
    return pl.BlockSpec((None, 2, NSA_KV_HEADS, HEAD_DIM, PAGE_SIZE), lambda b, pt: (pt[b, j], comp_block, 0, 0, 0))


def _nsa_compress_pages(pt, cache_t, pek_t, pev_t, w1k_t, w1v_t, w2k, w2v):
    db, n_pages = pt.shape
    n_rows = n_pages * NSA_KV_HEADS
    full = lambda shape: pl.BlockSpec(shape, lambda b, pt: (0,) * len(shape))
    consts = (pek_t, pev_t, w1k_t, w1v_t, w2k, w2v)
    grid_spec = pltpu.PrefetchScalarGridSpec(
        num_scalar_prefetch=1,
        grid=(db,),
        in_specs=[_nsa_page_spec(j, 0) for j in range(n_pages)] + [full(c.shape) for c in consts],
        out_specs=pl.BlockSpec((1, n_rows, 2 * LANES), lambda b, pt: (b, 0, 0)),
        scratch_shapes=[pltpu.VMEM((n_rows * TILE_PITCH, PAGE_SIZE), F32),
                        pltpu.VMEM((n_rows * TILE_PITCH, PAGE_SIZE), F32)],
    )
    return pl.pallas_call(
        functools.partial(_nsa_compress_pages_kernel, n_pages=n_pages),
        grid_spec=grid_spec,
        out_shape=jax.ShapeDtypeStruct((db, n_rows, 2 * LANES), F32),
        compiler_params=_cparams(1),
        name="nsa_compress_pages",
    )(pt, *([cache_t] * n_pages), *consts)


def _nsa_prompt_kernel(qraw_ref, qrot_ref, ks_ref, kw_ref, vt_ref, kc_ref, vct_ref, gates_ref, o_ref,
                       kslc_sc, kwin_sc, vt_sc, ocmp_sc, *, n_chunks, n_blocks):
    i = pl.program_id(1)
    kvh_n = NSA_KV_HEADS
    grp = N_HEADS // kvh_n

    @pl.when(i == 0)
    def _():
        for n in range(n_chunks):
            vt_sc[n] = vt_ref[0, :, n * TQ:(n + 1) * TQ].astype(BF16)

        def prep(n, carry):
            start = pl.multiple_of(n * TQ, TQ)
            ks = ks_ref[0, pl.ds(start, TQ), :]
            kwn = kw_ref[0, pl.ds(start, TQ), :]
            blk = (n * TQ + _iota((TQ, LANES), 0)) // NSA_BLOCK
            block_lane = (_iota((TQ, LANES), 1) - HEAD_DIM == blk).astype(F32)
            for j in range(kvh_n):
                kslc_sc[pl.ds(start, TQ), j * LANES:(j + 1) * LANES] = _slot(ks, j, block_lane).astype(BF16)
                kwin_sc[pl.ds(start, TQ), j * LANES:(j + 1) * LANES] = _slot(kwn, j).astype(BF16)
            return carry

        lax.fori_loop(0, n_chunks, prep, 0)

    lane = _iota((TQ, LANES), 1)
    q_raw = qraw_ref[0]
    kcb = kc_ref[0].astype(BF16)
    vctb = vct_ref[0].astype(BF16)
    row_io = _iota((LANES, TQ), 0)
    n_io = row_io % HEAD_DIM
    t = i * TQ + _iota((LANES, TQ), 1)
    avail = ((n_io + 1) * NSA_BLOCK - 1 <= t) & (n_io < n_blocks)
    score = jnp.zeros((LANES, TQ), F32)
    for h in range(N_HEADS):
        qh = (jnp.where(lane < HEAD_DIM, _head_slot(q_raw, h), 0.0) * ATTN_SCALE).astype(BF16)
        ok = avail & ((row_io // HEAD_DIM) == (h // grp))
        s = jnp.where(ok, _dot_nt(kcb, qh), MASKED)
        m = jnp.max(s, axis=0, keepdims=True)
        p = jnp.exp(s - m) * ok.astype(F32)
        p = p / jnp.maximum(jnp.sum(p, axis=0, keepdims=True), 1e-30)
        score = score + p
        ocmp_sc[h * HEAD_DIM:(h + 1) * HEAD_DIM, :] = _dot(vctb, p.astype(BF16))

    own = t // NSA_BLOCK
    forced = ((n_io == 0) | (n_io == own) | (n_io == own - 1)) & (n_io < n_blocks)
    sc = jnp.where(avail, score, -jnp.inf)
    sc = jnp.where(forced, jnp.inf, sc)
    rank = jnp.zeros((LANES, TQ), jnp.int32)
    for m_idx in range(n_blocks):
        row = jnp.where(row_io < HEAD_DIM, sc[m_idx:m_idx + 1, :], sc[HEAD_DIM + m_idx:HEAD_DIM + m_idx + 1, :])
        beats = (row > sc) | ((row == sc) & (m_idx < n_io))
        rank = rank + beats.astype(jnp.int32)
    bias = jnp.where((rank < NSA_TOPN) & (sc > -jnp.inf), 0.0, MASKED)

    q_rot = qrot_ref[0]
    gates_t = gates_ref[0].T
    lo_win = jnp.maximum(i - _window_chunks(NSA_WINDOW), 0)
    for j in range(kvh_n):
        heads = range(j * grp, (j + 1) * grp)
        bias_j = _bias_lanes(bias[j * HEAD_DIM:(j + 1) * HEAD_DIM])
        acc_s, _, l_s = _flash_t(_stack_queries(q_rot, heads, [bias_j] * grp), kslc_sc, j * LANES, vt_sc,
                                 j * HEAD_DIM, i, 0, None, False)
        acc_w, _, l_w = _flash_t(_stack_queries(q_rot, heads), kwin_sc, j * LANES, vt_sc,
                                 (kvh_n + j) * HEAD_DIM, i, lo_win, NSA_WINDOW, True)
        o_s = acc_s / l_s
        o_w = acc_w / l_w
        mixed = []
        for gi in range(grp):
            h = j * grp + gi
            cols = slice(gi * TQ, (gi + 1) * TQ)
            mixed.append(gates_t[3 * h:3 * h + 1] * ocmp_sc[h * HEAD_DIM:(h + 1) * HEAD_DIM, :]
                         + gates_t[3 * h + 1:3 * h + 2] * o_s[:, cols]
                         + gates_t[3 * h + 2:3 * h + 3] * o_w[:, cols])
        _store_heads(o_ref, jnp.concatenate(mixed, axis=1), j * grp, grp)


def _nsa_prompt(q_raw, q_rot, rows, win, v_t, kc, vc_t, gates):
    b, s, _ = q_raw.shape
    kw = NSA_KV_HEADS * HEAD_DIM
    n_chunks = s // TQ
    n_blocks = s // NSA_BLOCK
    assert s % TQ == 0 and n_blocks <= HEAD_DIM and kw == LANES
    grp = N_HEADS // NSA_KV_HEADS
    tile = lambda w: pl.BlockSpec((1, TQ, w), lambda a, i: (a, i, 0))
    per_b = lambda shape, lane_block=0: pl.BlockSpec((1,) + shape, lambda a, i: (a, 0, lane_block))
    return pl.pallas_call(
        functools.partial(_nsa_prompt_kernel, n_chunks=n_chunks, n_blocks=n_blocks),
        grid=(b, n_chunks),
        in_specs=[tile(Q_WIDTH), tile(Q_WIDTH),
                  per_b((s, kw), 2), per_b((s, kw), 0), per_b((2 * kw, s)),
                  per_b((LANES, LANES)), per_b((HEAD_DIM, LANES)), tile(LANES)],
        out_specs=tile(Q_WIDTH),
        out_shape=jax.ShapeDtypeStruct((b, s, Q_WIDTH), F32),
        scratch_shapes=[pltpu.VMEM((s, NSA_KV_HEADS * LANES), BF16),
                        pltpu.VMEM((s, NSA_KV_HEADS * LANES), BF16),
                        pltpu.VMEM((n_chunks, 2 * kw, TQ), BF16),
                        pltpu.VMEM((Q_WIDTH, TQ), F32)],
        compiler_params=_cparams(2),
        name="nsa_prompt",
    )(q_raw, q_rot, rows, win, v_t, kc, vc_t, gates)


def _fold_heads(o, kvh_n):
    grp = N_HEADS // kvh_n
    own = (_iota(o.shape, 1) // HEAD_DIM) == (_iota(o.shape, 0) // grp)
    t = jnp.where(own, o, 0.0)
    width = kvh_n * HEAD_DIM
    while width > HEAD_DIM:
        width //= 2
        t = t + pltpu.roll(t, width, 1)
    return t


def _rank_desc(vals, n_cand, n_io):
    rank = jnp.zeros(vals.shape, jnp.int32)
    for m_idx in range(n_cand):
        col = vals[:, m_idx:m_idx + 1]
        beats = (col > vals) | ((col == vals) & (m_idx < n_io))
        rank = rank + beats.astype(jnp.int32)
    return rank


def _moba_decode_kernel(pt_ref, *refs, n_pages):
    page_refs = refs[:n_pages]
    q_ref, new_ref, o_ref, s_sc, v_sc = refs[n_pages:]
    kw = MOBA_KV_HEADS * HEAD_DIM
    pages_per_block = MOBA_BLOCK // PAGE_SIZE
    n_blk = n_pages // pages_per_block
    qs = q_ref[0] * ATTN_SCALE
    qb = qs.astype(BF16)
    lane = _iota((N_HEADS, LANES), 1)
    gate = jnp.zeros((N_HEADS, LANES), F32)
    gsum = None
    for j in range(n_pages):
        s = _dot(qb, page_refs[j][0].reshape(kw, PAGE_SIZE).astype(BF16))
        s_sc[:, j * PAGE_SIZE:(j + 1) * PAGE_SIZE] = s
        v_sc[j] = page_refs[j][1].reshape(kw, PAGE_SIZE).astype(BF16)
        part = jnp.sum(s, axis=1, keepdims=True)
        gsum = part if j % pages_per_block == 0 else gsum + part
        if j % pages_per_block == pages_per_block - 1:
            gate = jnp.where(lane == j // pages_per_block, gsum, gate)
    eligible = lane < n_blk
    gate = jnp.where(eligible, gate, GATE_FLOOR)
    sel = eligible & (_rank_desc(gate, n_blk, lane) < MOBA_TOPK)
    bias = jnp.where(sel, 0.0, MASKED)
    new = new_ref[0]
    s_new = jnp.sum(qs * new[:, 0:kw], axis=1, keepdims=True)
    m = s_new
    for n in range(n_blk):
        cols = slice(n * MOBA_BLOCK, (n + 1) * MOBA_BLOCK)
        sb = s_sc[:, cols] + bias[:, n:n + 1]
        s_sc[:, cols] = sb
        m = jnp.maximum(m, jnp.max(sb, axis=1, keepdims=True))
    den = jnp.exp(s_new - m)
    o = den * new[:, kw:2 * kw]
    for j in range(n_pages):
        p = jnp.exp(s_sc[:, j * PAGE_SIZE:(j + 1) * PAGE_SIZE] - m)
        den = den + jnp.sum(p, axis=1, keepdims=True)
        o = o + _dot_nt(p.astype(BF16), v_sc[j])
    o_ref[0] = _fold_heads(o / den, MOBA_KV_HEADS)


def _moba_decode(pt, cache_t, qbd, new):
    db, n_pages = pt.shape
    kw = MOBA_KV_HEADS * HEAD_DIM
    assert (n_pages * PAGE_SIZE) % MOBA_BLOCK == 0 and n_pages * PAGE_SIZE // MOBA_BLOCK <= LANES
    page = lambda j: pl.BlockSpec((None, 2, MOBA_KV_HEADS, HEAD_DIM, PAGE_SIZE), lambda b, pt: (pt[b, j], 0, 0, 0, 0))
    grid_spec = pltpu.PrefetchScalarGridSpec(
        num_scalar_prefetch=1,
        grid=(db,),
        in_specs=[page(j) for j in range(n_pages)]
        + [pl.BlockSpec((1, N_HEADS, kw), lambda b, pt: (b, 0, 0)),
           pl.BlockSpec((1, 1, 2 * kw), lambda b, pt: (b, 0, 0))],
        out_specs=pl.BlockSpec((1, N_HEADS, kw), lambda b, pt: (b, 0, 0)),
        scratch_shapes=[pltpu.VMEM((N_HEADS, n_pages * PAGE_SIZE), F32),
                        pltpu.VMEM((n_pages, kw, PAGE_SIZE), BF16)],
    )
    return pl.pallas_call(
        functools.partial(_moba_decode_kernel, n_pages=n_pages),
        grid_spec=grid_spec,
        out_shape=jax.ShapeDtypeStruct((db, N_HEADS, kw), F32),
        compiler_params=_cparams(1),
        name="moba_decode",
    )(pt, *([cache_t] * n_pages), qbd, new)


def _swa_decode_kernel(buf_ref, new_ref, q_ref, sink_ref, o_ref):
    kw = SWA_KV_HEADS * HEAD_DIM
    wb = buf_ref.shape[-1]
    qs = q_ref[0] * ATTN_SCALE
    new = new_ref[0]
    s = _dot(qs.astype(BF16), buf_ref[0, 0].reshape(kw, wb).astype(BF16))
    s_new = jnp.sum(qs * new[:, 0:kw], axis=1, keepdims=True)
    sink = sink_ref[:, 0:1]
    m = jnp.maximum(jnp.maximum(jnp.max(s, axis=1, keepdims=True), s_new), sink)
    p = jnp.exp(s - m)
    p_new = jnp.exp(s_new - m)
    den = jnp.sum(p, axis=1, keepdims=True) + p_new + jnp.exp(sink - m)
    o = (_dot_nt(p.astype(BF16), buf_ref[0, 1].reshape(kw, wb).astype(BF16)) + p_new * new[:, kw:2 * kw]) / den
    o_ref[0] = _fold_heads(o, SWA_KV_HEADS)


def _swa_decode(buf_t, new, qbd, sink):
    db = buf_t.shape[0]
    wb = buf_t.shape[-1]
    kw = SWA_KV_HEADS * HEAD_DIM
    return pl.pallas_call(
        _swa_decode_kernel,
        grid=(db,),
        in_specs=[pl.BlockSpec((1, 2, SWA_KV_HEADS, HEAD_DIM, wb), lambda b: (b, 0, 0, 0, 0)),
                  pl.BlockSpec((1, 1, 2 * kw), lambda b: (b, 0, 0)),
                  pl.BlockSpec((1, N_HEADS, kw), lambda b: (b, 0, 0)),
                  pl.BlockSpec((N_HEADS, LANES), lambda b: (0, 0))],
        out_specs=pl.BlockSpec((1, N_HEADS, kw), lambda b: (b, 0, 0)),
        out_shape=jax.ShapeDtypeStruct((db, N_HEADS, kw), F32),
        compiler_params=_cparams(1),
        name="swa_decode",
    )(buf_t, new, qbd, sink)


def _nsa_decode_kernel(pt_ref, *refs, n_pages):
    page_refs = refs[:n_pages]
    (kcvc_ref, win_ref, qraw_ref, qrot_ref, newslc_ref, newwin_ref, gates_ref, o_ref, s_sc) = refs[n_pages:]
    kw = NSA_KV_HEADS * HEAD_DIM
    grp = N_HEADS // NSA_KV_HEADS
    blocks_per_page = PAGE_SIZE // NSA_BLOCK
    n_blk = n_pages * blocks_per_page
    wb = win_ref.shape[-1]
    q_raw = (qraw_ref[0] * ATTN_SCALE).astype(BF16)
    q_rot = qrot_ref[0] * ATTN_SCALE
    q_rot_b = q_rot.astype(BF16)

    kcvc = kcvc_ref[0]
    s_c = _dot_nt(q_raw, kcvc[:, 0:kw].astype(BF16))
    p_c = jnp.exp(s_c - jnp.max(s_c, axis=1, keepdims=True))
    p_c = p_c / jnp.sum(p_c, axis=1, keepdims=True)
    o_cmp = _dot(p_c.astype(BF16), kcvc[:, kw:2 * kw].astype(BF16))

    head = _iota(p_c.shape, 0)
    sc0 = jnp.sum(jnp.where(head < grp, p_c, 0.0), axis=0, keepdims=True)
    sc1 = jnp.sum(jnp.where(head >= grp, p_c, 0.0), axis=0, keepdims=True)
    r8 = _iota((SUBLANES, n_blk), 0)
    n_io = _iota((SUBLANES, n_blk), 1)
    sc = jnp.where(r8 == 0, sc0, jnp.where(r8 == 1, sc1, -jnp.inf))
    forced = (n_io == 0) | (n_io == n_blk - 1)
    sc = jnp.where(forced, jnp.inf, sc)
    sel = forced | (_rank_desc(sc, n_blk, n_io) < NSA_TOPN - 1)
    bias2 = jnp.where(sel, 0.0, MASKED)
    bias = jnp.where(head < grp, bias2[0:1], bias2[1:2])

    new_slc = newslc_ref[0]
    s_new = jnp.sum(q_rot * new_slc[:, 0:kw], axis=1, keepdims=True)
    m = s_new
    lane = _iota((N_HEADS, PAGE_SIZE), 1)
    k_all = jnp.concatenate([page_refs[j][0].reshape(kw, PAGE_SIZE).astype(BF16) for j in range(n_pages)], axis=1)
    v_all = jnp.concatenate([page_refs[j][1].reshape(kw, PAGE_SIZE).astype(BF16) for j in range(n_pages)], axis=1)
    s_all = _dot(q_rot_b, k_all)
    for j in range(n_pages):
        bj = bias[:, blocks_per_page * j:blocks_per_page * j + 1]
        for u in range(1, blocks_per_page):
            bj = jnp.where(lane < u * NSA_BLOCK, bj, bias[:, blocks_per_page * j + u:blocks_per_page * j + u + 1])
        s_sc[:, j * PAGE_SIZE:(j + 1) * PAGE_SIZE] = s_all[:, j * PAGE_SIZE:(j + 1) * PAGE_SIZE] + bj
    s_all = s_sc[...]
    m = jnp.maximum(m, jnp.max(s_all, axis=1, keepdims=True))
    p = jnp.exp(s_all - m)
    p_new = jnp.exp(s_new - m)
    den = jnp.sum(p, axis=1, keepdims=True) + p_new
    o_slc = (_dot_nt(p.astype(BF16), v_all) + p_new * new_slc[:, kw:2 * kw]) / den

    new_win = newwin_ref[0]
    s_w = _dot(q_rot_b, win_ref[0, 0].reshape(kw, wb).astype(BF16))
    s_wn = jnp.sum(q_rot * new_win[:, 0:kw], axis=1, keepdims=True)
    m_w = jnp.maximum(jnp.max(s_w, axis=1, keepdims=True), s_wn)
    p_w = jnp.exp(s_w - m_w)
    p_wn = jnp.exp(s_wn - m_w)
    den_w = jnp.sum(p_w, axis=1, keepdims=True) + p_wn
    o_win = (_dot_nt(p_w.astype(BF16), win_ref[0, 1].reshape(kw, wb).astype(BF16))
             + p_wn * new_win[:, kw:2 * kw]) / den_w

    g = gates_ref[0]
    o = g[:, 0:1] * o_cmp + g[:, 1:2] * o_slc + g[:, 2:3] * o_win
    o_ref[0] = _fold_heads(o, NSA_KV_HEADS)


def _nsa_decode(pt, cache_t, kcvc, win_t, q_raw_bd, q_rot_bd, new_rows, new_win, gates):
    db, n_pages = pt.shape
    kw = NSA_KV_HEADS * HEAD_DIM
    n_blk = kcvc.shape[1]
    wb = win_t.shape[-1]
    per_b = lambda shape: pl.BlockSpec((1,) + shape, lambda b, pt: (b,) + (0,) * len(shape))
    grid_spec = pltpu.PrefetchScalarGridSpec(
        num_scalar_prefetch=1,
        grid=(db,),
        in_specs=[_nsa_page_spec(j, 1) for j in range(n_pages)]
        + [per_b((n_blk, 2 * kw)), per_b((2, NSA_KV_HEADS, HEAD_DIM, wb)), per_b((N_HEADS, kw)), per_b((N_HEADS, kw)),
           pl.BlockSpec((1, 1, 2 * kw), lambda b, pt: (b, 0, 1)), per_b((1, 2 * kw)), per_b((N_HEADS, LANES))],
        out_specs=per_b((N_HEADS, kw)),
        scratch_shapes=[pltpu.VMEM((N_HEADS, n_pages * PAGE_SIZE), F32)],
    )
    return pl.pallas_call(
        functools.partial(_nsa_decode_kernel, n_pages=n_pages),
        grid_spec=grid_spec,
        out_shape=jax.ShapeDtypeStruct((db, N_HEADS, kw), F32),
        compiler_params=_cparams(1),
        name="nsa_decode",
    )(pt, *([cache_t] * n_pages), kcvc, win_t, q_raw_bd, q_rot_bd, new_rows, new_win, gates)


def _rope_tables(pos):
    half = HEAD_DIM // 2
    inv = ROPE_THETA ** (-jnp.arange(half, dtype=F32) / half)
    ang = pos.astype(F32)[:, None] * inv[None, :]
    cos = jnp.cos(ang)
    sin = jnp.sin(ang)
    reps = LANES // HEAD_DIM
    return jnp.tile(cos, (1, 2 * reps)), jnp.tile(jnp.concatenate([-sin, sin], 1), (1, reps))


def _block_diag_heads(q, kvh_n):
    r = q.shape[0]
    grp = N_HEADS // kvh_n
    own = (jnp.arange(N_HEADS)[:, None] // grp == jnp.arange(kvh_n)[None, :]).astype(q.dtype)
    q3 = q.reshape(r, N_HEADS, 1, HEAD_DIM) * own[None, :, :, None]
    return q3.reshape(r, N_HEADS, kvh_n * HEAD_DIM)


def _heads_from_folded(o):
    return o[:, :, :HEAD_DIM].reshape(1, o.shape[0], Q_WIDTH)


def _pad_cols(w, n):
    return jnp.pad(w, ((0, 0), (0, n - w.shape[1])))


def _block_diag(blocks):
    rows = sum(b.shape[0] for b in blocks)
    cols = sum(b.shape[1] for b in blocks)
    out = jnp.zeros((rows, cols), blocks[0].dtype)
    r = c = 0
    for b in blocks:
        out = out.at[r:r + b.shape[0], c:c + b.shape[1]].set(b)
        r += b.shape[0]
        c += b.shape[1]
    return out


def _positions_minor(x):
    nd = x.ndim
    return jnp.transpose(x, tuple(range(nd - 4)) + (nd - 3, nd - 2, nd - 1, nd - 4))


def _nsa_compress_weights_seq(pe_k, pe_v, w1_k, w2_k, w1_v, w2_v):
    pe2 = jnp.concatenate([pe_k, pe_k, pe_v, pe_v], axis=1)

    def pair(w1):
        w = w1.reshape(NSA_BLOCK, HEAD_DIM, NSA_CMP_HIDDEN)
        z = jnp.zeros_like(w)
        return jnp.concatenate([jnp.concatenate([w, z], 2), jnp.concatenate([z, w], 2)], 1).astype(BF16)

    w2 = _block_diag([w2_k, w2_k, w2_v, w2_v]).astype(BF16)
    return pe2, pair(w1_k), pair(w1_v), w2


def _nsa_compress_weights_pages(pe_k, pe_v, w1_k, w2_k, w1_v, w2_v):
    reps = PAGE_SIZE // NSA_BLOCK

    def pair(w1):
        w = w1.reshape(NSA_BLOCK, HEAD_DIM, NSA_CMP_HIDDEN).transpose(1, 0, 2)
        z = jnp.zeros_like(w)
        w = jnp.concatenate([jnp.concatenate([w, z], 2), jnp.concatenate([z, w], 2)], 1)
        return w.reshape(HEAD_DIM * PAGE_SIZE, -1).astype(BF16)

    return (jnp.tile(pe_k.T, (1, reps)), jnp.tile(pe_v.T, (1, reps)), pair(w1_k), pair(w1_v),
            _block_diag([w2_k] * reps).astype(BF16), _block_diag([w2_v] * reps).astype(BF16))


_MOBA_OUTS = (((0, Q_WIDTH, "rope"),),
              ((Q_WIDTH, 256, "rope"), (Q_WIDTH + 256, 256, "raw")))
_SWA_OUTS = (((0, Q_WIDTH, "rope"),),
             ((Q_WIDTH, 128, "rope"), (Q_WIDTH + 128, 128, "raw")))
_NSA_OUTS = (((0, Q_WIDTH, "raw"),),
             ((0, Q_WIDTH, "rope"),),
             ((Q_WIDTH, 128, "raw"), (Q_WIDTH + 128, 128, "raw"), (Q_WIDTH + 256, 128, "rope"),
              (Q_WIDTH + 384, 128, "raw")),
             ((Q_WIDTH + 512, 128, "rope"), (Q_WIDTH + 640, 128, "raw")),
             ((Q_WIDTH + 768, 128, "sigmoid"),))
_NSA_IN_PADDED = Q_WIDTH + 7 * 128


def kernel(x_prompt, x_sample, cache_moba_kv, state_swa_kv, cache_nsa_kv, state_nsa_win_kv, state_ffn_conv, page_table, c_prompt, c_sample, ada_w, ada_b, ln_g, ln_b, moba_w_in, moba_w_o, swa_w_in, swa_w_o, swa_sink, nsa_w_in, nsa_w_o, nsa_pe_k, nsa_pe_v, nsa_w1_k, nsa_w2_k, nsa_w1_v, nsa_w2_v, ffn_w_gate, ffn_w_up, ffn_conv_w, ffn_conv_b, ffn_w_down):
    b, s, d = x_prompt.shape
    db, dec_seq, _ = x_sample.shape
    depth = ada_w.shape[0]
    ff = ffn_w_gate.shape[2]
    n_pages = page_table.shape[1]
    n_phys = cache_moba_kv.shape[1]
    past_len = n_pages * PAGE_SIZE
    assert dec_seq == 1 and past_len % MOBA_BLOCK == 0 and s % 512 == 0
    assert state_swa_kv.shape[2] <= SWA_WINDOW and state_nsa_win_kv.shape[2] <= NSA_WINDOW
    alpha = (2 * depth) ** 0.25
    tm = 512
    tf = ff // 2

    rows = b + db
    rows_pad = -(-rows // SUBLANES) * SUBLANES
    c_all = jnp.concatenate([c_prompt, c_sample, jnp.zeros((rows_pad - rows, d), F32)], 0)
    mod = _adaln_all(c_all, ada_w.reshape(depth * 2, d, 3 * d), ada_b.reshape(depth * 2, 1, 3 * d))

    def modulation(i, sub):
        m = mod[i * 2 + sub]
        mp = m[:b].reshape(b, 1, 3 * d)
        ms = m[b:b + db].reshape(1, db, 3 * d)
        return ([mp[..., k * d:(k + 1) * d] for k in range(3)], [ms[..., k * d:(k + 1) * d] for k in range(3)])

    cos_p, sin_p = _rope_tables(jnp.arange(s, dtype=jnp.int32))
    cos_s, sin_s = _rope_tables(jnp.full((db,), past_len, jnp.int32))

    moba_cache_t = _positions_minor(cache_moba_kv).reshape(-1, 2, MOBA_KV_HEADS, HEAD_DIM, PAGE_SIZE)
    nsa_cache_t = _positions_minor(cache_nsa_kv).reshape(-1, 4, NSA_KV_HEADS, HEAD_DIM, PAGE_SIZE)

    xp = x_prompt
    xs = x_sample.reshape(1, db, d)
    moba_p, moba_s, swa_p, swa_s, nsa_p, nsa_s, nsaw_p, nsaw_s, conv_p, conv_s = ([] for _ in range(10))
    for i in range(depth):
        kind, j = i % N_MIXERS, i // N_MIXERS
        (sh_p, sc_p, gt_p), (sh_s, sc_s, gt_s) = modulation(i, 0)
        if kind == 0:
            w_in = moba_w_in[j].astype(BF16)
            q_p, kv_p, vt_p = _inproj(xp, sh_p, sc_p, w_in, cos_p, sin_p, _MOBA_OUTS, tm,
                                      w_in[:, Q_WIDTH + 256:Q_WIDTH + 512].T)
            q_s, kv_s = _inproj(xs, sh_s, sc_s, w_in, cos_s, sin_s, _MOBA_OUTS, db)
            o_p = _moba_prompt(q_p, kv_p, vt_p)
            o_s = _heads_from_folded(_moba_decode(page_table + j * n_phys, moba_cache_t,
                                                  _block_diag_heads(q_s[0], MOBA_KV_HEADS), kv_s.reshape(db, 1, -1)))
            moba_p.append(kv_p.reshape(b, s, 2, MOBA_KV_HEADS, HEAD_DIM))
            moba_s.append(kv_s.reshape(db, 1, 2, MOBA_KV_HEADS, HEAD_DIM))
            w_o = moba_w_o[j]
        elif kind == 1:
            w_in = swa_w_in[j].astype(BF16)
            q_p, kv_p, vt_p = _inproj(xp, sh_p, sc_p, w_in, cos_p, sin_p, _SWA_OUTS, tm,
                                      w_in[:, Q_WIDTH + 128:Q_WIDTH + 256].T)
            q_s, kv_s = _inproj(xs, sh_s, sc_s, w_in, cos_s, sin_s, _SWA_OUTS, db)
            sink = swa_sink[j]
            o_p = _swa_prompt(q_p, kv_p, vt_p, _pad_cols(sink[None, :], LANES))
            buf = state_swa_kv[j]
            new = kv_s.reshape(db, 1, -1)
            o_s = _heads_from_folded(_swa_decode(_positions_minor(buf), new, _block_diag_heads(q_s[0], SWA_KV_HEADS),
                                                 jnp.broadcast_to(sink[:, None], (N_HEADS, LANES))))
            swa_p.append(kv_p[:, s - min(SWA_WINDOW, s):].reshape(b, -1, 2, SWA_KV_HEADS, HEAD_DIM))
            swa_s.append(jnp.concatenate([buf[:, 1:], new.reshape(db, 1, 2, SWA_KV_HEADS, HEAD_DIM)], 1))
            w_o = swa_w_o[j]
        else:
            w_in = _pad_cols(nsa_w_in[j], _NSA_IN_PADDED).astype(BF16)
            nsa_w = (nsa_pe_k[j], nsa_pe_v[j], nsa_w1_k[j], nsa_w2_k[j], nsa_w1_v[j], nsa_w2_v[j])
            w_t = jnp.concatenate([w_in[:, Q_WIDTH + 384:Q_WIDTH + 512], w_in[:, Q_WIDTH + 640:Q_WIDTH + 768]], 1).T
            qraw_p, qrot_p, rows_p, win_p, gates_p, vt_p = _inproj(xp, sh_p, sc_p, w_in, cos_p, sin_p, _NSA_OUTS,
                                                                   tm, w_t)
            qraw_s, qrot_s, rows_s, win_s, gates_s = _inproj(xs, sh_s, sc_s, w_in, cos_s, sin_s, _NSA_OUTS, db)
            n_blk = s // NSA_BLOCK
            cmp_p = _nsa_compress_seq(rows_p.reshape(b * s, -1), *_nsa_compress_weights_seq(*nsa_w))
            cmp_p = cmp_p.reshape(b, n_blk, 2, NSA_KV_HEADS, HEAD_DIM).transpose(2, 0, 3, 1, 4)
            cmp_p = jnp.pad(cmp_p, ((0, 0), (0, 0), (0, 0), (0, HEAD_DIM - n_blk), (0, 0)))
            cmp_p = cmp_p.reshape(2, b, NSA_KV_HEADS * HEAD_DIM, HEAD_DIM)
            kc_p = jnp.pad(cmp_p[0], ((0, 0), (0, 0), (0, LANES - HEAD_DIM)))
            vct_p = cmp_p[1].transpose(0, 2, 1)
            o_p = _nsa_prompt(qraw_p, qrot_p, rows_p, win_p, vt_p, kc_p, vct_p, gates_p)
            pt = page_table + j * n_phys
            kcvc = _nsa_compress_pages(pt, nsa_cache_t, *_nsa_compress_weights_pages(*nsa_w))
            kcvc = kcvc.reshape(db, n_pages, NSA_KV_HEADS, 2, PAGE_SIZE // NSA_BLOCK, HEAD_DIM)
            kcvc = kcvc.transpose(0, 1, 4, 3, 2, 5).reshape(db, -1, 2 * NSA_KV_HEADS * HEAD_DIM)
            win_buf = state_nsa_win_kv[j]
            new_win = win_s.reshape(db, 1, -1)
            gates3 = jnp.pad(gates_s[0, :, :3 * N_HEADS].reshape(db, N_HEADS, 3), ((0, 0), (0, 0), (0, LANES - 3)))
            o_s = _heads_from_folded(_nsa_decode(pt, nsa_cache_t, kcvc, _positions_minor(win_buf),
                                                 _block_diag_heads(qraw_s[0], NSA_KV_HEADS),
                                                 _block_diag_heads(qrot_s[0], NSA_KV_HEADS),
                                                 rows_s.reshape(db, 1, -1), new_win, gates3))
            nsa_p.append(rows_p.reshape(b, s, 4, NSA_KV_HEADS, HEAD_DIM))
            nsa_s.append(rows_s.reshape(db, 1, 4, NSA_KV_HEADS, HEAD_DIM))
            nsaw_p.append(win_p[:, s - min(NSA_WINDOW, s):].reshape(b, -1, 2, NSA_KV_HEADS, HEAD_DIM))
            nsaw_s.append(jnp.concatenate([win_buf[:, 1:], new_win.reshape(db, 1, 2, NSA_KV_HEADS, HEAD_DIM)], 1))
            w_o = nsa_w_o[j]
        w_o = w_o.astype(BF16)
        g0, b0 = ln_g[i, 0][None, :], ln_b[i, 0][None, :]
        xp = _outproj_ln(o_p, w_o, xp, gt_p, g0, b0, alpha, tm)
        xs = _outproj_ln(o_s, w_o, xs, gt_s, g0, b0, alpha, db)

        (sh_p, sc_p, gt_p), (sh_s, sc_s, gt_s) = modulation(i, 1)
        wg, wu, wd = ffn_w_gate[i].astype(BF16), ffn_w_up[i].astype(BF16), ffn_w_down[i].astype(BF16)
        cwt, cbs = ffn_conv_w[i], ffn_conv_b[i][None, :]
        g1, b1 = ln_g[i, 1][None, :], ln_b[i, 1][None, :]
        xp, st_p = _ffn_seq(xp, sh_p, sc_p, gt_p, jnp.zeros((b, CONV_WIDTH - 1, ff), F32), wg, wu, cwt, cbs, wd,
                            g1, b1, alpha, tm)
        prev = state_ffn_conv[i]
        xs2, g_new = _ffn_step(xs[0], sh_s[0], sc_s[0], gt_s[0], prev[:, 0], prev[:, 1], wg, wu, cwt, cbs, wd,
                               g1, b1, alpha, tf)
        xs = xs2[None]
        conv_p.append(st_p)
        conv_s.append(jnp.stack([prev[:, 1], g_new], 1))
    return (xp, xs.reshape(db, 1, d), jnp.stack(moba_p), jnp.stack(moba_s), jnp.stack(swa_p), jnp.stack(swa_s),
            jnp.stack(nsa_p), jnp.stack(nsa_s), jnp.stack(nsaw_p), jnp.stack(nsaw_s),
            jnp.stack(conv_p), jnp.stack(conv_s))
```

```python
import functools

import jax
import jax.numpy as jnp
from jax import lax
from jax.experimental import pallas as pl
from jax.experimental.pallas import tpu as pltpu

F32 = jnp.float32
BF16 = jnp.bfloat16

HEAD_DIM = 64
N_HEADS = 16
Q_WIDTH = N_HEADS * HEAD_DIM
ROPE_THETA = 10000.0
N_MIXERS = 3
PAGE_SIZE = 128
MOBA_KV_HEADS = 4
MOBA_BLOCK = 256
MOBA_TOPK = 3
SWA_KV_HEADS = 2
SWA_WINDOW = 128
NSA_KV_HEADS = 2
NSA_BLOCK = 64
NSA_TOPN = 16
NSA_WINDOW = 512
NSA_CMP_HIDDEN = 128
CONV_WIDTH = 3
LN_EPS = 1e-5
ATTN_SCALE = HEAD_DIM ** -0.5
LOG2E = 1.4426950408889634

LANES = 128
SUBLANES = 8
TQ = 256
MASKED = -1e30
GATE_FLOOR = -3e38
MOBA_SPANS = (4, 2, 1)
FFN_HALO = 16
COMPRESS_GROUP = 8
TILE_PITCH = HEAD_DIM + SUBLANES
VMEM_LIMIT = 56 * 1024 * 1024


def _cparams(n_axes):
    return pltpu.CompilerParams(dimension_semantics=("arbitrary",) * n_axes,
                                vmem_limit_bytes=VMEM_LIMIT)


def _dot(a, b):
    return jnp.dot(a, b, preferred_element_type=F32)


def _dot_nt(a, b, precision=None):
    return lax.dot_general(a, b, (((1,), (1,)), ((), ())), precision=precision,
                           preferred_element_type=F32)


def _iota(shape, axis):
    return lax.broadcasted_iota(jnp.int32, shape, axis)


def _adaln_kernel(c_ref, w_ref, b_ref, o_ref):
    c = c_ref[...]
    a = c * (1.0 / (1.0 + jnp.exp(-c)))
    o_ref[0] = _dot(a.astype(BF16), w_ref[0].astype(BF16)) + b_ref[0]


def _adaln_all(c_all, ada_w, ada_b):
    n_sub, d, d3 = ada_w.shape
    rows = c_all.shape[0]
    tn = 1024
    return pl.pallas_call(
        _adaln_kernel,
        grid=(n_sub, d3 // tn),
        in_specs=[pl.BlockSpec((rows, d), lambda l, j: (0, 0)),
                  pl.BlockSpec((1, d, tn), lambda l, j: (l, 0, j)),
                  pl.BlockSpec((1, 1, tn), lambda l, j: (l, 0, j))],
        out_specs=pl.BlockSpec((1, rows, tn), lambda l, j: (l, 0, j)),
        out_shape=jax.ShapeDtypeStruct((n_sub, rows, d3), F32),
        compiler_params=_cparams(2),
        name="adaln",
    )(c_all, ada_w, ada_b)


def _rope_chunk(y, cos, sin_signed):
    lo = (_iota(y.shape, 1) % HEAD_DIM) < (HEAD_DIM // 2)
    swapped = jnp.where(lo, pltpu.roll(y, LANES - HEAD_DIM // 2, 1), pltpu.roll(y, HEAD_DIM // 2, 1))
    return y * cos + swapped * sin_signed


def _inproj_kernel(x_ref, sh_ref, sc_ref, w_ref, cos_ref, sin_ref, *rest, outs, has_t):
    h = (x_ref[0] * (1.0 + sc_ref[0]) + sh_ref[0]).astype(BF16)
    y = _dot(h, w_ref[...])
    cos = cos_ref[...]
    sin = sin_ref[...]
    out_refs = rest
    if has_t:
        wt_ref, out_refs, ot_ref = rest[0], rest[1:-1], rest[-1]
        ot_ref[0] = _dot_nt(wt_ref[...], h)
    for o_ref, segs in zip(out_refs, outs):
        dst = 0
        for (start, width, op) in segs:
            for c in range(0, width, LANES):
                blk = y[:, start + c:start + c + LANES]
                if op == "rope":
                    blk = _rope_chunk(blk, cos, sin)
                elif op == "sigmoid":
                    blk = 1.0 / (1.0 + jnp.exp(-blk))
                o_ref[0, :, dst + c:dst + c + LANES] = blk
            dst += width


def _inproj(x, shift, scale, w, cos, sin, outs, tm, w_t=None):
    g, r, d = x.shape
    n = w.shape[1]
    rm = shift.shape[1]
    mod_spec = (pl.BlockSpec((1, 1, d), lambda a, b: (a, 0, 0)) if rm == 1
                else pl.BlockSpec((1, tm, d), lambda a, b: (a, b, 0)))
    widths = [sum(s[1] for s in segs) for segs in outs]
    in_specs = [pl.BlockSpec((1, tm, d), lambda a, b: (a, b, 0)), mod_spec, mod_spec,
                pl.BlockSpec((d, n), lambda a, b: (0, 0)),
                pl.BlockSpec((tm, LANES), lambda a, b: (b, 0)),
                pl.BlockSpec((tm, LANES), lambda a, b: (b, 0))]
    out_specs = [pl.BlockSpec((1, tm, wd), lambda a, b: (a, b, 0)) for wd in widths]
    out_shape = [jax.ShapeDtypeStruct((g, r, wd), F32) for wd in widths]
    args = [x, shift, scale, w, cos, sin]
    if w_t is not None:
        nt = w_t.shape[0]
        in_specs.append(pl.BlockSpec((nt, d), lambda a, b: (0, 0)))
        out_specs.append(pl.BlockSpec((1, nt, tm), lambda a, b: (a, 0, b)))
        out_shape.append(jax.ShapeDtypeStruct((g, nt, r), F32))
        args.append(w_t)
    return pl.pallas_call(
        functools.partial(_inproj_kernel, outs=outs, has_t=w_t is not None),
        grid=(g, r // tm),
        in_specs=in_specs,
        out_specs=out_specs,
        out_shape=out_shape,
        compiler_params=_cparams(2),
        name="inproj",
    )(*args)


def _residual_ln(x, y, gate, g, b, alpha):
    z = alpha * x + (1.0 + gate) * y
    mu = jnp.mean(z, -1, keepdims=True)
    zc = z - mu
    var = jnp.mean(zc * zc, -1, keepdims=True)
    return zc * lax.rsqrt(var + LN_EPS) * g + b


def _outproj_kernel(o_ref, w_ref, x_ref, gt_ref, g_ref, b_ref, out_ref, *, alpha):
    y = _dot(o_ref[0].astype(BF16), w_ref[...])
    out_ref[0] = _residual_ln(x_ref[0], y, gt_ref[0], g_ref[...], b_ref[...], alpha)


def _outproj_ln(o, w, x, gate, ln_g, ln_b, alpha, tm):
    g, r, d = x.shape
    k = o.shape[2]
    rm = gate.shape[1]
    mod_spec = (pl.BlockSpec((1, 1, d), lambda a, b: (a, 0, 0)) if rm == 1
                else pl.BlockSpec((1, tm, d), lambda a, b: (a, b, 0)))
    return pl.pallas_call(
        functools.partial(_outproj_kernel, alpha=alpha),
        grid=(g, r // tm),
        in_specs=[pl.BlockSpec((1, tm, k), lambda a, b: (a, b, 0)),
                  pl.BlockSpec((k, d), lambda a, b: (0, 0)),
                  pl.BlockSpec((1, tm, d), lambda a, b: (a, b, 0)), mod_spec,
                  pl.BlockSpec((1, d), lambda a, b: (0, 0)),
                  pl.BlockSpec((1, d), lambda a, b: (0, 0))],
        out_specs=pl.BlockSpec((1, tm, d), lambda a, b: (a, b, 0)),
        out_shape=jax.ShapeDtypeStruct((g, r, d), F32),
        compiler_params=_cparams(2),
        name="outproj_ln",
    )(o, w, x, gate, ln_g, ln_b)


def _silu(x):
    return x * (1.0 / (1.0 + jnp.exp(-x)))


def _ffn_seq_kernel(x_ref, xh_ref, sh_ref, sc_ref, gt_ref, prev_ref, wg_ref, wu_ref, cw_ref, cb_ref,
                    wd_ref, g_ref, b_ref, out_ref, st_ref, *, alpha):
    r = pl.program_id(1)
    x = x_ref[0]
    tm = x.shape[0]
    he = (jnp.concatenate([xh_ref[0], x], axis=0) * (1.0 + sc_ref[0]) + sh_ref[0]).astype(BF16)
    h = he[FFN_HALO:]
    ge = _dot(he, wg_ref[...])
    gcur = ge[FFN_HALO:]
    first = r == 0
    prev = prev_ref[0]
    pm1 = jnp.where(first, prev[1:2], ge[FFN_HALO - 1:FFN_HALO])
    pm2 = jnp.where(first, prev[0:1], ge[FFN_HALO - 2:FFN_HALO - 1])
    row = _iota(gcur.shape, 0)
    g1 = jnp.where(row == 0, pm1, pltpu.roll(gcur, 1, 0))
    g2 = jnp.where(row == 0, pm2, jnp.where(row == 1, pm1, pltpu.roll(gcur, 2, 0)))
    cw = cw_ref[...]
    conv = cb_ref[...] + cw[0:1] * g2 + cw[1:2] * g1 + cw[2:3] * gcur
    act = _silu(conv) * _dot(h, wu_ref[...])
    y = _dot(act.astype(BF16), wd_ref[...])
    st_ref[0, 0] = gcur[tm - 2:tm]
    out_ref[0] = _residual_ln(x, y, gt_ref[0], g_ref[...], b_ref[...], alpha)


def _ffn_seq(x, shift, scale, gate, prev, wg, wu, cw, cb, wd, ln_g, ln_b, alpha, tm):
    g, r, d = x.shape
    ff = wg.shape[1]
    n_r = r // tm
    hb = tm // FFN_HALO
    mod = pl.BlockSpec((1, 1, d), lambda a, b: (a, 0, 0))
    const = lambda shape: pl.BlockSpec(shape, lambda a, b: (0, 0), pipeline_mode=pl.Buffered(1))
    out, st = pl.pallas_call(
        functools.partial(_ffn_seq_kernel, alpha=alpha),
        grid=(g, n_r),
        in_specs=[pl.BlockSpec((1, tm, d), lambda a, b: (a, b, 0)),
                  pl.BlockSpec((1, FFN_HALO, d), lambda a, b: (a, jnp.maximum(b * hb - 1, 0), 0)),
                  mod, mod, mod,
                  pl.BlockSpec((1, 2, ff), lambda a, b: (a, 0, 0)),
                  const((d, ff)), const((d, ff)), const((CONV_WIDTH, ff)), const((1, ff)), const((ff, d)),
                  const((1, d)), const((1, d))],
        out_specs=[pl.BlockSpec((1, tm, d), lambda a, b: (a, b, 0)),
                   pl.BlockSpec((1, 1, 2, ff), lambda a, b: (a, b, 0, 0))],
        out_shape=[jax.ShapeDtypeStruct((g, r, d), F32),
                   jax.ShapeDtypeStruct((g, n_r, 2, ff), F32)],
        compiler_params=_cparams(2),
        name="ffn_seq",
    )(x, x, shift, scale, gate, prev, wg, wu, cw, cb, wd, ln_g, ln_b)
    return out, st[:, n_r - 1]


def _ffn_step_kernel(x_ref, sh_ref, sc_ref, gt_ref, p0_ref, p1_ref, wg_ref, wu_ref, cw_ref, cb_ref,
                     wd_ref, g_ref, b_ref, out_ref, st_ref, acc_ref, *, alpha, n_f):
    f = pl.program_id(0)
    x = x_ref[...]
    h = (x * (1.0 + sc_ref[...]) + sh_ref[...]).astype(BF16)
    gcur = _dot(h, wg_ref[...])
    cw = cw_ref[...]
    conv = cb_ref[...] + cw[0:1] * p0_ref[...] + cw[1:2] * p1_ref[...] + cw[2:3] * gcur
    act = _silu(conv) * _dot(h, wu_ref[...])
    part = _dot(act.astype(BF16), wd_ref[...])
    st_ref[...] = gcur

    @pl.when(f == 0)
    def _():
        acc_ref[...] = part

    @pl.when(f != 0)
    def _():
        acc_ref[...] += part

    @pl.when(f == n_f - 1)
    def _():
        out_ref[...] = _residual_ln(x, acc_ref[...], gt_ref[...], g_ref[...], b_ref[...], alpha)


def _ffn_step(x, shift, scale, gate, p0, p1, wg, wu, cw, cb, wd, ln_g, ln_b, alpha, tf):
    r, d = x.shape
    ff = wg.shape[1]
    n_f = ff // tf
    full = pl.BlockSpec((r, d), lambda c: (0, 0))
    return pl.pallas_call(
        functools.partial(_ffn_step_kernel, alpha=alpha, n_f=n_f),
        grid=(n_f,),
        in_specs=[full, full, full, full,
                  pl.BlockSpec((r, tf), lambda c: (0, c)),
                  pl.BlockSpec((r, tf), lambda c: (0, c)),
                  pl.BlockSpec((d, tf), lambda c: (0, c)),
                  pl.BlockSpec((d, tf), lambda c: (0, c)),
                  pl.BlockSpec((CONV_WIDTH, tf), lambda c: (0, c)),
                  pl.BlockSpec((1, tf), lambda c: (0, c)),
                  pl.BlockSpec((tf, d), lambda c: (c, 0)),
                  pl.BlockSpec((1, d), lambda c: (0, 0)),
                  pl.BlockSpec((1, d), lambda c: (0, 0))],
        out_specs=[full, pl.BlockSpec((r, tf), lambda c: (0, c))],
        out_shape=[jax.ShapeDtypeStruct((r, d), F32), jax.ShapeDtypeStruct((r, ff), F32)],
        scratch_shapes=[pltpu.VMEM((r, d), F32)],
        compiler_params=_cparams(1),
        name="ffn_step",
    )(x, shift, scale, gate, p0, p1, wg, wu, cw, cb, wd, ln_g, ln_b)


def _head_slot(x, h):
    pair = x[:, (h // 2) * LANES:(h // 2 + 1) * LANES]
    return pltpu.roll(pair, HEAD_DIM, 1) if h % 2 else pair


def _slot(x, j, fill=0.0):
    lane = _iota((x.shape[0], LANES), 1)
    return jnp.where(lane < HEAD_DIM, _head_slot(x, j), fill)


def _flash_t(qg, k_ref, lane0, vt_ref, row0, i, lo, window, masked_past, spans=(2, 1)):
    rows = qg.shape[0]

    def scores(n_lo, span, masked):
        start = pl.multiple_of(n_lo * TQ, TQ)
        s = _dot_nt(k_ref[pl.ds(start, span * TQ), lane0:lane0 + LANES], qg)
        if masked:
            dist = (i - n_lo) * TQ + _iota(s.shape, 1) % TQ - _iota(s.shape, 0)
            ok = dist >= 0
            if window is not None:
                ok = ok & (dist <= window)
            s = jnp.where(ok, s, MASKED)
        return s

    def values(n_lo, span, p):
        vt = [vt_ref[n_lo + u, row0:row0 + HEAD_DIM, :] for u in range(span)]
        return _dot(vt[0] if span == 1 else jnp.concatenate(vt, axis=1), p.astype(BF16))

    s = scores(i, 1, True)
    m = jnp.max(s, axis=0, keepdims=True)
    p = jnp.exp2(s - m)
    l = jnp.sum(p, axis=0, keepdims=True)
    acc = values(i, 1, p)

    def step(n_lo, span, carry):
        m, l, acc = carry
        s = scores(n_lo, span, masked_past)
        m_new = jnp.maximum(m, jnp.max(s, axis=0, keepdims=True))
        a = jnp.exp2(m - m_new)
        p = jnp.exp2(s - m_new)
        l = a * l + jnp.sum(p, axis=0, keepdims=True)
        acc = a * acc + values(n_lo, span, p)
        return m_new, l, acc

    carry = (m, l, acc)
    top = i
    for span in spans:
        n_steps = (top - lo) // span
        carry = lax.fori_loop(0, n_steps, lambda t, c, top=top, span=span: step(top - (t + 1) * span, span, c),
                              carry)
        top = top - n_steps * span
    m, l, acc = carry
    return acc, m, l


def _store_heads(o_ref, out_t, head0, n_heads):
    for gp in range(n_heads // 2):
        pair = jnp.concatenate([out_t[:, (2 * gp) * TQ:(2 * gp + 1) * TQ],
                                out_t[:, (2 * gp + 1) * TQ:(2 * gp + 2) * TQ]], axis=0)
        c0 = ((head0 + 2 * gp) // 2) * LANES
        o_ref[0, :, c0:c0 + LANES] = pair.T


def _stack_queries(q, heads, bias=None):
    lane = _iota((TQ, LANES), 1)
    out = []
    for k, h in enumerate(heads):
        qh = jnp.where(lane < HEAD_DIM, _head_slot(q, h), 0.0) * (ATTN_SCALE * LOG2E)
        out.append((qh if bias is None else qh + bias[k]).astype(BF16))
    return jnp.concatenate(out, axis=0)


def _bias_lanes(bias_t):
    n = bias_t.shape[0]
    parts = [jnp.zeros((HEAD_DIM, TQ), F32), bias_t]
    if n < HEAD_DIM:
        parts.append(jnp.zeros((HEAD_DIM - n, TQ), F32))
    return jnp.concatenate(parts, axis=0).T


def _window_chunks(window):
    return -(-window // TQ)


def _moba_prompt_kernel(q_ref, kv_ref, vt_ref, o_ref, kslot_ref, vt_sc, kmean_ref, *, n_blocks):
    i = pl.program_id(1)
    kvh_n = MOBA_KV_HEADS
    grp = N_HEADS // kvh_n
    kw = kvh_n * HEAD_DIM

    @pl.when(i == 0)
    def _():
        for n in range(n_blocks):
            vt_sc[n] = vt_ref[0, :, n * TQ:(n + 1) * TQ].astype(BF16)
        blk_rows = _iota(kmean_ref.shape, 0)

        def prep(n, kmean):
            start = pl.multiple_of(n * TQ, TQ)
            kt = kv_ref[0, pl.ds(start, TQ), 0:kw]
            block_lane = (_iota((TQ, LANES), 1) - HEAD_DIM == n).astype(F32)
            means = []
            for j in range(kvh_n):
                kslot_ref[pl.ds(start, TQ), j * LANES:(j + 1) * LANES] = _slot(kt, j, block_lane).astype(BF16)
                means.append(jnp.mean(_slot(kt, j), axis=0, keepdims=True))
            return jnp.where(blk_rows == n, jnp.concatenate(means, axis=1), kmean)

        kmean_ref[...] = lax.fori_loop(0, n_blocks, prep, jnp.zeros(kmean_ref.shape, F32))

    q = q_ref[0]
    lane = _iota((TQ, LANES), 1)
    blk_io = _iota((kmean_ref.shape[0], TQ), 0)
    eligible = blk_io < i
    for j in range(kvh_n):
        kmj = kmean_ref[:, j * LANES:(j + 1) * LANES]
        bias = []
        for gi in range(grp):
            qh = jnp.where(lane < HEAD_DIM, _head_slot(q, j * grp + gi), 0.0)
            gate = _dot_nt(kmj, qh, precision=lax.Precision.HIGHEST)
            gate = jnp.where(eligible, gate, GATE_FLOOR)
            rank = jnp.zeros(gate.shape, jnp.int32)
            for m_idx in range(n_blocks):
                row = gate[m_idx:m_idx + 1, :]
                beats = (row > gate) | ((row == gate) & (m_idx < blk_io))
                rank = rank + beats.astype(jnp.int32)
            sel = (eligible & (rank < MOBA_TOPK)) | (blk_io == i)
            bias.append(_bias_lanes(jnp.where(sel, 0.0, MASKED)))
        qg = _stack_queries(q, range(j * grp, (j + 1) * grp), bias)
        acc, _, l = _flash_t(qg, kslot_ref, j * LANES, vt_sc, j * HEAD_DIM, i, 0, None, False, MOBA_SPANS)
        _store_heads(o_ref, acc / l, j * grp, grp)


def _moba_prompt(q, kv, v_t):
    b, s, _ = q.shape
    n_blocks = s // MOBA_BLOCK
    assert MOBA_BLOCK == TQ and s % TQ == 0 and n_blocks <= HEAD_DIM
    kw = MOBA_KV_HEADS * HEAD_DIM
    nbp = -(-n_blocks // SUBLANES) * SUBLANES
    return pl.pallas_call(
        functools.partial(_moba_prompt_kernel, n_blocks=n_blocks),
        grid=(b, s // TQ),
        in_specs=[pl.BlockSpec((1, TQ, Q_WIDTH), lambda a, i: (a, i, 0)),
                  pl.BlockSpec((1, s, 2 * kw), lambda a, i: (a, 0, 0)),
                  pl.BlockSpec((1, kw, s), lambda a, i: (a, 0, 0))],
        out_specs=pl.BlockSpec((1, TQ, Q_WIDTH), lambda a, i: (a, i, 0)),
        out_shape=jax.ShapeDtypeStruct((b, s, Q_WIDTH), F32),
        scratch_shapes=[pltpu.VMEM((s, MOBA_KV_HEADS * LANES), BF16),
                        pltpu.VMEM((n_blocks, kw, TQ), BF16),
                        pltpu.VMEM((nbp, MOBA_KV_HEADS * LANES), F32)],
        compiler_params=_cparams(2),
        name="moba_prompt",
    )(q, kv, v_t)


def _swa_prompt_kernel(q_ref, kv_ref, vt_ref, sink_ref, o_ref, kslot_ref, vt_sc, *, n_chunks):
    i = pl.program_id(1)
    kvh_n = SWA_KV_HEADS
    grp = N_HEADS // kvh_n
    kw = kvh_n * HEAD_DIM

    @pl.when(i == 0)
    def _():
        for n in range(n_chunks):
            vt_sc[n] = vt_ref[0, :, n * TQ:(n + 1) * TQ].astype(BF16)

        def prep(n, carry):
            start = pl.multiple_of(n * TQ, TQ)
            kt = kv_ref[0, pl.ds(start, TQ), 0:kw]
            for j in range(kvh_n):
                kslot_ref[pl.ds(start, TQ), j * LANES:(j + 1) * LANES] = _slot(kt, j).astype(BF16)
            return carry

        lax.fori_loop(0, n_chunks, prep, 0)

    q = q_ref[0]
    lo = jnp.maximum(i - _window_chunks(SWA_WINDOW), 0)
    for j in range(kvh_n):
        heads = range(j * grp, (j + 1) * grp)
        acc, m, l = _flash_t(_stack_queries(q, heads), kslot_ref, j * LANES, vt_sc, j * HEAD_DIM, i, lo,
                             SWA_WINDOW, True)
        sink = jnp.concatenate([jnp.broadcast_to(sink_ref[0:1, h:h + 1], (1, TQ)) for h in heads], axis=1) * LOG2E
        m_f = jnp.maximum(m, sink)
        a = jnp.exp2(m - m_f)
        den = l * a + jnp.exp2(sink - m_f)
        _store_heads(o_ref, acc * (a / den), j * grp, grp)


def _swa_prompt(q, kv, v_t, sink):
    b, s, _ = q.shape
    assert s % TQ == 0
    kw = SWA_KV_HEADS * HEAD_DIM
    return pl.pallas_call(
        functools.partial(_swa_prompt_kernel, n_chunks=s // TQ),
        grid=(b, s // TQ),
        in_specs=[pl.BlockSpec((1, TQ, Q_WIDTH), lambda a, i: (a, i, 0)),
                  pl.BlockSpec((1, s, 2 * kw), lambda a, i: (a, 0, 0)),
                  pl.BlockSpec((1, kw, s), lambda a, i: (a, 0, 0)),
                  pl.BlockSpec((1, LANES), lambda a, i: (0, 0))],
        out_specs=pl.BlockSpec((1, TQ, Q_WIDTH), lambda a, i: (a, i, 0)),
        out_shape=jax.ShapeDtypeStruct((b, s, Q_WIDTH), F32),
        scratch_shapes=[pltpu.VMEM((s, SWA_KV_HEADS * LANES), BF16),
                        pltpu.VMEM((s // TQ, kw, TQ), BF16)],
        compiler_params=_cparams(2),
        name="swa_prompt",
    )(q, kv, v_t, sink)


def _gelu_tanh(x):
    return x * (0.5 * (1.0 + jnp.tanh(0.7978845608028654 * (x + 0.044715 * (x * x * x)))))


def _compress_rows(xk_ref, xv_ref, n_blk, pe_ref, w1k_ref, w1v_ref, w2_ref):
    half = NSA_KV_HEADS * HEAD_DIM

    def body(p, carry):
        ak, av = carry
        pe = pe_ref[pl.ds(p, 1), :]
        xk = (xk_ref[pl.ds(p, n_blk, stride=NSA_BLOCK), :] + pe[:, 0:half]).astype(BF16)
        xv = (xv_ref[pl.ds(p, n_blk, stride=NSA_BLOCK), :] + pe[:, half:2 * half]).astype(BF16)
        ak = ak + _dot(xk, w1k_ref[p])
        av = av + _dot(xv, w1v_ref[p])
        return ak, av

    zero = jnp.zeros((n_blk, NSA_KV_HEADS * NSA_CMP_HIDDEN), F32)
    ak, av = lax.fori_loop(0, NSA_BLOCK, body, (zero, zero))
    hid = jnp.concatenate([_gelu_tanh(ak), _gelu_tanh(av)], axis=1).astype(BF16)
    return _dot(hid, w2_ref[...])


def _nsa_compress_seq_kernel(xk_ref, xv_ref, pe_ref, w1k_ref, w1v_ref, w2_ref, o_ref, *, n_blk):
    o_ref[...] = _compress_rows(xk_ref, xv_ref, n_blk, pe_ref, w1k_ref, w1v_ref, w2_ref)


def _nsa_compress_seq(rows2d, pe2, w1k, w1v, w2):
    t = rows2d.shape[0]
    n_blk = min(64, t // NSA_BLOCK)
    tr = n_blk * NSA_BLOCK
    half = 2 * NSA_KV_HEADS * HEAD_DIM
    full = lambda shape: pl.BlockSpec(shape, lambda j: (0,) * len(shape))
    return pl.pallas_call(
        functools.partial(_nsa_compress_seq_kernel, n_blk=n_blk),
        grid=(t // tr,),
        in_specs=[pl.BlockSpec((tr, LANES), lambda j: (j, 0)),
                  pl.BlockSpec((tr, LANES), lambda j: (j, 1)),
                  full(pe2.shape), full(w1k.shape), full(w1v.shape), full(w2.shape)],
        out_specs=pl.BlockSpec((n_blk, half), lambda j: (j, 0)),
        out_shape=jax.ShapeDtypeStruct((t // NSA_BLOCK, half), F32),
        compiler_params=_cparams(1),
        name="nsa_compress_seq",
    )(rows2d, rows2d, pe2, w1k, w1v, w2)


def _nsa_compress_pages_kernel(pt_ref, *refs, n_pages):
    page_refs = refs[:n_pages]
    pek_ref, pev_ref, w1k_ref, w1v_ref, w2k_ref, w2v_ref, o_ref, xk_sc, xv_sc = refs[n_pages:]
    for j in range(n_pages):
        for u in range(NSA_KV_HEADS):
            r0 = (j * NSA_KV_HEADS + u) * TILE_PITCH
            xk_sc[r0:r0 + HEAD_DIM, :] = page_refs[j][0, u]
            xv_sc[r0:r0 + HEAD_DIM, :] = page_refs[j][1, u]
    n_rows = n_pages * NSA_KV_HEADS

    def hidden(x_sc, pe_ref, w1_ref):
        acc = None
        for d0 in range(0, HEAD_DIM, COMPRESS_GROUP):
            x = jnp.concatenate(
                [(x_sc[pl.ds(d, n_rows, stride=TILE_PITCH), :] + pe_ref[d:d + 1, :]).astype(BF16)
                 for d in range(d0, d0 + COMPRESS_GROUP)], axis=1)
            part = _dot(x, w1_ref[d0 * PAGE_SIZE:(d0 + COMPRESS_GROUP) * PAGE_SIZE, :])
            acc = part if acc is None else acc + part
        return _gelu_tanh(acc).astype(BF16)

    kc = _dot(hidden(xk_sc, pek_ref, w1k_ref), w2k_ref[...])
    vc = _dot(hidden(xv_sc, pev_ref, w1v_ref), w2v_ref[...])
    o_ref[0] = jnp.concatenate([kc, vc], axis=1)


def _nsa_page_spec(j, comp_block):
    return pl.BlockSpec((None, 2, NSA_KV_HEADS, HEAD_DIM, PAGE_SIZE), lambda b, pt: (pt[b, j], comp_block, 0, 0, 0))


def _nsa_compress_pages(pt, cache_t, pek_t, pev_t, w1k_t, w1v_t, w2k, w2v):
    db, n_pages = pt.shape
    n_rows = n_pages * NSA_KV_HEADS
    full = lambda shape: pl.BlockSpec(shape, lambda b, pt: (0,) * len(shape))
    consts = (pek_t, pev_t, w1k_t, w1v_t, w2k, w2v)
    grid_spec = pltpu.PrefetchScalarGridSpec(
        num_scalar_prefetch=1,
        grid=(db,),
        in_specs=[_nsa_page_spec(j, 0) for j in range(n_pages)] + [full(c.shape) for c in consts],
        out_specs=pl.BlockSpec((1, n_rows, 2 * LANES), lambda b, pt: (b, 0, 0)),
        scratch_shapes=[pltpu.VMEM((n_rows * TILE_PITCH, PAGE_SIZE), F32),
                        pltpu.VMEM((n_rows * TILE_PITCH, PAGE_SIZE), F32)],
    )
    return pl.pallas_call(
        functools.partial(_nsa_compress_pages_kernel, n_pages=n_pages),
        grid_spec=grid_spec,
        out_shape=jax.ShapeDtypeStruct((db, n_rows, 2 * LANES), F32),
        compiler_params=_cparams(1),
        name="nsa_compress_pages",
    )(pt, *([cache_t] * n_pages), *consts)


def _nsa_prompt_kernel(qraw_ref, qrot_ref, ks_ref, kw_ref, vt_ref, kc_ref, vct_ref, gates_ref, o_ref,
                       kslc_sc, kwin_sc, vt_sc, ocmp_sc, *, n_chunks, n_blocks):
    i = pl.program_id(1)
    kvh_n = NSA_KV_HEADS
    grp = N_HEADS // kvh_n

    @pl.when(i == 0)
    def _():
        for n in range(n_chunks):
            vt_sc[n] = vt_ref[0, :, n * TQ:(n + 1) * TQ].astype(BF16)

        def prep(n, carry):
            start = pl.multiple_of(n * TQ, TQ)
            ks = ks_ref[0, pl.ds(start, TQ), :]
            kwn = kw_ref[0, pl.ds(start, TQ), :]
            blk = (n * TQ + _iota((TQ, LANES), 0)) // NSA_BLOCK
            block_lane = (_iota((TQ, LANES), 1) - HEAD_DIM == blk).astype(F32)
            for j in range(kvh_n):
                kslc_sc[pl.ds(start, TQ), j * LANES:(j + 1) * LANES] = _slot(ks, j, block_lane).astype(BF16)
                kwin_sc[pl.ds(start, TQ), j * LANES:(j + 1) * LANES] = _slot(kwn, j).astype(BF16)
            return carry

        lax.fori_loop(0, n_chunks, prep, 0)

    lane = _iota((TQ, LANES), 1)
    q_raw = qraw_ref[0]
    kcb = kc_ref[0].astype(BF16)
    vctb = vct_ref[0].astype(BF16)
    row_io = _iota((LANES, TQ), 0)
    n_io = row_io % HEAD_DIM
    t = i * TQ + _iota((LANES, TQ), 1)
    avail = ((n_io + 1) * NSA_BLOCK - 1 <= t) & (n_io < n_blocks)
    score = jnp.zeros((LANES, TQ), F32)
    for h in range(N_HEADS):
        qh = (jnp.where(lane < HEAD_DIM, _head_slot(q_raw, h), 0.0) * ATTN_SCALE).astype(BF16)
        ok = avail & ((row_io // HEAD_DIM) == (h // grp))
        s = jnp.where(ok, _dot_nt(kcb, qh), MASKED)
        m = jnp.max(s, axis=0, keepdims=True)
        p = jnp.exp(s - m) * ok.astype(F32)
        p = p / jnp.maximum(jnp.sum(p, axis=0, keepdims=True), 1e-30)
        score = score + p
        ocmp_sc[h * HEAD_DIM:(h + 1) * HEAD_DIM, :] = _dot(vctb, p.astype(BF16))

    own = t // NSA_BLOCK
    forced = ((n_io == 0) | (n_io == own) | (n_io == own - 1)) & (n_io < n_blocks)
    sc = jnp.where(avail, score, -jnp.inf)
    sc = jnp.where(forced, jnp.inf, sc)
    rank = jnp.zeros((LANES, TQ), jnp.int32)
    for m_idx in range(n_blocks):
        row = jnp.where(row_io < HEAD_DIM, sc[m_idx:m_idx + 1, :], sc[HEAD_DIM + m_idx:HEAD_DIM + m_idx + 1, :])
        beats = (row > sc) | ((row == sc) & (m_idx < n_io))
        rank = rank + beats.astype(jnp.int32)
    bias = jnp.where((rank < NSA_TOPN) & (sc > -jnp.inf), 0.0, MASKED)

    q_rot = qrot_ref[0]
    gates_t = gates_ref[0].T
    lo_win = jnp.maximum(i - _window_chunks(NSA_WINDOW), 0)
    for j in range(kvh_n):
        heads = range(j * grp, (j + 1) * grp)
        bias_j = _bias_lanes(bias[j * HEAD_DIM:(j + 1) * HEAD_DIM])
        acc_s, _, l_s = _flash_t(_stack_queries(q_rot, heads, [bias_j] * grp), kslc_sc, j * LANES, vt_sc,
                                 j * HEAD_DIM, i, 0, None, False)
        acc_w, _, l_w = _flash_t(_stack_queries(q_rot, heads), kwin_sc, j * LANES, vt_sc,
                                 (kvh_n + j) * HEAD_DIM, i, lo_win, NSA_WINDOW, True)
        o_s = acc_s / l_s
        o_w = acc_w / l_w
        mixed = []
        for gi in range(grp):
            h = j * grp + gi
            cols = slice(gi * TQ, (gi + 1) * TQ)
            mixed.append(gates_t[3 * h:3 * h + 1] * ocmp_sc[h * HEAD_DIM:(h + 1) * HEAD_DIM, :]
                         + gates_t[3 * h + 1:3 * h + 2] * o_s[:, cols]
                         + gates_t[3 * h + 2:3 * h + 3] * o_w[:, cols])
        _store_heads(o_ref, jnp.concatenate(mixed, axis=1), j * grp, grp)


def _nsa_prompt(q_raw, q_rot, rows, win, v_t, kc, vc_t, gates):
    b, s, _ = q_raw.shape
    kw = NSA_KV_HEADS * HEAD_DIM
    n_chunks = s // TQ
    n_blocks = s // NSA_BLOCK
    assert s % TQ == 0 and n_blocks <= HEAD_DIM and kw == LANES
    grp = N_HEADS // NSA_KV_HEADS
    tile = lambda w: pl.BlockSpec((1, TQ, w), lambda a, i: (a, i, 0))
    per_b = lambda shape, lane_block=0: pl.BlockSpec((1,) + shape, lambda a, i: (a, 0, lane_block))
    return pl.pallas_call(
        functools.partial(_nsa_prompt_kernel, n_chunks=n_chunks, n_blocks=n_blocks),
        grid=(b, n_chunks),
        in_specs=[tile(Q_WIDTH), tile(Q_WIDTH),
                  per_b((s, kw), 2), per_b((s, kw), 0), per_b((2 * kw, s)),
                  per_b((LANES, LANES)), per_b((HEAD_DIM, LANES)), tile(LANES)],
        out_specs=tile(Q_WIDTH),
        out_shape=jax.ShapeDtypeStruct((b, s, Q_WIDTH), F32),
        scratch_shapes=[pltpu.VMEM((s, NSA_KV_HEADS * LANES), BF16),
                        pltpu.VMEM((s, NSA_KV_HEADS * LANES), BF16),
                        pltpu.VMEM((n_chunks, 2 * kw, TQ), BF16),
                        pltpu.VMEM((Q_WIDTH, TQ), F32)],
        compiler_params=_cparams(2),
        name="nsa_prompt",
    )(q_raw, q_rot, rows, win, v_t, kc, vc_t, gates)


def _fold_heads(o, kvh_n):
    grp = N_HEADS // kvh_n
    own = (_iota(o.shape, 1) // HEAD_DIM) == (_iota(o.shape, 0) // grp)
    t = jnp.where(own, o, 0.0)
    width = kvh_n * HEAD_DIM
    while width > HEAD_DIM:
        width //= 2
        t = t + pltpu.roll(t, width, 1)
    return t


def _rank_desc(vals, n_cand, n_io):
    rank = jnp.zeros(vals.shape, jnp.int32)
    for m_idx in range(n_cand):
        col = vals[:, m_idx:m_idx + 1]
        beats = (col > vals) | ((col == vals) & (m_idx < n_io))
        rank = rank + beats.astype(jnp.int32)
    return rank


def _moba_decode_kernel(pt_ref, *refs, n_pages):
    page_refs = refs[:n_pages]
    q_ref, new_ref, o_ref, s_sc, v_sc = refs[n_pages:]
    kw = MOBA_KV_HEADS * HEAD_DIM
    pages_per_block = MOBA_BLOCK // PAGE_SIZE
    n_blk = n_pages // pages_per_block
    qs = q_ref[0] * ATTN_SCALE
    qb = qs.astype(BF16)
    lane = _iota((N_HEADS, LANES), 1)
    gate = jnp.zeros((N_HEADS, LANES), F32)
    gsum = None
    for j in range(n_pages):
        s = _dot(qb, page_refs[j][0].reshape(kw, PAGE_SIZE).astype(BF16))
        s_sc[:, j * PAGE_SIZE:(j + 1) * PAGE_SIZE] = s
        v_sc[j] = page_refs[j][1].reshape(kw, PAGE_SIZE).astype(BF16)
        part = jnp.sum(s, axis=1, keepdims=True)
        gsum = part if j % pages_per_block == 0 else gsum + part
        if j % pages_per_block == pages_per_block - 1:
            gate = jnp.where(lane == j // pages_per_block, gsum, gate)
    eligible = lane < n_blk
    gate = jnp.where(eligible, gate, GATE_FLOOR)
    sel = eligible & (_rank_desc(gate, n_blk, lane) < MOBA_TOPK)
    bias = jnp.where(sel, 0.0, MASKED)
    new = new_ref[0]
    s_new = jnp.sum(qs * new[:, 0:kw], axis=1, keepdims=True)
    m = s_new
    for n in range(n_blk):
        cols = slice(n * MOBA_BLOCK, (n + 1) * MOBA_BLOCK)
        sb = s_sc[:, cols] + bias[:, n:n + 1]
        s_sc[:, cols] = sb
        m = jnp.maximum(m, jnp.max(sb, axis=1, keepdims=True))
    den = jnp.exp(s_new - m)
    o = den * new[:, kw:2 * kw]
    for j in range(n_pages):
        p = jnp.exp(s_sc[:, j * PAGE_SIZE:(j + 1) * PAGE_SIZE] - m)
        den = den + jnp.sum(p, axis=1, keepdims=True)
        o = o + _dot_nt(p.astype(BF16), v_sc[j])
    o_ref[0] = _fold_heads(o / den, MOBA_KV_HEADS)


def _moba_decode(pt, cache_t, qbd, new):
    db, n_pages = pt.shape
    kw = MOBA_KV_HEADS * HEAD_DIM
    assert (n_pages * PAGE_SIZE) % MOBA_BLOCK == 0 and n_pages * PAGE_SIZE // MOBA_BLOCK <= LANES
    page = lambda j: pl.BlockSpec((None, 2, MOBA_KV_HEADS, HEAD_DIM, PAGE_SIZE), lambda b, pt: (pt[b, j], 0, 0, 0, 0))
    grid_spec = pltpu.PrefetchScalarGridSpec(
        num_scalar_prefetch=1,
        grid=(db,),
        in_specs=[page(j) for j in range(n_pages)]
        + [pl.BlockSpec((1, N_HEADS, kw), lambda b, pt: (b, 0, 0)),
           pl.BlockSpec((1, 1, 2 * kw), lambda b, pt: (b, 0, 0))],
        out_specs=pl.BlockSpec((1, N_HEADS, kw), lambda b, pt: (b, 0, 0)),
        scratch_shapes=[pltpu.VMEM((N_HEADS, n_pages * PAGE_SIZE), F32),
                        pltpu.VMEM((n_pages, kw, PAGE_SIZE), BF16)],
    )
    return pl.pallas_call(
        functools.partial(_moba_decode_kernel, n_pages=n_pages),
        grid_spec=grid_spec,
        out_shape=jax.ShapeDtypeStruct((db, N_HEADS, kw), F32),
        compiler_params=_cparams(1),
        name="moba_decode",
    )(pt, *([cache_t] * n_pages), qbd, new)


def _swa_decode_kernel(buf_ref, new_ref, q_ref, sink_ref, o_ref):
    kw = SWA_KV_HEADS * HEAD_DIM
    wb = buf_ref.shape[-1]
    qs = q_ref[0] * ATTN_SCALE
    new = new_ref[0]
    s = _dot(qs.astype(BF16), buf_ref[0, 0].reshape(kw, wb).astype(BF16))
    s_new = jnp.sum(qs * new[:, 0:kw], axis=1, keepdims=True)
    sink = sink_ref[:, 0:1]
    m = jnp.maximum(jnp.maximum(jnp.max(s, axis=1, keepdims=True), s_new), sink)
    p = jnp.exp(s - m)
    p_new = jnp.exp(s_new - m)
    den = jnp.sum(p, axis=1, keepdims=True) + p_new + jnp.exp(sink - m)
    o = (_dot_nt(p.astype(BF16), buf_ref[0, 1].reshape(kw, wb).astype(BF16)) + p_new * new[:, kw:2 * kw]) / den
    o_ref[0] = _fold_heads(o, SWA_KV_HEADS)


def _swa_decode(buf_t, new, qbd, sink):
    db = buf_t.shape[0]
    wb = buf_t.shape[-1]
    kw = SWA_KV_HEADS * HEAD_DIM
    return pl.pallas_call(
        _swa_decode_kernel,
        grid=(db,),
        in_specs=[pl.BlockSpec((1, 2, SWA_KV_HEADS, HEAD_DIM, wb), lambda b: (b, 0, 0, 0, 0)),
                  pl.BlockSpec((1, 1, 2 * kw), lambda b: (b, 0, 0)),
                  pl.BlockSpec((1, N_HEADS, kw), lambda b: (b, 0, 0)),
                  pl.BlockSpec((N_HEADS, LANES), lambda b: (0, 0))],
        out_specs=pl.BlockSpec((1, N_HEADS, kw), lambda b: (b, 0, 0)),
        out_shape=jax.ShapeDtypeStruct((db, N_HEADS, kw), F32),
        compiler_params=_cparams(1),
        name="swa_decode",
    )(buf_t, new, qbd, sink)


def _nsa_decode_kernel(pt_ref, *refs, n_pages):
    page_refs = refs[:n_pages]
    (kcvc_ref, win_ref, qraw_ref, qrot_ref, newslc_ref, newwin_ref, gates_ref, o_ref, s_sc) = refs[n_pages:]
    kw = NSA_KV_HEADS * HEAD_DIM
    grp = N_HEADS // NSA_KV_HEADS
    blocks_per_page = PAGE_SIZE // NSA_BLOCK
    n_blk = n_pages * blocks_per_page
    wb = win_ref.shape[-1]
    q_raw = (qraw_ref[0] * ATTN_SCALE).astype(BF16)
    q_rot = qrot_ref[0] * ATTN_SCALE
    q_rot_b = q_rot.astype(BF16)

    kcvc = kcvc_ref[0]
    s_c = _dot_nt(q_raw, kcvc[:, 0:kw].astype(BF16))
    p_c = jnp.exp(s_c - jnp.max(s_c, axis=1, keepdims=True))
    p_c = p_c / jnp.sum(p_c, axis=1, keepdims=True)
    o_cmp = _dot(p_c.astype(BF16), kcvc[:, kw:2 * kw].astype(BF16))

    head = _iota(p_c.shape, 0)
    sc0 = jnp.sum(jnp.where(head < grp, p_c, 0.0), axis=0, keepdims=True)
    sc1 = jnp.sum(jnp.where(head >= grp, p_c, 0.0), axis=0, keepdims=True)
    r8 = _iota((SUBLANES, n_blk), 0)
    n_io = _iota((SUBLANES, n_blk), 1)
    sc = jnp.where(r8 == 0, sc0, jnp.where(r8 == 1, sc1, -jnp.inf))
    forced = (n_io == 0) | (n_io == n_blk - 1)
    sc = jnp.where(forced, jnp.inf, sc)
    sel = forced | (_rank_desc(sc, n_blk, n_io) < NSA_TOPN - 1)
    bias2 = jnp.where(sel, 0.0, MASKED)
    bias = jnp.where(head < grp, bias2[0:1], bias2[1:2])

    new_slc = newslc_ref[0]
    s_new = jnp.sum(q_rot * new_slc[:, 0:kw], axis=1, keepdims=True)
    m = s_new
    lane = _iota((N_HEADS, PAGE_SIZE), 1)
    k_all = jnp.concatenate([page_refs[j][0].reshape(kw, PAGE_SIZE).astype(BF16) for j in range(n_pages)], axis=1)
    v_all = jnp.concatenate([page_refs[j][1].reshape(kw, PAGE_SIZE).astype(BF16) for j in range(n_pages)], axis=1)
    s_all = _dot(q_rot_b, k_all)
    for j in range(n_pages):
        bj = bias[:, blocks_per_page * j:blocks_per_page * j + 1]
        for u in range(1, blocks_per_page):
            bj = jnp.where(lane < u * NSA_BLOCK, bj, bias[:, blocks_per_page * j + u:blocks_per_page * j + u + 1])
        s_sc[:, j * PAGE_SIZE:(j + 1) * PAGE_SIZE] = s_all[:, j * PAGE_SIZE:(j + 1) * PAGE_SIZE] + bj
    s_all = s_sc[...]
    m = jnp.maximum(m, jnp.max(s_all, axis=1, keepdims=True))
    p = jnp.exp(s_all - m)
    p_new = jnp.exp(s_new - m)
    den = jnp.sum(p, axis=1, keepdims=True) + p_new
    o_slc = (_dot_nt(p.astype(BF16), v_all) + p_new * new_slc[:, kw:2 * kw]) / den

    new_win = newwin_ref[0]
    s_w = _dot(q_rot_b, win_ref[0, 0].reshape(kw, wb).astype(BF16))
    s_wn = jnp.sum(q_rot * new_win[:, 0:kw], axis=1, keepdims=True)
    m_w = jnp.maximum(jnp.max(s_w, axis=1, keepdims=True), s_wn)
    p_w = jnp.exp(s_w - m_w)
    p_wn = jnp.exp(s_wn - m_w)
    den_w = jnp.sum(p_w, axis=1, keepdims=True) + p_wn
    o_win = (_dot_nt(p_w.astype(BF16), win_ref[0, 1].reshape(kw, wb).astype(BF16))
             + p_wn * new_win[:, kw:2 * kw]) / den_w

    g = gates_ref[0]
    o = g[:, 0:1] * o_cmp + g[:, 1:2] * o_slc + g[:, 2:3] * o_win
    o_ref[0] = _fold_heads(o, NSA_KV_HEADS)


def _nsa_decode(pt, cache_t, kcvc, win_t, q_raw_bd, q_rot_bd, new_rows, new_win, gates):
    db, n_pages = pt.shape
    kw = NSA_KV_HEADS * HEAD_DIM
    n_blk = kcvc.shape[1]
    wb = win_t.shape[-1]
    per_b = lambda shape: pl.BlockSpec((1,) + shape, lambda b, pt: (b,) + (0,) * len(shape))
    grid_spec = pltpu.PrefetchScalarGridSpec(
        num_scalar_prefetch=1,
        grid=(db,),
        in_specs=[_nsa_page_spec(j, 1) for j in range(n_pages)]
        + [per_b((n_blk, 2 * kw)), per_b((2, NSA_KV_HEADS, HEAD_DIM, wb)), per_b((N_HEADS, kw)), per_b((N_HEADS, kw)),
           pl.BlockSpec((1, 1, 2 * kw), lambda b, pt: (b, 0, 1)), per_b((1, 2 * kw)), per_b((N_HEADS, LANES))],
        out_specs=per_b((N_HEADS, kw)),
        scratch_shapes=[pltpu.VMEM((N_HEADS, n_pages * PAGE_SIZE), F32)],
    )
    return pl.pallas_call(
        functools.partial(_nsa_decode_kernel, n_pages=n_pages),
        grid_spec=grid_spec,
        out_shape=jax.ShapeDtypeStruct((db, N_HEADS, kw), F32),
        compiler_params=_cparams(1),
        name="nsa_decode",
    )(pt, *([cache_t] * n_pages), kcvc, win_t, q_raw_bd, q_rot_bd, new_rows, new_win, gates)


def _rope_tables(pos):
    half = HEAD_DIM // 2
    inv = ROPE_THETA ** (-jnp.arange(half, dtype=F32) / half)
    ang = pos.astype(F32)[:, None] * inv[None, :]
    cos = jnp.cos(ang)
    sin = jnp.sin(ang)
    reps = LANES // HEAD_DIM
    return jnp.tile(cos, (1, 2 * reps)), jnp.tile(jnp.concatenate([-sin, sin], 1), (1, reps))


def _block_diag_heads(q, kvh_n):
    r = q.shape[0]
    grp = N_HEADS // kvh_n
    own = (jnp.arange(N_HEADS)[:, None] // grp == jnp.arange(kvh_n)[None, :]).astype(q.dtype)
    q3 = q.reshape(r, N_HEADS, 1, HEAD_DIM) * own[None, :, :, None]
    return q3.reshape(r, N_HEADS, kvh_n * HEAD_DIM)


def _heads_from_folded(o):
    return o[:, :, :HEAD_DIM].reshape(1, o.shape[0], Q_WIDTH)


def _pad_cols(w, n):
    return jnp.pad(w, ((0, 0), (0, n - w.shape[1])))


def _block_diag(blocks):
    rows = sum(b.shape[0] for b in blocks)
    cols = sum(b.shape[1] for b in blocks)
    out = jnp.zeros((rows, cols), blocks[0].dtype)
    r = c = 0
    for b in blocks:
        out = out.at[r:r + b.shape[0], c:c + b.shape[1]].set(b)
        r += b.shape[0]
        c += b.shape[1]
    return out


def _positions_minor(x):
    nd = x.ndim
    return jnp.transpose(x, tuple(range(nd - 4)) + (nd - 3, nd - 2, nd - 1, nd - 4))


def _nsa_compress_weights_seq(pe_k, pe_v, w1_k, w2_k, w1_v, w2_v):
    pe2 = jnp.concatenate([pe_k, pe_k, pe_v, pe_v], axis=1)

    def pair(w1):
        w = w1.reshape(NSA_BLOCK, HEAD_DIM, NSA_CMP_HIDDEN)
        z = jnp.zeros_like(w)
        return jnp.concatenate([jnp.concatenate([w, z], 2), jnp.concatenate([z, w], 2)], 1).astype(BF16)

    w2 = _block_diag([w2_k, w2_k, w2_v, w2_v]).astype(BF16)
    return pe2, pair(w1_k), pair(w1_v), w2


def _nsa_compress_weights_pages(pe_k, pe_v, w1_k, w2_k, w1_v, w2_v):
    reps = PAGE_SIZE // NSA_BLOCK

    def pair(w1):
        w = w1.reshape(NSA_BLOCK, HEAD_DIM, NSA_CMP_HIDDEN).transpose(1, 0, 2)
        z = jnp.zeros_like(w)
        w = jnp.concatenate([jnp.concatenate([w, z], 2), jnp.concatenate([z, w], 2)], 1)
        return w.reshape(HEAD_DIM * PAGE_SIZE, -1).astype(BF16)

    return (jnp.tile(pe_k.T, (1, reps)), jnp.tile(pe_v.T, (1, reps)), pair(w1_k), pair(w1_v),
            _block_diag([w2_k] * reps).astype(BF16), _block_diag([w2_v] * reps).astype(BF16))


_MOBA_OUTS = (((0, Q_WIDTH, "rope"),),
              ((Q_WIDTH, 256, "rope"), (Q_WIDTH + 256, 256, "raw")))
_SWA_OUTS = (((0, Q_WIDTH, "rope"),),
             ((Q_WIDTH, 128, "rope"), (Q_WIDTH + 128, 128, "raw")))
_NSA_OUTS = (((0, Q_WIDTH, "raw"),),
             ((0, Q_WIDTH, "rope"),),
             ((Q_WIDTH, 128, "raw"), (Q_WIDTH + 128, 128, "raw"), (Q_WIDTH + 256, 128, "rope"),
              (Q_WIDTH + 384, 128, "raw")),
             ((Q_WIDTH + 512, 128, "rope"), (Q_WIDTH + 640, 128, "raw")),
             ((Q_WIDTH + 768, 128, "sigmoid"),))
_NSA_IN_PADDED = Q_WIDTH + 7 * 128


def kernel(x_prompt, x_sample, cache_moba_kv, state_swa_kv, cache_nsa_kv, state_nsa_win_kv, state_ffn_conv, page_table, c_prompt, c_sample, ada_w, ada_b, ln_g, ln_b, moba_w_in, moba_w_o, swa_w_in, swa_w_o, swa_sink, nsa_w_in, nsa_w_o, nsa_pe_k, nsa_pe_v, nsa_w1_k, nsa_w2_k, nsa_w1_v, nsa_w2_v, ffn_w_gate, ffn_w_up, ffn_conv_w, ffn_conv_b, ffn_w_down):
    b, s, d = x_prompt.shape
    db, dec_seq, _ = x_sample.shape
    depth = ada_w.shape[0]
    ff = ffn_w_gate.shape[2]
    n_pages = page_table.shape[1]
    n_phys = cache_moba_kv.shape[1]
    past_len = n_pages * PAGE_SIZE
    assert dec_seq == 1 and past_len % MOBA_BLOCK == 0 and s % 512 == 0
    assert state_swa_kv.shape[2] <= SWA_WINDOW and state_nsa_win_kv.shape[2] <= NSA_WINDOW
    alpha = (2 * depth) ** 0.25
    tm = 512
    tf = ff // 2

    rows = b + db
    rows_pad = -(-rows // SUBLANES) * SUBLANES
    c_all = jnp.concatenate([c_prompt, c_sample, jnp.zeros((rows_pad - rows, d), F32)], 0)
    mod = _adaln_all(c_all, ada_w.reshape(depth * 2, d, 3 * d), ada_b.reshape(depth * 2, 1, 3 * d))

    def modulation(i, sub):
        m = mod[i * 2 + sub]
        mp = m[:b].reshape(b, 1, 3 * d)
        ms = m[b:b + db].reshape(1, db, 3 * d)
        return ([mp[..., k * d:(k + 1) * d] for k in range(3)], [ms[..., k * d:(k + 1) * d] for k in range(3)])

    cos_p, sin_p = _rope_tables(jnp.arange(s, dtype=jnp.int32))
    cos_s, sin_s = _rope_tables(jnp.full((db,), past_len, jnp.int32))

    moba_cache_t = _positions_minor(cache_moba_kv).reshape(-1, 2, MOBA_KV_HEADS, HEAD_DIM, PAGE_SIZE)
    nsa_cache_t = _positions_minor(cache_nsa_kv).reshape(-1, 4, NSA_KV_HEADS, HEAD_DIM, PAGE_SIZE)

    xp = x_prompt
    xs = x_sample.reshape(1, db, d)
    moba_p, moba_s, swa_p, swa_s, nsa_p, nsa_s, nsaw_p, nsaw_s, conv_p, conv_s = ([] for _ in range(10))
    for i in range(depth):
        kind, j = i % N_MIXERS, i // N_MIXERS
        (sh_p, sc_p, gt_p), (sh_s, sc_s, gt_s) = modulation(i, 0)
        if kind == 0:
            w_in = moba_w_in[j].astype(BF16)
            q_p, kv_p, vt_p = _inproj(xp, sh_p, sc_p, w_in, cos_p, sin_p, _MOBA_OUTS, tm,
                                      w_in[:, Q_WIDTH + 256:Q_WIDTH + 512].T)
            q_s, kv_s = _inproj(xs, sh_s, sc_s, w_in, cos_s, sin_s, _MOBA_OUTS, db)
            o_p = _moba_prompt(q_p, kv_p, vt_p)
            o_s = _heads_from_folded(_moba_decode(page_table + j * n_phys, moba_cache_t,
                                                  _block_diag_heads(q_s[0], MOBA_KV_HEADS), kv_s.reshape(db, 1, -1)))
            moba_p.append(kv_p.reshape(b, s, 2, MOBA_KV_HEADS, HEAD_DIM))
            moba_s.append(kv_s.reshape(db, 1, 2, MOBA_KV_HEADS, HEAD_DIM))
            w_o = moba_w_o[j]
        elif kind == 1:
            w_in = swa_w_in[j].astype(BF16)
            q_p, kv_p, vt_p = _inproj(xp, sh_p, sc_p, w_in, cos_p, sin_p, _SWA_OUTS, tm,
                                      w_in[:, Q_WIDTH + 128:Q_WIDTH + 256].T)
            q_s, kv_s = _inproj(xs, sh_s, sc_s, w_in, cos_s, sin_s, _SWA_OUTS, db)
            sink = swa_sink[j]
            o_p = _swa_prompt(q_p, kv_p, vt_p, _pad_cols(sink[None, :], LANES))
            buf = state_swa_kv[j]
            new = kv_s.reshape(db, 1, -1)
            o_s = _heads_from_folded(_swa_decode(_positions_minor(buf), new, _block_diag_heads(q_s[0], SWA_KV_HEADS),
                                                 jnp.broadcast_to(sink[:, None], (N_HEADS, LANES))))
            swa_p.append(kv_p[:, s - min(SWA_WINDOW, s):].reshape(b, -1, 2, SWA_KV_HEADS, HEAD_DIM))
            swa_s.append(jnp.concatenate([buf[:, 1:], new.reshape(db, 1, 2, SWA_KV_HEADS, HEAD_DIM)], 1))
            w_o = swa_w_o[j]
        else:
            w_in = _pad_cols(nsa_w_in[j], _NSA_IN_PADDED).astype(BF16)
            nsa_w = (nsa_pe_k[j], nsa_pe_v[j], nsa_w1_k[j], nsa_w2_k[j], nsa_w1_v[j], nsa_w2_v[j])
            w_t = jnp.concatenate([w_in[:, Q_WIDTH + 384:Q_WIDTH + 512], w_in[:, Q_WIDTH + 640:Q_WIDTH + 768]], 1).T
            qraw_p, qrot_p, rows_p, win_p, gates_p, vt_p = _inproj(xp, sh_p, sc_p, w_in, cos_p, sin_p, _NSA_OUTS,
                                                                   tm, w_t)
            qraw_s, qrot_s, rows_s, win_s, gates_s = _inproj(xs, sh_s, sc_s, w_in, cos_s, sin_s, _NSA_OUTS, db)
            n_blk = s // NSA_BLOCK
            cmp_p = _nsa_compress_seq(rows_p.reshape(b * s, -1), *_nsa_compress_weights_seq(*nsa_w))
            cmp_p = cmp_p.reshape(b, n_blk, 2, NSA_KV_HEADS, HEAD_DIM).transpose(2, 0, 3, 1, 4)
            cmp_p = jnp.pad(cmp_p, ((0, 0), (0, 0), (0, 0), (0, HEAD_DIM - n_blk), (0, 0)))
            cmp_p = cmp_p.reshape(2, b, NSA_KV_HEADS * HEAD_DIM, HEAD_DIM)
            kc_p = jnp.pad(cmp_p[0], ((0, 0), (0, 0), (0, LANES - HEAD_DIM)))
            vct_p = cmp_p[1].transpose(0, 2, 1)
            o_p = _nsa_prompt(qraw_p, qrot_p, rows_p, win_p, vt_p, kc_p, vct_p, gates_p)
            pt = page_table + j * n_phys
            kcvc = _nsa_compress_pages(pt, nsa_cache_t, *_nsa_compress_weights_pages(*nsa_w))
            kcvc = kcvc.reshape(db, n_pages, NSA_KV_HEADS, 2, PAGE_SIZE // NSA_BLOCK, HEAD_DIM)
            kcvc = kcvc.transpose(0, 1, 4, 3, 2, 5).reshape(db, -1, 2 * NSA_KV_HEADS * HEAD_DIM)
            win_buf = state_nsa_win_kv[j]
            new_win = win_s.reshape(db, 1, -1)
            gates3 = jnp.pad(gates_s[0, :, :3 * N_HEADS].reshape(db, N_HEADS, 3), ((0, 0), (0, 0), (0, LANES - 3)))
            o_s = _heads_from_folded(_nsa_decode(pt, nsa_cache_t, kcvc, _positions_minor(win_buf),
                                                 _block_diag_heads(qraw_s[0], NSA_KV_HEADS),
                                                 _block_diag_heads(qrot_s[0], NSA_KV_HEADS),
                                                 rows_s.reshape(db, 1, -1), new_win, gates3))
            nsa_p.append(rows_p.reshape(b, s, 4, NSA_KV_HEADS, HEAD_DIM))
            nsa_s.append(rows_s.reshape(db, 1, 4, NSA_KV_HEADS, HEAD_DIM))
            nsaw_p.append(win_p[:, s - min(NSA_WINDOW, s):].reshape(b, -1, 2, NSA_KV_HEADS, HEAD_DIM))
            nsaw_s.append(jnp.concatenate([win_buf[:, 1:], new_win.reshape(db, 1, 2, NSA_KV_HEADS, HEAD_DIM)], 1))
            w_o = nsa_w_o[j]
        w_o = w_o.astype(BF16)
        g0, b0 = ln_g[i, 0][None, :], ln_b[i, 0][None, :]
        xp = _outproj_ln(o_p, w_o, xp, gt_p, g0, b0, alpha, tm)
        xs = _outproj_ln(o_s, w_o, xs, gt_s, g0, b0, alpha, db)

        (sh_p, sc_p, gt_p), (sh_s, sc_s, gt_s) = modulation(i, 1)
        wg, wu, wd = ffn_w_gate[i].astype(BF16), ffn_w_up[i].astype(BF16), ffn_w_down[i].astype(BF16)
        cwt, cbs = ffn_conv_w[i], ffn_conv_b[i][None, :]
        g1, b1 = ln_g[i, 1][None, :], ln_b[i, 1][None, :]
        xp, st_p = _ffn_seq(xp, sh_p, sc_p, gt_p, jnp.zeros((b, CONV_WIDTH - 1, ff), F32), wg, wu, cwt, cbs, wd,
                            g1, b1, alpha, tm)
        prev = state_ffn_conv[i]
        xs2, g_new = _ffn_step(xs[0], sh_s[0], sc_s[0], gt_s[0], prev[:, 0], prev[:, 1], wg, wu, cwt, cbs, wd,
                               g1, b1, alpha, tf)
        xs = xs2[None]
        conv_p.append(st_p)
        conv_s.append(jnp.stack([prev[:, 1], g_new], 1))
    return (xp, xs.reshape(db, 1, d), jnp.stack(moba_p), jnp.stack(moba_s), jnp.stack(swa_p), jnp.stack(swa_s),
            jnp.stack(nsa_p), jnp.stack(nsa_s), jnp.stack(nsaw_p), jnp.stack(nsaw_s),
            jnp.stack(conv_p), jnp.stack(conv_s))
```

```python
import functools

import jax
import jax.numpy as jnp
from jax import lax
from jax.experimental import pallas as pl
from jax.experimental.pallas import tpu as pltpu

F32 = jnp.float32
BF16 = jnp.bfloat16

HEAD_DIM = 64
N_HEADS = 16
Q_WIDTH = N_HEADS * HEAD_DIM
ROPE_THETA = 10000.0
N_MIXERS = 3
PAGE_SIZE = 128
MOBA_KV_HEADS = 4
MOBA_BLOCK = 256
MOBA_TOPK = 3
SWA_KV_HEADS = 2
SWA_WINDOW = 128
NSA_KV_HEADS = 2
NSA_BLOCK = 64
NSA_TOPN = 16
NSA_WINDOW = 512
NSA_CMP_HIDDEN = 128
CONV_WIDTH = 3
LN_EPS = 1e-5
ATTN_SCALE = HEAD_DIM ** -0.5
LOG2E = 1.4426950408889634

LANES = 128
SUBLANES = 8
TQ = 256
MASKED = -1e30
GATE_FLOOR = -3e38
MOBA_SPANS = (4, 2, 1)
FFN_HALO = 16
COMPRESS_GROUP = 8
TILE_PITCH = HEAD_DIM + SUBLANES
VMEM_LIMIT = 56 * 1024 * 1024


def _cparams(n_axes):
    return pltpu.CompilerParams(dimension_semantics=("arbitrary",) * n_axes,
                                vmem_limit_bytes=VMEM_LIMIT)


def _dot(a, b):
    return jnp.dot(a, b, preferred_element_type=F32)


def _dot_nt(a, b, precision=None):
    return lax.dot_general(a, b, (((1,), (1,)), ((), ())), precision=precision,
                           preferred_element_type=F32)


def _iota(shape, axis):
    return lax.broadcasted_iota(jnp.int32, shape, axis)


def _adaln_kernel(c_ref, w_ref, b_ref, o_ref):
    c = c_ref[...]
    a = c * (1.0 / (1.0 + jnp.exp(-c)))
    o_ref[0] = _dot(a.astype(BF16), w_ref[0].astype(BF16)) + b_ref[0]


def _adaln_all(c_all, ada_w, ada_b):
    n_sub, d, d3 = ada_w.shape
    rows = c_all.shape[0]
    tn = 1024
    return pl.pallas_call(
        _adaln_kernel,
        grid=(n_sub, d3 // tn),
        in_specs=[pl.BlockSpec((rows, d), lambda l, j: (0, 0)),
                  pl.BlockSpec((1, d, tn), lambda l, j: (l, 0, j)),
                  pl.BlockSpec((1, 1, tn), lambda l, j: (l, 0, j))],
        out_specs=pl.BlockSpec((1, rows, tn), lambda l, j: (l, 0, j)),
        out_shape=jax.ShapeDtypeStruct((n_sub, rows, d3), F32),
        compiler_params=_cparams(2),
        name="adaln",
    )(c_all, ada_w, ada_b)


def _rope_chunk(y, cos, sin_signed):
    lo = (_iota(y.shape, 1) % HEAD_DIM) < (HEAD_DIM // 2)
    swapped = jnp.where(lo, pltpu.roll(y, LANES - HEAD_DIM // 2, 1), pltpu.roll(y, HEAD_DIM // 2, 1))
    return y * cos + swapped * sin_signed


def _inproj_kernel(x_ref, sh_ref, sc_ref, w_ref, cos_ref, sin_ref, *rest, outs, has_t):
    h = (x_ref[0] * (1.0 + sc_ref[0]) + sh_ref[0]).astype(BF16)
    y = _dot(h, w_ref[...])
    cos = cos_ref[...]
    sin = sin_ref[...]
    out_refs = rest
    if has_t:
        wt_ref, out_refs, ot_ref = rest[0], rest[1:-1], rest[-1]
        ot_ref[0] = _dot_nt(wt_ref[...], h)
    for o_ref, segs in zip(out_refs, outs):
        dst = 0
        for (start, width, op) in segs:
            for c in range(0, width, LANES):
                blk = y[:, start + c:start + c + LANES]
                if op == "rope":
                    blk = _rope_chunk(blk, cos, sin)
                elif op == "sigmoid":
                    blk = 1.0 / (1.0 + jnp.exp(-blk))
                o_ref[0, :, dst + c:dst + c + LANES] = blk
            dst += width


def _inproj(x, shift, scale, w, cos, sin, outs, tm, w_t=None):
    g, r, d = x.shape
    n = w.shape[1]
    rm = shift.shape[1]
    mod_spec = (pl.BlockSpec((1, 1, d), lambda a, b: (a, 0, 0)) if rm == 1
                else pl.BlockSpec((1, tm, d), lambda a, b: (a, b, 0)))
    widths = [sum(s[1] for s in segs) for segs in outs]
    in_specs = [pl.BlockSpec((1, tm, d), lambda a, b: (a, b, 0)), mod_spec, mod_spec,
                pl.BlockSpec((d, n), lambda a, b: (0, 0)),
                pl.BlockSpec((tm, LANES), lambda a, b: (b, 0)),
                pl.BlockSpec((tm, LANES), lambda a, b: (b, 0))]
    out_specs = [pl.BlockSpec((1, tm, wd), lambda a, b: (a, b, 0)) for wd in widths]
    out_shape = [jax.ShapeDtypeStruct((g, r, wd), F32) for wd in widths]
    args = [x, shift, scale, w, cos, sin]
    if w_t is not None:
        nt = w_t.shape[0]
        in_specs.append(pl.BlockSpec((nt, d), lambda a, b: (0, 0)))
        out_specs.append(pl.BlockSpec((1, nt, tm), lambda a, b: (a, 0, b)))
        out_shape.append(jax.ShapeDtypeStruct((g, nt, r), F32))
        args.append(w_t)
    return pl.pallas_call(
        functools.partial(_inproj_kernel, outs=outs, has_t=w_t is not None),
        grid=(g, r // tm),
        in_specs=in_specs,
        out_specs=out_specs,
        out_shape=out_shape,
        compiler_params=_cparams(2),
        name="inproj",
    )(*args)


def _residual_ln(x, y, gate, g, b, alpha):
    z = alpha * x + (1.0 + gate) * y
    mu = jnp.mean(z, -1, keepdims=True)
    zc = z - mu
    var = jnp.mean(zc * zc, -1, keepdims=True)
    return zc * lax.rsqrt(var + LN_EPS) * g + b


def _outproj_kernel(o_ref, w_ref, x_ref, gt_ref, g_ref, b_ref, out_ref, *, alpha):
    y = _dot(o_ref[0].astype(BF16), w_ref[...])
    out_ref[0] = _residual_ln(x_ref[0], y, gt_ref[0], g_ref[...], b_ref[...], alpha)


def _outproj_ln(o, w, x, gate, ln_g, ln_b, alpha, tm):
    g, r, d = x.shape
    k = o.shape[2]
    rm = gate.shape[1]
    mod_spec = (pl.BlockSpec((1, 1, d), lambda a, b: (a, 0, 0)) if rm == 1
                else pl.BlockSpec((1, tm, d), lambda a, b: (a, b, 0)))
    return pl.pallas_call(
        functools.partial(_outproj_kernel, alpha=alpha),
        grid=(g, r // tm),
        in_specs=[pl.BlockSpec((1, tm, k), lambda a, b: (a, b, 0)),
                  pl.BlockSpec((k, d), lambda a, b: (0, 0)),
                  pl.BlockSpec((1, tm, d), lambda a, b: (a, b, 0)), mod_spec,
                  pl.BlockSpec((1, d), lambda a, b: (0, 0)),
                  pl.BlockSpec((1, d), lambda a, b: (0, 0))],
        out_specs=pl.BlockSpec((1, tm, d), lambda a, b: (a, b, 0)),
        out_shape=jax.ShapeDtypeStruct((g, r, d), F32),
        compiler_params=_cparams(2),
        name="outproj_ln",
    )(o, w, x, gate, ln_g, ln_b)


def _silu(x):
    return x * (1.0 / (1.0 + jnp.exp(-x)))


def _ffn_seq_kernel(x_ref, xh_ref, sh_ref, sc_ref, gt_ref, prev_ref, wg_ref, wu_ref, cw_ref, cb_ref,
                    wd_ref, g_ref, b_ref, out_ref, st_ref, *, alpha):
    r = pl.program_id(1)
    x = x_ref[0]
    tm = x.shape[0]
    he = (jnp.concatenate([xh_ref[0], x], axis=0) * (1.0 + sc_ref[0]) + sh_ref[0]).astype(BF16)
    h = he[FFN_HALO:]
    ge = _dot(he, wg_ref[...])
    gcur = ge[FFN_HALO:]
    first = r == 0
    prev = prev_ref[0]
    pm1 = jnp.where(first, prev[1:2], ge[FFN_HALO - 1:FFN_HALO])
    pm2 = jnp.where(first, prev[0:1], ge[FFN_HALO - 2:FFN_HALO - 1])
    row = _iota(gcur.shape, 0)
    g1 = jnp.where(row == 0, pm1, pltpu.roll(gcur, 1, 0))
    g2 = jnp.where(row == 0, pm2, jnp.where(row == 1, pm1, pltpu.roll(gcur, 2, 0)))
    cw = cw_ref[...]
    conv = cb_ref[...] + cw[0:1] * g2 + cw[1:2] * g1 + cw[2:3] * gcur
    act = _silu(conv) * _dot(h, wu_ref[...])
    y = _dot(act.astype(BF16), wd_ref[...])
    st_ref[0, 0] = gcur[tm - 2:tm]
    out_ref[0] = _residual_ln(x, y, gt_ref[0], g_ref[...], b_ref[...], alpha)


def _ffn_seq(x, shift, scale, gate, prev, wg, wu, cw, cb, wd, ln_g, ln_b, alpha, tm):
    g, r, d = x.shape
    ff = wg.shape[1]
    n_r = r // tm
    hb = tm // FFN_HALO
    mod = pl.BlockSpec((1, 1, d), lambda a, b: (a, 0, 0))
    const = lambda shape: pl.BlockSpec(shape, lambda a, b: (0, 0), pipeline_mode=pl.Buffered(1))
    out, st = pl.pallas_call(
        functools.partial(_ffn_seq_kernel, alpha=alpha),
        grid=(g, n_r),
        in_specs=[pl.BlockSpec((1, tm, d), lambda a, b: (a, b, 0)),
                  pl.BlockSpec((1, FFN_HALO, d), lambda a, b: (a, jnp.maximum(b * hb - 1, 0), 0)),
                  mod, mod, mod,
                  pl.BlockSpec((1, 2, ff), lambda a, b: (a, 0, 0)),
                  const((d, ff)), const((d, ff)), const((CONV_WIDTH, ff)), const((1, ff)), const((ff, d)),
                  const((1, d)), const((1, d))],
        out_specs=[pl.BlockSpec((1, tm, d), lambda a, b: (a, b, 0)),
                   pl.BlockSpec((1, 1, 2, ff), lambda a, b: (a, b, 0, 0))],
        out_shape=[jax.ShapeDtypeStruct((g, r, d), F32),
                   jax.ShapeDtypeStruct((g, n_r, 2, ff), F32)],
        compiler_params=_cparams(2),
        name="ffn_seq",
    )(x, x, shift, scale, gate, prev, wg, wu, cw, cb, wd, ln_g, ln_b)
    return out, st[:, n_r - 1]


def _ffn_step_kernel(x_ref, sh_ref, sc_ref, gt_ref, p0_ref, p1_ref, wg_ref, wu_ref, cw_ref, cb_ref,
                     wd_ref, g_ref, b_ref, out_ref, st_ref, acc_ref, *, alpha, n_f):
    f = pl.program_id(0)
    x = x_ref[...]
    h = (x * (1.0 + sc_ref[...]) + sh_ref[...]).astype(BF16)
    gcur = _dot(h, wg_ref[...])
    cw = cw_ref[...]
    conv = cb_ref[...] + cw[0:1] * p0_ref[...] + cw[1:2] * p1_ref[...] + cw[2:3] * gcur
    act = _silu(conv) * _dot(h, wu_ref[...])
    part = _dot(act.astype(BF16), wd_ref[...])
    st_ref[...] = gcur

    @pl.when(f == 0)
    def _():
        acc_ref[...] = part

    @pl.when(f != 0)
    def _():
        acc_ref[...] += part

    @pl.when(f == n_f - 1)
    def _():
        out_ref[...] = _residual_ln(x, acc_ref[...], gt_ref[...], g_ref[...], b_ref[...], alpha)


def _ffn_step(x, shift, scale, gate, p0, p1, wg, wu, cw, cb, wd, ln_g, ln_b, alpha, tf):
    r, d = x.shape
    ff = wg.shape[1]
    n_f = ff // tf
    full = pl.BlockSpec((r, d), lambda c: (0, 0))
    return pl.pallas_call(
        functools.partial(_ffn_step_kernel, alpha=alpha, n_f=n_f),
        grid=(n_f,),
        in_specs=[full, full, full, full,
                  pl.BlockSpec((r, tf), lambda c: (0, c)),
                  pl.BlockSpec((r, tf), lambda c: (0, c)),
                  pl.BlockSpec((d, tf), lambda c: (0, c)),
                  pl.BlockSpec((d, tf), lambda c: (0, c)),
                  pl.BlockSpec((CONV_WIDTH, tf), lambda c: (0, c)),
                  pl.BlockSpec((1, tf), lambda c: (0, c)),
                  pl.BlockSpec((tf, d), lambda c: (c, 0)),
                  pl.BlockSpec((1, d), lambda c: (0, 0)),
                  pl.BlockSpec((1, d), lambda c: (0, 0))],
        out_specs=[full, pl.BlockSpec((r, tf), lambda c: (0, c))],
        out_shape=[jax.ShapeDtypeStruct((r, d), F32), jax.ShapeDtypeStruct((r, ff), F32)],
        scratch_shapes=[pltpu.VMEM((r, d), F32)],
        compiler_params=_cparams(1),
        name="ffn_step",
    )(x, shift, scale, gate, p0, p1, wg, wu, cw, cb, wd, ln_g, ln_b)


def _head_slot(x, h):
    pair = x[:, (h // 2) * LANES:(h // 2 + 1) * LANES]
    return pltpu.roll(pair, HEAD_DIM, 1) if h % 2 else pair


def _slot(x, j, fill=0.0):
    lane = _iota((x.shape[0], LANES), 1)
    return jnp.where(lane < HEAD_DIM, _head_slot(x, j), fill)


def _flash_t(qg, k_ref, lane0, vt_ref, row0, i, lo, window, masked_past, spans=(2, 1)):
    rows = qg.shape[0]

    def scores(n_lo, span, masked):
        start = pl.multiple_of(n_lo * TQ, TQ)
        s = _dot_nt(k_ref[pl.ds(start, span * TQ), lane0:lane0 + LANES], qg)
        if masked:
            dist = (i - n_lo) * TQ + _iota(s.shape, 1) % TQ - _iota(s.shape, 0)
            ok = dist >= 0
            if window is not None:
                ok = ok & (dist <= window)
            s = jnp.where(ok, s, MASKED)
        return s

    def values(n_lo, span, p):
        vt = [vt_ref[n_lo + u, row0:row0 + HEAD_DIM, :] for u in range(span)]
        return _dot(vt[0] if span == 1 else jnp.concatenate(vt, axis=1), p.astype(BF16))

    s = scores(i, 1, True)
    m = jnp.max(s, axis=0, keepdims=True)
    p = jnp.exp2(s - m)
    l = jnp.sum(p, axis=0, keepdims=True)
    acc = values(i, 1, p)

    def step(n_lo, span, carry):
        m, l, acc = carry
        s = scores(n_lo, span, masked_past)
        m_new = jnp.maximum(m, jnp.max(s, axis=0, keepdims=True))
        a = jnp.exp2(m - m_new)
        p = jnp.exp2(s - m_new)
        l = a * l + jnp.sum(p, axis=0, keepdims=True)
        acc = a * acc + values(n_lo, span, p)
        return m_new, l, acc

    carry = (m, l, acc)
    top = i
    for span in spans:
        n_steps = (top - lo) // span
        carry = lax.fori_loop(0, n_steps, lambda t, c, top=top, span=span: step(top - (t + 1) * span, span, c),
                              carry)
        top = top - n_steps * span
    m, l, acc = carry
    return acc, m, l


def _store_heads(o_ref, out_t, head0, n_heads):
    for gp in range(n_heads // 2):
        pair = jnp.concatenate([out_t[:, (2 * gp) * TQ:(2 * gp + 1) * TQ],
                                out_t[:, (2 * gp + 1) * TQ:(2 * gp + 2) * TQ]], axis=0)
        c0 = ((head0 + 2 * gp) // 2) * LANES
        o_ref[0, :, c0:c0 + LANES] = pair.T


def _stack_queries(q, heads, bias=None):
    lane = _iota((TQ, LANES), 1)
    out = []
    for k, h in enumerate(heads):
        qh = jnp.where(lane < HEAD_DIM, _head_slot(q, h), 0.0) * (ATTN_SCALE * LOG2E)
        out.append((qh if bias is None else qh + bias[k]).astype(BF16))
    return jnp.concatenate(out, axis=0)


def _bias_lanes(bias_t):
    n = bias_t.shape[0]
    parts = [jnp.zeros((HEAD_DIM, TQ), F32), bias_t]
    if n < HEAD_DIM:
        parts.append(jnp.zeros((HEAD_DIM - n, TQ), F32))
    return jnp.concatenate(parts, axis=0).T


def _window_chunks(window):
    return -(-window // TQ)


def _moba_prompt_kernel(q_ref, kv_ref, vt_ref, o_ref, kslot_ref, vt_sc, kmean_ref, *, n_blocks):
    i = pl.program_id(1)
    kvh_n = MOBA_KV_HEADS
    grp = N_HEADS // kvh_n
    kw = kvh_n * HEAD_DIM

    @pl.when(i == 0)
    def _():
        for n in range(n_blocks):
            vt_sc[n] = vt_ref[0, :, n * TQ:(n + 1) * TQ].astype(BF16)
        blk_rows = _iota(kmean_ref.shape, 0)

        def prep(n, kmean):
            start = pl.multiple_of(n * TQ, TQ)
            kt = kv_ref[0, pl.ds(start, TQ), 0:kw]
            block_lane = (_iota((TQ, LANES), 1) - HEAD_DIM == n).astype(F32)
            means = []
            for j in range(kvh_n):
                kslot_ref[pl.ds(start, TQ), j * LANES:(j + 1) * LANES] = _slot(kt, j, block_lane).astype(BF16)
                means.append(jnp.mean(_slot(kt, j), axis=0, keepdims=True))
            return jnp.where(blk_rows == n, jnp.concatenate(means, axis=1), kmean)

        kmean_ref[...] = lax.fori_loop(0, n_blocks, prep, jnp.zeros(kmean_ref.shape, F32))

    q = q_ref[0]
    lane = _iota((TQ, LANES), 1)
    blk_io = _iota((kmean_ref.shape[0], TQ), 0)
    eligible = blk_io < i
    for j in range(kvh_n):
        kmj = kmean_ref[:, j * LANES:(j + 1) * LANES]
        bias = []
        for gi in range(grp):
            qh = jnp.where(lane < HEAD_DIM, _head_slot(q, j * grp + gi), 0.0)
            gate = _dot_nt(kmj, qh, precision=lax.Precision.HIGHEST)
            gate = jnp.where(eligible, gate, GATE_FLOOR)
            rank = jnp.zeros(gate.shape, jnp.int32)
            for m_idx in range(n_blocks):
                row = gate[m_idx:m_idx + 1, :]
                beats = (row > gate) | ((row == gate) & (m_idx < blk_io))
                rank = rank + beats.astype(jnp.int32)
            sel = (eligible & (rank < MOBA_TOPK)) | (blk_io == i)
            bias.append(_bias_lanes(jnp.where(sel, 0.0, MASKED)))
        qg = _stack_queries(q, range(j * grp, (j + 1) * grp), bias)
        acc, _, l = _flash_t(qg, kslot_ref, j * LANES, vt_sc, j * HEAD_DIM, i, 0, None, False, MOBA_SPANS)
        _store_heads(o_ref, acc / l, j * grp, grp)


def _moba_prompt(q, kv, v_t):
    b, s, _ = q.shape
    n_blocks = s // MOBA_BLOCK
    assert MOBA_BLOCK == TQ and s % TQ == 0 and n_blocks <= HEAD_DIM
    kw = MOBA_KV_HEADS * HEAD_DIM
    nbp = -(-n_blocks // SUBLANES) * SUBLANES
    return pl.pallas_call(
        functools.partial(_moba_prompt_kernel, n_blocks=n_blocks),
        grid=(b, s // TQ),
        in_specs=[pl.BlockSpec((1, TQ, Q_WIDTH), lambda a, i: (a, i, 0)),
                  pl.BlockSpec((1, s, 2 * kw), lambda a, i: (a, 0, 0)),
                  pl.BlockSpec((1, kw, s), lambda a, i: (a, 0, 0))],
        out_specs=pl.BlockSpec((1, TQ, Q_WIDTH), lambda a, i: (a, i, 0)),
        out_shape=jax.ShapeDtypeStruct((b, s, Q_WIDTH), F32),
        scratch_shapes=[pltpu.VMEM((s, MOBA_KV_HEADS * LANES), BF16),
                        pltpu.VMEM((n_blocks, kw, TQ), BF16),
                        pltpu.VMEM((nbp, MOBA_KV_HEADS * LANES), F32)],
        compiler_params=_cparams(2),
        name="moba_prompt",
    )(q, kv, v_t)


def _swa_prompt_kernel(q_ref, kv_ref, vt_ref, sink_ref, o_ref, kslot_ref, vt_sc, *, n_chunks):
    i = pl.program_id(1)
    kvh_n = SWA_KV_HEADS
    grp = N_HEADS // kvh_n
    kw = kvh_n * HEAD_DIM

    @pl.when(i == 0)
    def _():
        for n in range(n_chunks):
            vt_sc[n] = vt_ref[0, :, n * TQ:(n + 1) * TQ].astype(BF16)

        def prep(n, carry):
            start = pl.multiple_of(n * TQ, TQ)
            kt = kv_ref[0, pl.ds(start, TQ), 0:kw]
            for j in range(kvh_n):
                kslot_ref[pl.ds(start, TQ), j * LANES:(j + 1) * LANES] = _slot(kt, j).astype(BF16)
            return carry

        lax.fori_loop(0, n_chunks, prep, 0)

    q = q_ref[0]
    lo = jnp.maximum(i - _window_chunks(SWA_WINDOW), 0)
    for j in range(kvh_n):
        heads = range(j * grp, (j + 1) * grp)
        acc, m, l = _flash_t(_stack_queries(q, heads), kslot_ref, j * LANES, vt_sc, j * HEAD_DIM, i, lo,
                             SWA_WINDOW, True)
        sink = jnp.concatenate([jnp.broadcast_to(sink_ref[0:1, h:h + 1], (1, TQ)) for h in heads], axis=1) * LOG2E
        m_f = jnp.maximum(m, sink)
        a = jnp.exp2(m - m_f)
        den = l * a + jnp.exp2(sink - m_f)
        _store_heads(o_ref, acc * (a / den), j * grp, grp)


def _swa_prompt(q, kv, v_t, sink):
    b, s, _ = q.shape
    assert s % TQ == 0
    kw = SWA_KV_HEADS * HEAD_DIM
    return pl.pallas_call(
        functools.partial(_swa_prompt_kernel, n_chunks=s // TQ),
        grid=(b, s // TQ),
        in_specs=[pl.BlockSpec((1, TQ, Q_WIDTH), lambda a, i: (a, i, 0)),
                  pl.BlockSpec((1, s, 2 * kw), lambda a, i: (a, 0, 0)),
                  pl.BlockSpec((1, kw, s), lambda a, i: (a, 0, 0)),
                  pl.BlockSpec((1, LANES), lambda a, i: (0, 0))],
        out_specs=pl.BlockSpec((1, TQ, Q_WIDTH), lambda a, i: (a, i, 0)),
        out_shape=jax.ShapeDtypeStruct((b, s, Q_WIDTH), F32),
        scratch_shapes=[pltpu.VMEM((s, SWA_KV_HEADS * LANES), BF16),
                        pltpu.VMEM((s // TQ, kw, TQ), BF16)],
        compiler_params=_cparams(2),
        name="swa_prompt",
    )(q, kv, v_t, sink)


def _gelu_tanh(x):
    return x * (0.5 * (1.0 + jnp.tanh(0.7978845608028654 * (x + 0.044715 * (x * x * x)))))


def _compress_rows(xk_ref, xv_ref, n_blk, pe_ref, w1k_ref, w1v_ref, w2_ref):
    half = NSA_KV_HEADS * HEAD_DIM

    def body(p, carry):
        ak, av = carry
        pe = pe_ref[pl.ds(p, 1), :]
        xk = (xk_ref[pl.ds(p, n_blk, stride=NSA_BLOCK), :] + pe[:, 0:half]).astype(BF16)
        xv = (xv_ref[pl.ds(p, n_blk, stride=NSA_BLOCK), :] + pe[:, half:2 * half]).astype(BF16)
        ak = ak + _dot(xk, w1k_ref[p])
        av = av + _dot(xv, w1v_ref[p])
        return ak, av

    zero = jnp.zeros((n_blk, NSA_KV_HEADS * NSA_CMP_HIDDEN), F32)
    ak, av = lax.fori_loop(0, NSA_BLOCK, body, (zero, zero))
    hid = jnp.concatenate([_gelu_tanh(ak), _gelu_tanh(av)], axis=1).astype(BF16)
    return _dot(hid, w2_ref[...])


def _nsa_compress_seq_kernel(xk_ref, xv_ref, pe_ref, w1k_ref, w1v_ref, w2_ref, o_ref, *, n_blk):
    o_ref[...] = _compress_rows(xk_ref, xv_ref, n_blk, pe_ref, w1k_ref, w1v_ref, w2_ref)


def _nsa_compress_seq(rows2d, pe2, w1k, w1v, w2):
    t = rows2d.shape[0]
    n_blk = min(64, t // NSA_BLOCK)
    tr = n_blk * NSA_BLOCK
    half = 2 * NSA_KV_HEADS * HEAD_DIM
    full = lambda shape: pl.BlockSpec(shape, lambda j: (0,) * len(shape))
    return pl.pallas_call(
        functools.partial(_nsa_compress_seq_kernel, n_blk=n_blk),
        grid=(t // tr,),
        in_specs=[pl.BlockSpec((tr, LANES), lambda j: (j, 0)),
                  pl.BlockSpec((tr, LANES), lambda j: (j, 1)),
                  full(pe2.shape), full(w1k.shape), full(w1v.shape), full(w2.shape)],
        out_specs=pl.BlockSpec((n_blk, half), lambda j: (j, 0)),
        out_shape=jax.ShapeDtypeStruct((t // NSA_BLOCK, half), F32),
        compiler_params=_cparams(1),
        name="nsa_compress_seq",
    )(rows2d, rows2d, pe2, w1k, w1v, w2)


def _nsa_compress_pages_kernel(pt_ref, *refs, n_pages):
    page_refs = refs[:n_pages]
    pek_ref, pev_ref, w1k_ref, w1v_ref, w2k_ref, w2v_ref, o_ref, xk_sc, xv_sc = refs[n_pages:]
    for j in range(n_pages):
        for u in range(NSA_KV_HEADS):
            r0 = (j * NSA_KV_HEADS + u) * TILE_PITCH
            xk_sc[r0:r0 + HEAD_DIM, :] = page_refs[j][0, u]
            xv_sc[r0:r0 + HEAD_DIM, :] = page_refs[j][1, u]
    n_rows = n_pages * NSA_KV_HEADS

    def hidden(x_sc, pe_ref, w1_ref):
        acc = None
        for d0 in range(0, HEAD_DIM, COMPRESS_GROUP):
            x = jnp.concatenate(
                [(x_sc[pl.ds(d, n_rows, stride=TILE_PITCH), :] + pe_ref[d:d + 1, :]).astype(BF16)
                 for d in range(d0, d0 + COMPRESS_GROUP)], axis=1)
            part = _dot(x, w1_ref[d0 * PAGE_SIZE:(d0 + COMPRESS_GROUP) * PAGE_SIZE, :])
            acc = part if acc is None else acc + part
        return _gelu_tanh(acc).astype(BF16)

    kc = _dot(hidden(xk_sc, pek_ref, w1k_ref), w2k_ref[...])
    vc = _dot(hidden(xv_sc, pev_ref, w1v_ref), w2v_ref[...])
    o_ref[0] = jnp.concatenate([kc, vc], axis=1)


def _nsa_page_spec(j, comp_block):
    return pl.BlockSpec((None, 2, NSA_KV_HEADS, HEAD_DIM, PAGE_SIZE), lambda b, pt: (pt[b, j], comp_block, 0, 0, 0))


def _nsa_compress_pages(pt, cache_t, pek_t, pev_t, w1k_t, w1v_t, w2k, w2v):
    db, n_pages = pt.shape
    n_rows = n_pages * NSA_KV_HEADS
    full = lambda shape: pl.BlockSpec(shape, lambda b, pt: (0,) * len(shape))
    consts = (pek_t, pev_t, w1k_t, w1v_t, w2k, w2v)
    grid_spec = pltpu.PrefetchScalarGridSpec(
        num_scalar_prefetch=1,
        grid=(db,),
        in_specs=[_nsa_page_spec(j, 0) for j in range(n_pages)] + [full(c.shape) for c in consts],
        out_specs=pl.BlockSpec((1, n_rows, 2 * LANES), lambda b, pt: (b, 0, 0)),
        scratch_shapes=[pltpu.VMEM((n_rows * TILE_PITCH, PAGE_SIZE), F32),
                        pltpu.VMEM((n_rows * TILE_PITCH, PAGE_SIZE), F32)],
    )
    return pl.pallas_call(
        functools.partial(_nsa_compress_pages_kernel, n_pages=n_pages),
        grid_spec=grid_spec,
        out_shape=jax.ShapeDtypeStruct((db, n_rows, 2 * LANES), F32),
        compiler_params=_cparams(1),
        name="nsa_compress_pages",
    )(pt, *([cache_t] * n_pages), *consts)


def _nsa_prompt_kernel(qraw_ref, qrot_ref, ks_ref, kw_ref, vt_ref, kc_ref, vct_ref, gates_ref, o_ref,
                       kslc_sc, kwin_sc, vt_sc, ocmp_sc, *, n_chunks, n_blocks):
    i = pl.program_id(1)
    kvh_n = NSA_KV_HEADS
    grp = N_HEADS // kvh_n

    @pl.when(i == 0)
    def _():
        for n in range(n_chunks):
            vt_sc[n] = vt_ref[0, :, n * TQ:(n + 1) * TQ].astype(BF16)

        def prep(n, carry):
            start = pl.multiple_of(n * TQ, TQ)
            ks = ks_ref[0, pl.ds(start, TQ), :]
            kwn = kw_ref[0, pl.ds(start, TQ), :]
            blk = (n * TQ + _iota((TQ, LANES), 0)) // NSA_BLOCK
            block_lane = (_iota((TQ, LANES), 1) - HEAD_DIM == blk).astype(F32)
            for j in range(kvh_n):
                kslc_sc[pl.ds(start, TQ), j * LANES:(j + 1) * LANES] = _slot(ks, j, block_lane).astype(BF16)
                kwin_sc[pl.ds(start, TQ), j * LANES:(j + 1) * LANES] = _slot(kwn, j).astype(BF16)
            return carry

        lax.fori_loop(0, n_chunks, prep, 0)

    lane = _iota((TQ, LANES), 1)
    q_raw = qraw_ref[0]
    kcb = kc_ref[0].astype(BF16)
    vctb = vct_ref[0].astype(BF16)
    row_io = _iota((LANES, TQ), 0)
    n_io = row_io % HEAD_DIM
    t = i * TQ + _iota((LANES, TQ), 1)
    avail = ((n_io + 1) * NSA_BLOCK - 1 <= t) & (n_io < n_blocks)
    score = jnp.zeros((LANES, TQ), F32)
    for h in range(N_HEADS):
        qh = (jnp.where(lane < HEAD_DIM, _head_slot(q_raw, h), 0.0) * ATTN_SCALE).astype(BF16)
        ok = avail & ((row_io // HEAD_DIM) == (h // grp))
        s = jnp.where(ok, _dot_nt(kcb, qh), MASKED)
        m = jnp.max(s, axis=0, keepdims=True)
        p = jnp.exp(s - m) * ok.astype(F32)
        p = p / jnp.maximum(jnp.sum(p, axis=0, keepdims=True), 1e-30)
        score = score + p
        ocmp_sc[h * HEAD_DIM:(h + 1) * HEAD_DIM, :] = _dot(vctb, p.astype(BF16))

    own = t // NSA_BLOCK
    forced = ((n_io == 0) | (n_io == own) | (n_io == own - 1)) & (n_io < n_blocks)
    sc = jnp.where(avail, score, -jnp.inf)
    sc = jnp.where(forced, jnp.inf, sc)
    rank = jnp.zeros((LANES, TQ), jnp.int32)
    for m_idx in range(n_blocks):
        row = jnp.where(row_io < HEAD_DIM, sc[m_idx:m_idx + 1, :], sc[HEAD_DIM + m_idx:HEAD_DIM + m_idx + 1, :])
        beats = (row > sc) | ((row == sc) & (m_idx < n_io))
        rank = rank + beats.astype(jnp.int32)
    bias = jnp.where((rank < NSA_TOPN) & (sc > -jnp.inf), 0.0, MASKED)

    q_rot = qrot_ref[0]
    gates_t = gates_ref[0].T
    lo_win = jnp.maximum(i - _window_chunks(NSA_WINDOW), 0)
    for j in range(kvh_n):
        heads = range(j * grp, (j + 1) * grp)
        bias_j = _bias_lanes(bias[j * HEAD_DIM:(j + 1) * HEAD_DIM])
        acc_s, _, l_s = _flash_t(_stack_queries(q_rot, heads, [bias_j] * grp), kslc_sc, j * LANES, vt_sc,
                                 j * HEAD_DIM, i, 0, None, False, MOBA_SPANS)
        acc_w, _, l_w = _flash_t(_stack_queries(q_rot, heads), kwin_sc, j * LANES, vt_sc,
                                 (kvh_n + j) * HEAD_DIM, i, lo_win, NSA_WINDOW, True)
        o_s = acc_s / l_s
        o_w = acc_w / l_w
        mixed = []
        for gi in range(grp):
            h = j * grp + gi
            cols = slice(gi * TQ, (gi + 1) * TQ)
            mixed.append(gates_t[3 * h:3 * h + 1] * ocmp_sc[h * HEAD_DIM:(h + 1) * HEAD_DIM, :]
                         + gates_t[3 * h + 1:3 * h + 2] * o_s[:, cols]
                         + gates_t[3 * h + 2:3 * h + 3] * o_w[:, cols])
        _store_heads(o_ref, jnp.concatenate(mixed, axis=1), j * grp, grp)


def _nsa_prompt(q_raw, q_rot, rows, win, v_t, kc, vc_t, gates):
    b, s, _ = q_raw.shape
    kw = NSA_KV_HEADS * HEAD_DIM
    n_chunks = s // TQ
    n_blocks = s // NSA_BLOCK
    assert s % TQ == 0 and n_blocks <= HEAD_DIM and kw == LANES
    grp = N_HEADS // NSA_KV_HEADS
    tile = lambda w: pl.BlockSpec((1, TQ, w), lambda a, i: (a, i, 0))
    per_b = lambda shape, lane_block=0: pl.BlockSpec((1,) + shape, lambda a, i: (a, 0, lane_block))
    return pl.pallas_call(
        functools.partial(_nsa_prompt_kernel, n_chunks=n_chunks, n_blocks=n_blocks),
        grid=(b, n_chunks),
        in_specs=[tile(Q_WIDTH), tile(Q_WIDTH),
                  per_b((s, kw), 2), per_b((s, kw), 0), per_b((2 * kw, s)),
                  per_b((LANES, LANES)), per_b((HEAD_DIM, LANES)), tile(LANES)],
        out_specs=tile(Q_WIDTH),
        out_shape=jax.ShapeDtypeStruct((b, s, Q_WIDTH), F32),
        scratch_shapes=[pltpu.VMEM((s, NSA_KV_HEADS * LANES), BF16),
                        pltpu.VMEM((s, NSA_KV_HEADS * LANES), BF16),
                        pltpu.VMEM((n_chunks, 2 * kw, TQ), BF16),
                        pltpu.VMEM((Q_WIDTH, TQ), F32)],
        compiler_params=_cparams(2),
        name="nsa_prompt",
    )(q_raw, q_rot, rows, win, v_t, kc, vc_t, gates)


def _fold_heads(o, kvh_n):
    grp = N_HEADS // kvh_n
    own = (_iota(o.shape, 1) // HEAD_DIM) == (_iota(o.shape, 0) // grp)
    t = jnp.where(own, o, 0.0)
    width = kvh_n * HEAD_DIM
    while width > HEAD_DIM:
        width //= 2
        t = t + pltpu.roll(t, width, 1)
    return t


def _rank_desc(vals, n_cand, n_io):
    rank = jnp.zeros(vals.shape, jnp.int32)
    for m_idx in range(n_cand):
        col = vals[:, m_idx:m_idx + 1]
        beats = (col > vals) | ((col == vals) & (m_idx < n_io))
        rank = rank + beats.astype(jnp.int32)
    return rank


def _moba_decode_kernel(pt_ref, *refs, n_pages):
    page_refs = refs[:n_pages]
    q_ref, new_ref, o_ref, s_sc, v_sc = refs[n_pages:]
    kw = MOBA_KV_HEADS * HEAD_DIM
    pages_per_block = MOBA_BLOCK // PAGE_SIZE
    n_blk = n_pages // pages_per_block
    qs = q_ref[0] * ATTN_SCALE
    qb = qs.astype(BF16)
    lane = _iota((N_HEADS, LANES), 1)
    gate = jnp.zeros((N_HEADS, LANES), F32)
    gsum = None
    for j in range(n_pages):
        s = _dot(qb, page_refs[j][0].reshape(kw, PAGE_SIZE).astype(BF16))
        s_sc[:, j * PAGE_SIZE:(j + 1) * PAGE_SIZE] = s
        v_sc[j] = page_refs[j][1].reshape(kw, PAGE_SIZE).astype(BF16)
        part = jnp.sum(s, axis=1, keepdims=True)
        gsum = part if j % pages_per_block == 0 else gsum + part
        if j % pages_per_block == pages_per_block - 1:
            gate = jnp.where(lane == j // pages_per_block, gsum, gate)
    eligible = lane < n_blk
    gate = jnp.where(eligible, gate, GATE_FLOOR)
    sel = eligible & (_rank_desc(gate, n_blk, lane) < MOBA_TOPK)
    bias = jnp.where(sel, 0.0, MASKED)
    new = new_ref[0]
    s_new = jnp.sum(qs * new[:, 0:kw], axis=1, keepdims=True)
    m = s_new
    for n in range(n_blk):
        cols = slice(n * MOBA_BLOCK, (n + 1) * MOBA_BLOCK)
        sb = s_sc[:, cols] + bias[:, n:n + 1]
        s_sc[:, cols] = sb
        m = jnp.maximum(m, jnp.max(sb, axis=1, keepdims=True))
    den = jnp.exp(s_new - m)
    o = den * new[:, kw:2 * kw]
    for j in range(n_pages):
        p = jnp.exp(s_sc[:, j * PAGE_SIZE:(j + 1) * PAGE_SIZE] - m)
        den = den + jnp.sum(p, axis=1, keepdims=True)
        o = o + _dot_nt(p.astype(BF16), v_sc[j])
    o_ref[0] = _fold_heads(o / den, MOBA_KV_HEADS)


def _moba_decode(pt, cache_t, qbd, new):
    db, n_pages = pt.shape
    kw = MOBA_KV_HEADS * HEAD_DIM
    assert (n_pages * PAGE_SIZE) % MOBA_BLOCK == 0 and n_pages * PAGE_SIZE // MOBA_BLOCK <= LANES
    page = lambda j: pl.BlockSpec((None, 2, MOBA_KV_HEADS, HEAD_DIM, PAGE_SIZE), lambda b, pt: (pt[b, j], 0, 0, 0, 0))
    grid_spec = pltpu.PrefetchScalarGridSpec(
        num_scalar_prefetch=1,
        grid=(db,),
        in_specs=[page(j) for j in range(n_pages)]
        + [pl.BlockSpec((1, N_HEADS, kw), lambda b, pt: (b, 0, 0)),
           pl.BlockSpec((1, 1, 2 * kw), lambda b, pt: (b, 0, 0))],
        out_specs=pl.BlockSpec((1, N_HEADS, kw), lambda b, pt: (b, 0, 0)),
        scratch_shapes=[pltpu.VMEM((N_HEADS, n_pages * PAGE_SIZE), F32),
                        pltpu.VMEM((n_pages, kw, PAGE_SIZE), BF16)],
    )
    return pl.pallas_call(
        functools.partial(_moba_decode_kernel, n_pages=n_pages),
        grid_spec=grid_spec,
        out_shape=jax.ShapeDtypeStruct((db, N_HEADS, kw), F32),
        compiler_params=_cparams(1),
        name="moba_decode",
    )(pt, *([cache_t] * n_pages), qbd, new)


def _swa_decode_kernel(buf_ref, new_ref, q_ref, sink_ref, o_ref):
    kw = SWA_KV_HEADS * HEAD_DIM
    wb = buf_ref.shape[-1]
    qs = q_ref[0] * ATTN_SCALE
    new = new_ref[0]
    s = _dot(qs.astype(BF16), buf_ref[0, 0].reshape(kw, wb).astype(BF16))
    s_new = jnp.sum(qs * new[:, 0:kw], axis=1, keepdims=True)
    sink = sink_ref[:, 0:1]
    m = jnp.maximum(jnp.maximum(jnp.max(s, axis=1, keepdims=True), s_new), sink)
    p = jnp.exp(s - m)
    p_new = jnp.exp(s_new - m)
    den = jnp.sum(p, axis=1, keepdims=True) + p_new + jnp.exp(sink - m)
    o = (_dot_nt(p.astype(BF16), buf_ref[0, 1].reshape(kw, wb).astype(BF16)) + p_new * new[:, kw:2 * kw]) / den
    o_ref[0] = _fold_heads(o, SWA_KV_HEADS)


def _swa_decode(buf_t, new, qbd, sink):
    db = buf_t.shape[0]
    wb = buf_t.shape[-1]
    kw = SWA_KV_HEADS * HEAD_DIM
    return pl.pallas_call(
        _swa_decode_kernel,
        grid=(db,),
        in_specs=[pl.BlockSpec((1, 2, SWA_KV_HEADS, HEAD_DIM, wb), lambda b: (b, 0, 0, 0, 0)),
                  pl.BlockSpec((1, 1, 2 * kw), lambda b: (b, 0, 0)),
                  pl.BlockSpec((1, N_HEADS, kw), lambda b: (b, 0, 0)),
                  pl.BlockSpec((N_HEADS, LANES), lambda b: (0, 0))],
        out_specs=pl.BlockSpec((1, N_HEADS, kw), lambda b: (b, 0, 0)),
        out_shape=jax.ShapeDtypeStruct((db, N_HEADS, kw), F32),
        compiler_params=_cparams(1),
        name="swa_decode",
    )(buf_t, new, qbd, sink)


def _nsa_decode_kernel(pt_ref, *refs, n_pages):
    page_refs = refs[:n_pages]
    (kcvc_ref, win_ref, qraw_ref, qrot_ref, newslc_ref, newwin_ref, gates_ref, o_ref, s_sc) = refs[n_pages:]
    kw = NSA_KV_HEADS * HEAD_DIM
    grp = N_HEADS // NSA_KV_HEADS
    blocks_per_page = PAGE_SIZE // NSA_BLOCK
    n_blk = n_pages * blocks_per_page
    wb = win_ref.shape[-1]
    q_raw = (qraw_ref[0] * ATTN_SCALE).astype(BF16)
    q_rot = qrot_ref[0] * ATTN_SCALE
    q_rot_b = q_rot.astype(BF16)

    kcvc = kcvc_ref[0]
    s_c = _dot_nt(q_raw, kcvc[:, 0:kw].astype(BF16))
    p_c = jnp.exp(s_c - jnp.max(s_c, axis=1, keepdims=True))
    p_c = p_c / jnp.sum(p_c, axis=1, keepdims=True)
    o_cmp = _dot(p_c.astype(BF16), kcvc[:, kw:2 * kw].astype(BF16))

    head = _iota(p_c.shape, 0)
    sc0 = jnp.sum(jnp.where(head < grp, p_c, 0.0), axis=0, keepdims=True)
    sc1 = jnp.sum(jnp.where(head >= grp, p_c, 0.0), axis=0, keepdims=True)
    r8 = _iota((SUBLANES, n_blk), 0)
    n_io = _iota((SUBLANES, n_blk), 1)
    sc = jnp.where(r8 == 0, sc0, jnp.where(r8 == 1, sc1, -jnp.inf))
    forced = (n_io == 0) | (n_io == n_blk - 1)
    sc = jnp.where(forced, jnp.inf, sc)
    sel = forced | (_rank_desc(sc, n_blk, n_io) < NSA_TOPN - 1)
    bias2 = jnp.where(sel, 0.0, MASKED)
    bias = jnp.where(head < grp, bias2[0:1], bias2[1:2])

    new_slc = newslc_ref[0]
    s_new = jnp.sum(q_rot * new_slc[:, 0:kw], axis=1, keepdims=True)
    m = s_new
    lane = _iota((N_HEADS, PAGE_SIZE), 1)
    k_all = jnp.concatenate([page_refs[j][0].reshape(kw, PAGE_SIZE).astype(BF16) for j in range(n_pages)], axis=1)
    v_all = jnp.concatenate([page_refs[j][1].reshape(kw, PAGE_SIZE).astype(BF16) for j in range(n_pages)], axis=1)
    s_all = _dot(q_rot_b, k_all)
    for j in range(n_pages):
        bj = bias[:, blocks_per_page * j:blocks_per_page * j + 1]
        for u in range(1, blocks_per_page):
            bj = jnp.where(lane < u * NSA_BLOCK, bj, bias[:, blocks_per_page * j + u:blocks_per_page * j + u + 1])
        s_sc[:, j * PAGE_SIZE:(j + 1) * PAGE_SIZE] = s_all[:, j * PAGE_SIZE:(j + 1) * PAGE_SIZE] + bj
    s_all = s_sc[...]
    m = jnp.maximum(m, jnp.max(s_all, axis=1, keepdims=True))
    p = jnp.exp(s_all - m)
    p_new = jnp.exp(s_new - m)
    den = jnp.sum(p, axis=1, keepdims=True) + p_new
    o_slc = (_dot_nt(p.astype(BF16), v_all) + p_new * new_slc[:, kw:2 * kw]) / den

    new_win = newwin_ref[0]
    s_w = _dot(q_rot_b, win_ref[0, 0].reshape(kw, wb).astype(BF16))
    s_wn = jnp.sum(q_rot * new_win[:, 0:kw], axis=1, keepdims=True)
    m_w = jnp.maximum(jnp.max(s_w, axis=1, keepdims=True), s_wn)
    p_w = jnp.exp(s_w - m_w)
    p_wn = jnp.exp(s_wn - m_w)
    den_w = jnp.sum(p_w, axis=1, keepdims=True) + p_wn
    o_win = (_dot_nt(p_w.astype(BF16), win_ref[0, 1].reshape(kw, wb).astype(BF16))
             + p_wn * new_win[:, kw:2 * kw]) / den_w

    g = gates_ref[0]
    o = g[:, 0:1] * o_cmp + g[:, 1:2] * o_slc + g[:, 2:3] * o_win
    o_ref[0] = _fold_heads(o, NSA_KV_HEADS)


def _nsa_decode(pt, cache_t, kcvc, win_t, q_raw_bd, q_rot_bd, new_rows, new_win, gates):
    db, n_pages = pt.shape
    kw = NSA_KV_HEADS * HEAD_DIM
    n_blk = kcvc.shape[1]
    wb = win_t.shape[-1]
    per_b = lambda shape: pl.BlockSpec((1,) + shape, lambda b, pt: (b,) + (0,) * len(shape))
    grid_spec = pltpu.PrefetchScalarGridSpec(
        num_scalar_prefetch=1,
        grid=(db,),
        in_specs=[_nsa_page_spec(j, 1) for j in range(n_pages)]
        + [per_b((n_blk, 2 * kw)), per_b((2, NSA_KV_HEADS, HEAD_DIM, wb)), per_b((N_HEADS, kw)), per_b((N_HEADS, kw)),
           pl.BlockSpec((1, 1, 2 * kw), lambda b, pt: (b, 0, 1)), per_b((1, 2 * kw)), per_b((N_HEADS, LANES))],
        out_specs=per_b((N_HEADS, kw)),
        scratch_shapes=[pltpu.VMEM((N_HEADS, n_pages * PAGE_SIZE), F32)],
    )
    return pl.pallas_call(
        functools.partial(_nsa_decode_kernel, n_pages=n_pages),
        grid_spec=grid_spec,
        out_shape=jax.ShapeDtypeStruct((db, N_HEADS, kw), F32),
        compiler_params=_cparams(1),
        name="nsa_decode",
    )(pt, *([cache_t] * n_pages), kcvc, win_t, q_raw_bd, q_rot_bd, new_rows, new_win, gates)


def _rope_tables(pos):
    half = HEAD_DIM // 2
    inv = ROPE_THETA ** (-jnp.arange(half, dtype=F32) / half)
    ang = pos.astype(F32)[:, None] * inv[None, :]
    cos = jnp.cos(ang)
    sin = jnp.sin(ang)
    reps = LANES // HEAD_DIM
    return jnp.tile(cos, (1, 2 * reps)), jnp.tile(jnp.concatenate([-sin, sin], 1), (1, reps))


def _block_diag_heads(q, kvh_n):
    r = q.shape[0]
    grp = N_HEADS // kvh_n
    own = (jnp.arange(N_HEADS)[:, None] // grp == jnp.arange(kvh_n)[None, :]).astype(q.dtype)
    q3 = q.reshape(r, N_HEADS, 1, HEAD_DIM) * own[None, :, :, None]
    return q3.reshape(r, N_HEADS, kvh_n * HEAD_DIM)


def _heads_from_folded(o):
    return o[:, :, :HEAD_DIM].reshape(1, o.shape[0], Q_WIDTH)


def _pad_cols(w, n):
    return jnp.pad(w, ((0, 0), (0, n - w.shape[1])))


def _block_diag(blocks):
    rows = sum(b.shape[0] for b in blocks)
    cols = sum(b.shape[1] for b in blocks)
    out = jnp.zeros((rows, cols), blocks[0].dtype)
    r = c = 0
    for b in blocks:
        out = out.at[r:r + b.shape[0], c:c + b.shape[1]].set(b)
        r += b.shape[0]
        c += b.shape[1]
    return out


def _positions_minor(x):
    nd = x.ndim
    return jnp.transpose(x, tuple(range(nd - 4)) + (nd - 3, nd - 2, nd - 1, nd - 4))


def _nsa_compress_weights_seq(pe_k, pe_v, w1_k, w2_k, w1_v, w2_v):
    pe2 = jnp.concatenate([pe_k, pe_k, pe_v, pe_v], axis=1)

    def pair(w1):
        w = w1.reshape(NSA_BLOCK, HEAD_DIM, NSA_CMP_HIDDEN)
        z = jnp.zeros_like(w)
        return jnp.concatenate([jnp.concatenate([w, z], 2), jnp.concatenate([z, w], 2)], 1).astype(BF16)

    w2 = _block_diag([w2_k, w2_k, w2_v, w2_v]).astype(BF16)
    return pe2, pair(w1_k), pair(w1_v), w2


def _nsa_compress_weights_pages(pe_k, pe_v, w1_k, w2_k, w1_v, w2_v):
    reps = PAGE_SIZE // NSA_BLOCK

    def pair(w1):
        w = w1.reshape(NSA_BLOCK, HEAD_DIM, NSA_CMP_HIDDEN).transpose(1, 0, 2)
        z = jnp.zeros_like(w)
        w = jnp.concatenate([jnp.concatenate([w, z], 2), jnp.concatenate([z, w], 2)], 1)
        return w.reshape(HEAD_DIM * PAGE_SIZE, -1).astype(BF16)

    return (jnp.tile(pe_k.T, (1, reps)), jnp.tile(pe_v.T, (1, reps)), pair(w1_k), pair(w1_v),
            _block_diag([w2_k] * reps).astype(BF16), _block_diag([w2_v] * reps).astype(BF16))


_MOBA_OUTS = (((0, Q_WIDTH, "rope"),),
              ((Q_WIDTH, 256, "rope"), (Q_WIDTH + 256, 256, "raw")))
_SWA_OUTS = (((0, Q_WIDTH, "rope"),),
             ((Q_WIDTH, 128, "rope"), (Q_WIDTH + 128, 128, "raw")))
_NSA_OUTS = (((0, Q_WIDTH, "raw"),),
             ((0, Q_WIDTH, "rope"),),
             ((Q_WIDTH, 128, "raw"), (Q_WIDTH + 128, 128, "raw"), (Q_WIDTH + 256, 128, "rope"),
              (Q_WIDTH + 384, 128, "raw")),
             ((Q_WIDTH + 512, 128, "rope"), (Q_WIDTH + 640, 128, "raw")),
             ((Q_WIDTH + 768, 128, "sigmoid"),))
_NSA_IN_PADDED = Q_WIDTH + 7 * 128


def kernel(x_prompt, x_sample, cache_moba_kv, state_swa_kv, cache_nsa_kv, state_nsa_win_kv, state_ffn_conv, page_table, c_prompt, c_sample, ada_w, ada_b, ln_g, ln_b, moba_w_in, moba_w_o, swa_w_in, swa_w_o, swa_sink, nsa_w_in, nsa_w_o, nsa_pe_k, nsa_pe_v, nsa_w1_k, nsa_w2_k, nsa_w1_v, nsa_w2_v, ffn_w_gate, ffn_w_up, ffn_conv_w, ffn_conv_b, ffn_w_down):
    b, s, d = x_prompt.shape
    db, dec_seq, _ = x_sample.shape
    depth = ada_w.shape[0]
    ff = ffn_w_gate.shape[2]
    n_pages = page_table.shape[1]
    n_phys = cache_moba_kv.shape[1]
    past_len = n_pages * PAGE_SIZE
    assert dec_seq == 1 and past_len % MOBA_BLOCK == 0 and s % 512 == 0
    assert state_swa_kv.shape[2] <= SWA_WINDOW and state_nsa_win_kv.shape[2] <= NSA_WINDOW
    alpha = (2 * depth) ** 0.25
    tm = 512
    tf = ff // 2

    rows = b + db
    rows_pad = -(-rows // SUBLANES) * SUBLANES
    c_all = jnp.concatenate([c_prompt, c_sample, jnp.zeros((rows_pad - rows, d), F32)], 0)
    mod = _adaln_all(c_all, ada_w.reshape(depth * 2, d, 3 * d), ada_b.reshape(depth * 2, 1, 3 * d))

    def modulation(i, sub):
        m = mod[i * 2 + sub]
        mp = m[:b].reshape(b, 1, 3 * d)
        ms = m[b:b + db].reshape(1, db, 3 * d)
        return ([mp[..., k * d:(k + 1) * d] for k in range(3)], [ms[..., k * d:(k + 1) * d] for k in range(3)])

    cos_p, sin_p = _rope_tables(jnp.arange(s, dtype=jnp.int32))
    cos_s, sin_s = _rope_tables(jnp.full((db,), past_len, jnp.int32))

    moba_cache_t = _positions_minor(cache_moba_kv).reshape(-1, 2, MOBA_KV_HEADS, HEAD_DIM, PAGE_SIZE)
    nsa_cache_t = _positions_minor(cache_nsa_kv).reshape(-1, 4, NSA_KV_HEADS, HEAD_DIM, PAGE_SIZE)

    xp = x_prompt
    xs = x_sample.reshape(1, db, d)
    moba_p, moba_s, swa_p, swa_s, nsa_p, nsa_s, nsaw_p, nsaw_s, conv_p, conv_s = ([] for _ in range(10))
    for i in range(depth):
        kind, j = i % N_MIXERS, i // N_MIXERS
        (sh_p, sc_p, gt_p), (sh_s, sc_s, gt_s) = modulation(i, 0)
        if kind == 0:
            w_in = moba_w_in[j].astype(BF16)
            q_p, kv_p, vt_p = _inproj(xp, sh_p, sc_p, w_in, cos_p, sin_p, _MOBA_OUTS, tm,
                                      w_in[:, Q_WIDTH + 256:Q_WIDTH + 512].T)
            q_s, kv_s = _inproj(xs, sh_s, sc_s, w_in, cos_s, sin_s, _MOBA_OUTS, db)
            o_p = _moba_prompt(q_p, kv_p, vt_p)
            o_s = _heads_from_folded(_moba_decode(page_table + j * n_phys, moba_cache_t,
                                                  _block_diag_heads(q_s[0], MOBA_KV_HEADS), kv_s.reshape(db, 1, -1)))
            moba_p.append(kv_p.reshape(b, s, 2, MOBA_KV_HEADS, HEAD_DIM))
            moba_s.append(kv_s.reshape(db, 1, 2, MOBA_KV_HEADS, HEAD_DIM))
            w_o = moba_w_o[j]
        elif kind == 1:
            w_in = swa_w_in[j].astype(BF16)
            q_p, kv_p, vt_p = _inproj(xp, sh_p, sc_p, w_in, cos_p, sin_p, _SWA_OUTS, tm,
                                      w_in[:, Q_WIDTH + 128:Q_WIDTH + 256].T)
            q_s, kv_s = _inproj(xs, sh_s, sc_s, w_in, cos_s, sin_s, _SWA_OUTS, db)
            sink = swa_sink[j]
            o_p = _swa_prompt(q_p, kv_p, vt_p, _pad_cols(sink[None, :], LANES))
            buf = state_swa_kv[j]
            new = kv_s.reshape(db, 1, -1)
            o_s = _heads_from_folded(_swa_decode(_positions_minor(buf), new, _block_diag_heads(q_s[0], SWA_KV_HEADS),
                                                 jnp.broadcast_to(sink[:, None], (N_HEADS, LANES))))
            swa_p.append(kv_p[:, s - min(SWA_WINDOW, s):].reshape(b, -1, 2, SWA_KV_HEADS, HEAD_DIM))
            swa_s.append(jnp.concatenate([buf[:, 1:], new.reshape(db, 1, 2, SWA_KV_HEADS, HEAD_DIM)], 1))
            w_o = swa_w_o[j]
        else:
            w_in = _pad_cols(nsa_w_in[j], _NSA_IN_PADDED).astype(BF16)
            nsa_w = (nsa_pe_k[j], nsa_pe_v[j], nsa_w1_k[j], nsa_w2_k[j], nsa_w1_v[j], nsa_w2_v[j])
            w_t = jnp.concatenate([w_in[:, Q_WIDTH + 384:Q_WIDTH + 512], w_in[:, Q_WIDTH + 640:Q_WIDTH + 768]], 1).T
            qraw_p, qrot_p, rows_p, win_p, gates_p, vt_p = _inproj(xp, sh_p, sc_p, w_in, cos_p, sin_p, _NSA_OUTS,
                                                                   tm, w_t)
            qraw_s, qrot_s, rows_s, win_s, gates_s = _inproj(xs, sh_s, sc_s, w_in, cos_s, sin_s, _NSA_OUTS, db)
            n_blk = s // NSA_BLOCK
            cmp_p = _nsa_compress_seq(rows_p.reshape(b * s, -1), *_nsa_compress_weights_seq(*nsa_w))
            cmp_p = cmp_p.reshape(b, n_blk, 2, NSA_KV_HEADS, HEAD_DIM).transpose(2, 0, 3, 1, 4)
            cmp_p = jnp.pad(cmp_p, ((0, 0), (0, 0), (0, 0), (0, HEAD_DIM - n_blk), (0, 0)))
            cmp_p = cmp_p.reshape(2, b, NSA_KV_HEADS * HEAD_DIM, HEAD_DIM)
            kc_p = jnp.pad(cmp_p[0], ((0, 0), (0, 0), (0, LANES - HEAD_DIM)))
            vct_p = cmp_p[1].transpose(0, 2, 1)
            o_p = _nsa_prompt(qraw_p, qrot_p, rows_p, win_p, vt_p, kc_p, vct_p, gates_p)
            pt = page_table + j * n_phys
            kcvc = _nsa_compress_pages(pt, nsa_cache_t, *_nsa_compress_weights_pages(*nsa_w))
            kcvc = kcvc.reshape(db, n_pages, NSA_KV_HEADS, 2, PAGE_SIZE // NSA_BLOCK, HEAD_DIM)
            kcvc = kcvc.transpose(0, 1, 4, 3, 2, 5).reshape(db, -1, 2 * NSA_KV_HEADS * HEAD_DIM)
            win_buf = state_nsa_win_kv[j]
            new_win = win_s.reshape(db, 1, -1)
            gates3 = jnp.pad(gates_s[0, :, :3 * N_HEADS].reshape(db, N_HEADS, 3), ((0, 0), (0, 0), (0, LANES - 3)))
            o_s = _heads_from_folded(_nsa_decode(pt, nsa_cache_t, kcvc, _positions_minor(win_buf),
                                                 _block_diag_heads(qraw_s[0], NSA_KV_HEADS),
                                                 _block_diag_heads(qrot_s[0], NSA_KV_HEADS),
                                                 rows_s.reshape(db, 1, -1), new_win, gates3))
            nsa_p.append(rows_p.reshape(b, s, 4, NSA_KV_HEADS, HEAD_DIM))
            nsa_s.append(rows_s.reshape(db, 1, 4, NSA_KV_HEADS, HEAD_DIM))
            nsaw_p.append(win_p[:, s - min(NSA_WINDOW, s):].reshape(b, -1, 2, NSA_KV_HEADS, HEAD_DIM))
            nsaw_s.append(jnp.concatenate([win_buf[:, 1:], new_win.reshape(db, 1, 2, NSA_KV_HEADS, HEAD_DIM)], 1))
            w_o = nsa_w_o[j]
        w_o = w_o.astype(BF16)
        g0, b0 = ln_g[i, 0][None, :], ln_b[i, 0][None, :]
        xp = _outproj_ln(o_p, w_o, xp, gt_p, g0, b0, alpha, tm)
        xs = _outproj_ln(o_s, w_o, xs, gt_s, g0, b0, alpha, db)

        (sh_p, sc_p, gt_p), (sh_s, sc_s, gt_s) = modulation(i, 1)
        wg, wu, wd = ffn_w_gate[i].astype(BF16), ffn_w_up[i].astype(BF16), ffn_w_down[i].astype(BF16)
        cwt, cbs = ffn_conv_w[i], ffn_conv_b[i][None, :]
        g1, b1 = ln_g[i, 1][None, :], ln_b[i, 1][None, :]
        xp, st_p = _ffn_seq(xp, sh_p, sc_p, gt_p, jnp.zeros((b, CONV_WIDTH - 1, ff), F32), wg, wu, cwt, cbs, wd,
                            g1, b1, alpha, tm)
        prev = state_ffn_conv[i]
        xs2, g_new = _ffn_step(xs[0], sh_s[0], sc_s[0], gt_s[0], prev[:, 0], prev[:, 1], wg, wu, cwt, cbs, wd,
                               g1, b1, alpha, tf)
        xs = xs2[None]
        conv_p.append(st_p)
        conv_s.append(jnp.stack([prev[:, 1], g_new], 1))
    return (xp, xs.reshape(db, 1, d), jnp.stack(moba_p), jnp.stack(moba_s), jnp.stack(swa_p), jnp.stack(swa_s),
            jnp.stack(nsa_p), jnp.stack(nsa_s), jnp.stack(nsaw_p), jnp.stack(nsaw_s),
            jnp.stack(conv_p), jnp.stack(conv_s))
```
